```python
import math
import jax
import jax.numpy as jnp
from jax import lax
import numpy as np

D_MODEL = 1024
BATCH = 16
SEQ = 2048
DEPTH = 2

CTX_LEN = 256
GRID_W = 64
MIX_WIDTH = D_MODEL
GROUP_W = MIX_WIDTH // 4
HEAD = 64
N_RWKV_HEADS = GROUP_W // HEAD
DECAY_LORA = 64
AAA_LORA = 64
GATE_LORA = 128
LORA_COLS = 2 * DECAY_LORA + 2 * AAA_LORA + GATE_LORA
LORA_SPLITS = (DECAY_LORA, 2 * DECAY_LORA, 2 * DECAY_LORA + AAA_LORA, 2 * DECAY_LORA + 2 * AAA_LORA)
POOL_WINDOWS = (2, 4, 8, 16)
POOL_GROUPS = 4
POOL_CH = GROUP_W // POOL_GROUPS
CHUNK = 128
GMLP_GROUPS = 4
GMLP_CH = GROUP_W // GMLP_GROUPS
FNET_GROUPS = 4
FNET_CH = GROUP_W // FNET_GROUPS
RWKV_COLS = 3 * GROUP_W + LORA_COLS
IN_COLS = RWKV_COLS + GROUP_W + 2 * GROUP_W + GROUP_W
IN_SPLITS = (3 * GROUP_W, RWKV_COLS, RWKV_COLS + GROUP_W, RWKV_COLS + 3 * GROUP_W)
D_FF = -(-8 * D_MODEL // (3 * 256)) * 256
ALPHA = (2 * DEPTH) ** 0.25
BETA = (8 * DEPTH) ** -0.25
LN_EPS = 1e-5
GN_EPS = 64e-5

kernel_name = 'hybrid_rwkv_pool_gmlp_fnet_dit'


def layer_norm(x, g, b, eps=LN_EPS):
    xf = x.astype(jnp.float32)
    mu = jnp.mean(xf, -1, keepdims=True)
    var = jnp.mean(jnp.square(xf - mu), -1, keepdims=True)
    return ((xf - mu) * lax.rsqrt(var + eps) * g.astype(jnp.float32) + b.astype(jnp.float32)).astype(x.dtype)


def pos_embed_2d(n_tok, dim, dtype):
    rows = n_tok // GRID_W
    row, col = jnp.meshgrid(jnp.arange(rows, dtype=jnp.float32), jnp.arange(GRID_W, dtype=jnp.float32), indexing='ij')
    quarter = dim // 4
    freqs = jnp.exp(-math.log(10000.0) * jnp.arange(quarter, dtype=jnp.float32) / quarter)
    def enc(p):
        ang = p.reshape(-1, 1) * freqs[None, :]
        return jnp.concatenate([jnp.sin(ang), jnp.cos(ang)], -1)
    return jnp.concatenate([enc(row), enc(col)], -1).astype(dtype)


def modulation(cond, w, b):
    m = (jax.nn.silu(cond) @ w + b)[..., None, :]
    return jnp.split(m, 6, axis=-1)


def short_conv(z, w):
    zp = jnp.pad(z, ((0, 0), (1, 1), (0, 0)))
    return zp[:, :-2] * w[0] + zp[:, 1:-1] * w[1] + zp[:, 2:] * w[2]


def rwkv_prep(z_rkv, z_lora, conv, w0, w2, a0, a2, k_k, k_a):
    B, L, _ = z_rkv.shape
    def heads(t):
        return t.reshape(B, L, N_RWKV_HEADS, HEAD).astype(jnp.float32)
    r, k, v = jnp.split(short_conv(z_rkv, conv), 3, axis=-1)
    dw_f, dw_b, da_f, da_b, dg = jnp.split(z_lora, LORA_SPLITS, axis=-1)
    kk = heads(k * k_k)
    kk = kk / jnp.maximum(jnp.sqrt(jnp.sum(kk * kk, -1, keepdims=True)), 1e-12)
    kf = k.astype(jnp.float32)
    dirs = []
    for d, (dw, da) in enumerate(((dw_f, da_f), (dw_b, da_b))):
        logw = -jax.nn.softplus(-(w0[d] + jnp.tanh(dw) @ w2[d]).astype(jnp.float32)) - 0.5
        decay = jnp.exp(-jnp.exp(logw))
        a = jax.nn.sigmoid((a0[d] + da @ a2[d]).astype(jnp.float32))
        k_d = kf * (1.0 + (a - 1.0) * k_a.astype(jnp.float32))
        dirs.append((heads(k_d), heads(decay), heads(a)))
    return heads(r), heads(v), kk, dg, dirs


def rwkv_scan(prep, direction, s0):
    r, v, kk, _, dirs = prep
    k, decay, a = dirs[direction]
    def step(S, inp):
        r_t, k_t, v_t, w_t, kk_t, a_t = inp
        s_kk = jnp.einsum('bhvk,bhk->bhv', S, -kk_t)
        S = (S * w_t[:, :, None, :]
             + s_kk[..., None] * (kk_t * a_t)[:, :, None, :]
             + v_t[..., None] * k_t[:, :, None, :])
        return S, jnp.einsum('bhvk,bhk->bhv', S, r_t)
    xs = tuple(jnp.moveaxis(t, 1, 0) for t in (r, k, v, decay, kk, a))
    s_final, ys = lax.scan(step, s0, xs, reverse=(direction == 1))
    return jnp.moveaxis(ys, 0, 1), s_final


def rwkv_readout(prep, y_f, y_b, r_k, gn_g, gn_b, gate_g2):
    r, v, kk, dg, dirs = prep
    B, L, H, N = r.shape
    y = y_f + y_b
    mu = jnp.mean(y, -1, keepdims=True)
    var = jnp.mean(jnp.square(y - mu), -1, keepdims=True)
    y = (y - mu) * lax.rsqrt(var + GN_EPS)
    bonus = jnp.sum(r * (dirs[0][0] + dirs[1][0]) * r_k.astype(jnp.float32), -1, keepdims=True) * v
    y = y.reshape(B, L, H * N) * gn_g.astype(jnp.float32) + gn_b.astype(jnp.float32) + bonus.reshape(B, L, H * N)
    gate = jax.nn.sigmoid(dg) @ gate_g2
    return (y * gate.astype(jnp.float32)).astype(dg.dtype)


def pool_mixer(z, pool_w, pool_scale):
    B, L, _ = z.shape
    zg = z.reshape(B, L, POOL_GROUPS, POOL_CH).astype(jnp.float32)
    cs = jnp.pad(jnp.cumsum(zg, axis=1), ((0, 0), (1, 0), (0, 0), (0, 0)))
    pos = jnp.arange(L)
    pooled = []
    for gi, w in enumerate(POOL_WINDOWS):
        lo = jnp.clip(pos - w // 2, 0, L)
        hi = jnp.clip(pos + w - w // 2, 0, L)
        cnt = (hi - lo).astype(jnp.float32)[None, :, None]
        pooled.append((cs[:, hi, gi] - cs[:, lo, gi]) / cnt)
    d = (jnp.stack(pooled, axis=2) - zg).astype(z.dtype)
    y = jnp.einsum('blgc,gcd->blgd', d, pool_w)
    return y.reshape(B, L, GROUP_W) * pool_scale


def gmlp_mixer(z, ln_g, ln_b, ws, bs):
    B, L, _ = z.shape
    u, v = jnp.split(jax.nn.gelu(z), 2, axis=-1)
    v = v.reshape(B, L // CHUNK, CHUNK, GMLP_GROUPS, GMLP_CH)
    v = layer_norm(v, ln_g.reshape(GMLP_GROUPS, GMLP_CH), ln_b.reshape(GMLP_GROUPS, GMLP_CH))
    sv = jnp.einsum('gpq,bnqgc->bnpgc', ws, v) + bs.T[None, None, :, :, None]
    return u * sv.reshape(B, L, GROUP_W)


def fourier_mixer(z, fnet_w, fnet_b):
    B, L, _ = z.shape
    zg = jnp.transpose(z.reshape(B, L, FNET_GROUPS, FNET_CH).astype(jnp.float32), (0, 2, 1, 3))
    f = jnp.transpose(jnp.fft.fft2(zg, norm='ortho').real, (0, 2, 1, 3)).astype(z.dtype)
    return jnp.einsum('blgc,gcd->blgd', f, fnet_w).reshape(B, L, GROUP_W) + fnet_b


def mix_out(parts, prep, y_f, y_b, r_k, gn_g, gn_b, gate_g2, pool_w, pool_scale,
            gmlp_ln_g, gmlp_ln_b, gmlp_ws, gmlp_bs, fnet_w, fnet_b, w_out):
    a = rwkv_readout(prep, y_f, y_b, r_k, gn_g, gn_b, gate_g2)
    b = pool_mixer(parts[2], pool_w, pool_scale)
    g = gmlp_mixer(parts[3], gmlp_ln_g, gmlp_ln_b, gmlp_ws, gmlp_bs)
    f = fourier_mixer(parts[4], fnet_w, fnet_b)
    return jnp.concatenate([a, b, g, f], axis=-1) @ w_out


def swiglu(h, w1, w2):
    gate, up = jnp.split(h @ w1, 2, axis=-1)
    return (jax.nn.silu(gate) * up) @ w2


def setup_inputs(seed: int = 0) -> dict:
    key = jax.random.key(seed)
    ks = iter(jax.random.split(key, 40))
    def nrm(shape, s):
        return s * jax.random.normal(next(ks), shape, jnp.float32)
    D = D_MODEL
    centre = jnp.asarray([0.0, 1.0, 0.0], jnp.float32)[None, :, None]
    return {
        'x': nrm((BATCH, SEQ, D), 1.0),
        'c': nrm((BATCH, D), 1.0),
        'ctx': nrm((BATCH, CTX_LEN, D), 1.0),
        'c_ctx': nrm((D,), 1.0),
        'w_mod': nrm((DEPTH, D, 6 * D), 0.5 * D ** -0.5),
        'b_mod': nrm((DEPTH, 6 * D), 0.01),
        'w_in': nrm((DEPTH, D, IN_COLS), D ** -0.5),
        'rkv_conv': centre + nrm((DEPTH, 3, 3 * GROUP_W), 0.3),
        'decay_w0': nrm((DEPTH, 2, GROUP_W), 1.0),
        'decay_w2': nrm((DEPTH, 2, DECAY_LORA, GROUP_W), 0.5 * DECAY_LORA ** -0.5),
        'iclr_a0': nrm((DEPTH, 2, GROUP_W), 0.5),
        'iclr_a2': nrm((DEPTH, 2, AAA_LORA, GROUP_W), 0.5 * AAA_LORA ** -0.5),
        'gate_g2': nrm((DEPTH, GATE_LORA, GROUP_W), GATE_LORA ** -0.5),
        'k_k': 0.85 + nrm((DEPTH, GROUP_W), 0.05),
        'k_a': 1.0 + nrm((DEPTH, GROUP_W), 0.05),
        'r_k': nrm((DEPTH, N_RWKV_HEADS, HEAD), 0.1),
        'gn_g': 1.0 + nrm((DEPTH, GROUP_W), 0.05),
        'gn_b': nrm((DEPTH, GROUP_W), 0.01),
        'pool_w': nrm((DEPTH, POOL_GROUPS, POOL_CH, POOL_CH), POOL_CH ** -0.5),
        'pool_scale': 1.0 + nrm((DEPTH, GROUP_W), 0.1),
        'gmlp_ln_g': 1.0 + nrm((DEPTH, GROUP_W), 0.05),
        'gmlp_ln_b': nrm((DEPTH, GROUP_W), 0.01),
        'gmlp_ws': nrm((DEPTH, GMLP_GROUPS, CHUNK, CHUNK), CHUNK ** -0.5),
        'gmlp_bs': 1.0 + nrm((DEPTH, GMLP_GROUPS, CHUNK), 0.1),
        'fnet_w': nrm((DEPTH, FNET_GROUPS, FNET_CH, FNET_CH), FNET_CH ** -0.5),
        'fnet_b': nrm((DEPTH, GROUP_W), 0.01),
        'w_out': nrm((DEPTH, MIX_WIDTH, D), BETA * MIX_WIDTH ** -0.5),
        'ln1_g': 1.0 + nrm((DEPTH, D), 0.1),
        'ln1_b': nrm((DEPTH, D), 0.01),
        'ln2_g': 1.0 + nrm((DEPTH, D), 0.1),
        'ln2_b': nrm((DEPTH, D), 0.01),
        'ffn_w1': nrm((DEPTH, D, 2 * D_FF), D ** -0.5),
        'ffn_w2': nrm((DEPTH, D_FF, D), BETA * D_FF ** -0.5),
    }


def reference(x, c, ctx, c_ctx, w_mod, b_mod, w_in, rkv_conv, decay_w0, decay_w2, iclr_a0, iclr_a2,
              gate_g2, k_k, k_a, r_k, gn_g, gn_b, pool_w, pool_scale, gmlp_ln_g, gmlp_ln_b, gmlp_ws,
              gmlp_bs, fnet_w, fnet_b, w_out, ln1_g, ln1_b, ln2_g, ln2_b, ffn_w1, ffn_w2):
    B, L, D = x.shape
    xs = x + pos_embed_2d(L, D, x.dtype)[None]
    cs = ctx
    s0 = jnp.zeros((B, N_RWKV_HEADS, HEAD, HEAD), jnp.float32)
    for l in range(DEPTH):
        last = l == DEPTH - 1
        sh1, sc1, g1, sh2, sc2, g2 = modulation(c, w_mod[l], b_mod[l])
        csh1, csc1, cg1, csh2, csc2, cg2 = modulation(c_ctx[None], w_mod[l], b_mod[l])
        hx = xs * (1.0 + sc1) + sh1
        hc = cs * (1.0 + csc1) + csh1
        px = jnp.split(hx @ w_in[l], IN_SPLITS, axis=-1)
        if last:
            pc = jnp.split(hc @ w_in[l][:, :RWKV_COLS], (3 * GROUP_W,), axis=-1)
        else:
            pc = jnp.split(hc @ w_in[l], IN_SPLITS, axis=-1)
        rw = (rkv_conv[l], decay_w0[l], decay_w2[l], iclr_a0[l], iclr_a2[l], k_k[l], k_a[l])
        prep_c = rwkv_prep(pc[0], pc[1], *rw)
        prep_x = rwkv_prep(px[0], px[1], *rw)
        yc_f, sc_f = rwkv_scan(prep_c, 0, s0)
        yc_b, sc_b = rwkv_scan(prep_c, 1, s0)
        yx_f, _ = rwkv_scan(prep_x, 0, sc_f)
        yx_b, _ = rwkv_scan(prep_x, 1, sc_b)
        op = (r_k[l], gn_g[l], gn_b[l], gate_g2[l], pool_w[l], pool_scale[l], gmlp_ln_g[l], gmlp_ln_b[l],
              gmlp_ws[l], gmlp_bs[l], fnet_w[l], fnet_b[l], w_out[l])
        mix_x = mix_out(px, prep_x, yx_f, yx_b, *op)
        xs = layer_norm(ALPHA * xs + g1 * mix_x, ln1_g[l], ln1_b[l])
        xs = layer_norm(ALPHA * xs + g2 * swiglu(xs * (1.0 + sc2) + sh2, ffn_w1[l], ffn_w2[l]), ln2_g[l], ln2_b[l])
        if not last:
            mix_c = mix_out(pc, prep_c, yc_f, yc_b, *op)
            cs = layer_norm(ALPHA * cs + cg1 * mix_c, ln1_g[l], ln1_b[l])
            cs = layer_norm(ALPHA * cs + cg2 * swiglu(cs * (1.0 + csc2) + csh2, ffn_w1[l], ffn_w2[l]), ln2_g[l], ln2_b[l])
    return xs
```

```python
import functools
import math

import jax
import jax.numpy as jnp
from jax import lax
from jax.experimental import pallas as pl
from jax.experimental.pallas import tpu as pltpu

F32 = jnp.float32
BF16 = jnp.bfloat16

V7X_LANES = 128
V7X_SUBLANES = 8
V7X_VMEM_BYTES = 64 * 1024 * 1024
V7X_VMEM_LIMIT_CAP = 56 * 1024 * 1024

GRID_W = 64
HEAD = 64
GROUP_W = 256
N_HEADS = GROUP_W // HEAD
LORA_W = 384
POOL_WINDOWS = (2, 4, 8, 16)
CHUNK = 128
LN_EPS = 1e-5
GN_EPS = 64e-5
HI = lax.Precision.HIGHEST


def _cparams(sem, vmem_bytes):
    limit = int(min(max(vmem_bytes, 16 * 1024 * 1024), V7X_VMEM_LIMIT_CAP))
    return pltpu.CompilerParams(dimension_semantics=sem, vmem_limit_bytes=limit)


def _silu(x):
    return x * jax.nn.sigmoid(x)


def _seg_sum(x, ones_bd):
    return jnp.dot(x, ones_bd, precision=HI, preferred_element_type=F32)


def _layer_norm(z, g, b):
    mu = jnp.mean(z, axis=-1, keepdims=True)
    zc = z - mu
    var = jnp.mean(zc * zc, axis=-1, keepdims=True)
    return zc * lax.rsqrt(var + LN_EPS) * g + b


def _mod_kernel(c_ref, w_ref, b_ref, o_ref):
    s = _silu(c_ref[...])
    o_ref[...] = jnp.dot(s.astype(BF16), w_ref[...], preferred_element_type=F32) + b_ref[...]


def _modulation(c_all, w_mod, b_mod):
    rows, d = c_all.shape
    n = w_mod.shape[1]
    tn = 1024
    return pl.pallas_call(
        _mod_kernel,
        grid=(n // tn,),
        in_specs=[pl.BlockSpec((rows, d), lambda j: (0, 0)),
                  pl.BlockSpec((d, tn), lambda j: (0, j)),
                  pl.BlockSpec((1, tn), lambda j: (0, j))],
        out_specs=pl.BlockSpec((rows, tn), lambda j: (0, j)),
        out_shape=jax.ShapeDtypeStruct((rows, n), F32),
        compiler_params=_cparams(("parallel",), 4 * d * tn * 2),
        name="modulation",
    )(c_all, w_mod, b_mod.reshape(1, n))


def _inproj_kernel(*refs, widths, add_pos):
    if add_pos:
        x_ref, pos_ref, mod_ref, w_ref = refs[:4]
        outs = refs[4:]
        x = x_ref[0] + pos_ref[...]
    else:
        x_ref, mod_ref, w_ref = refs[:3]
        outs = refs[3:]
        x = x_ref[0]
    sh = mod_ref[0, 0:1, :]
    sc = mod_ref[0, 1:2, :]
    h = (x * (1.0 + sc) + sh).astype(BF16)
    off = 0
    for o_ref, wd in zip(outs, widths):
        o_ref[0] = jnp.dot(h, w_ref[:, off:off + wd], preferred_element_type=F32)
        off += wd


def _in_proj(x, pos, mod, w, widths, tm):
    b, l, d = x.shape
    n = w.shape[1]
    assert sum(widths) == n
    add_pos = pos is not None
    in_specs = [pl.BlockSpec((1, tm, d), lambda bi, i: (bi, i, 0))]
    args = [x]
    if add_pos:
        in_specs.append(pl.BlockSpec((tm, d), lambda bi, i: (i, 0)))
        args.append(pos)
    in_specs += [pl.BlockSpec((1, 6, d), lambda bi, i: (bi, 0, 0)),
                 pl.BlockSpec((d, n), lambda bi, i: (0, 0))]
    args += [mod, w]
    vmem = 2 * (2 * tm * d * 4 + d * n * 2 + tm * n * 4) + tm * d * 8
    return pl.pallas_call(
        functools.partial(_inproj_kernel, widths=widths, add_pos=add_pos),
        grid=(b, l // tm),
        in_specs=in_specs,
        out_specs=[pl.BlockSpec((1, tm, wd), lambda bi, i: (bi, i, 0)) for wd in widths],
        out_shape=[jax.ShapeDtypeStruct((b, l, wd), F32) for wd in widths],
        compiler_params=_cparams(("parallel", "parallel"), vmem),
        name="in_proj",
    )(*args)


def _prep_kernel(z_ref, zp_ref, zn_ref, lo_ref, conv_ref, wl_ref, w0_ref, a0_ref, kk_ref, ka_ref,
                 ones_ref, r_o, v_o, kk_o, wf_o, wb_o, bf_o, bb_o, kdf_o, kdb_o, gate_o, *, tm):
    i = pl.program_id(1)
    n = pl.num_programs(1)
    z = z_ref[0]
    row = lax.broadcasted_iota(jnp.int32, z.shape, 0)
    prev_row = zp_ref[0, V7X_SUBLANES - 1:V7X_SUBLANES, :] * (i > 0).astype(F32)
    next_row = zn_ref[0, 0:1, :] * (i < n - 1).astype(F32)
    zm1 = jnp.where(row == 0, prev_row, pltpu.roll(z, 1, 0))
    zp1 = jnp.where(row == tm - 1, next_row, pltpu.roll(z, tm - 1, 0))
    c = zm1 * conv_ref[0:1, :] + z * conv_ref[1:2, :] + zp1 * conv_ref[2:3, :]
    r = c[:, 0:GROUP_W]
    k = c[:, GROUP_W:2 * GROUP_W]
    v = c[:, 2 * GROUP_W:3 * GROUP_W]

    lo = lo_ref[0]
    col = lax.broadcasted_iota(jnp.int32, lo.shape, 1)
    act = jnp.where(col < 128, jnp.tanh(lo), jnp.where(col < 256, lo, jax.nn.sigmoid(lo)))
    pre = jnp.dot(act.astype(BF16), wl_ref[...], preferred_element_type=F32)

    ones_bd = ones_ref[...]
    kk = k * kk_ref[...]
    nrm = jnp.sqrt(_seg_sum(kk * kk, ones_bd))
    kk = kk / jnp.maximum(nrm, 1e-12)
    r_o[0] = r
    v_o[0] = v
    kk_o[0] = kk
    gate_o[0] = pre[:, 4 * GROUP_W:5 * GROUP_W]
    ka = ka_ref[...]
    for d, (w_o, b_o, kd_o) in enumerate(((wf_o, bf_o, kdf_o), (wb_o, bb_o, kdb_o))):
        xw = -(w0_ref[d:d + 1, :] + pre[:, d * GROUP_W:(d + 1) * GROUP_W])
        softplus = jnp.maximum(xw, 0.0) + jnp.log1p(jnp.exp(-jnp.abs(xw)))
        logw = -softplus - 0.5
        w_o[0] = jnp.exp(-jnp.exp(logw))
        a = jax.nn.sigmoid(a0_ref[d:d + 1, :] + pre[:, (2 + d) * GROUP_W:(3 + d) * GROUP_W])
        kd_o[0] = k * (1.0 + (a - 1.0) * ka)
        b_o[0] = kk * a


def _rwkv_prep(z_rkv, z_lora, conv, wl, w0, a0, k_k, k_a, ones_bd, tm):
    b, l, c3 = z_rkv.shape
    nblk8 = l // V7X_SUBLANES
    r8 = tm // V7X_SUBLANES
    full = lambda shape: pl.BlockSpec(shape, lambda bi, i: (0,) * len(shape))
    in_specs = [
        pl.BlockSpec((1, tm, c3), lambda bi, i: (bi, i, 0)),
        pl.BlockSpec((1, V7X_SUBLANES, c3), lambda bi, i: (bi, jnp.maximum(i * r8 - 1, 0), 0)),
        pl.BlockSpec((1, V7X_SUBLANES, c3), lambda bi, i: (bi, jnp.minimum((i + 1) * r8, nblk8 - 1), 0)),
        pl.BlockSpec((1, tm, LORA_W), lambda bi, i: (bi, i, 0)),
        full(conv.shape), full(wl.shape), full(w0.shape), full(a0.shape),
        full(k_k.shape), full(k_a.shape), full(ones_bd.shape),
    ]
    out_spec = pl.BlockSpec((1, tm, GROUP_W), lambda bi, i: (bi, i, 0))
    vmem = 2 * (tm * (c3 + LORA_W) * 4 + 10 * tm * GROUP_W * 4) + 12 * tm * c3 * 4
    return pl.pallas_call(
        functools.partial(_prep_kernel, tm=tm),
        grid=(b, l // tm),
        in_specs=in_specs,
        out_specs=[out_spec] * 10,
        out_shape=[jax.ShapeDtypeStruct((b, l, GROUP_W), F32)] * 10,
        compiler_params=_cparams(("parallel", "parallel"), vmem),
        name="rwkv_prep",
    )(z_rkv, z_rkv, z_rkv, z_lora, conv, wl, w0, a0, k_k, k_a, ones_bd)


def _scan_kernel(r_ref, v_ref, kkn_ref, w_ref, b_ref, kd_ref, y_ref, h_ref, u_ref, *, tb):
    @pl.when(pl.program_id(0) == 0)
    def _():
        h_ref[...] = jnp.zeros_like(h_ref)
        u_ref[...] = jnp.zeros_like(u_ref)

    def step(t, carry):
        u = u_ref[...]
        vv = v_ref[t]
        y = jnp.zeros((HEAD, V7X_LANES), F32)
        un = jnp.zeros((HEAD, V7X_LANES), F32)
        for k in range(HEAD):
            hk = h_ref[k]
            wk = w_ref[t, k:k + 1, :]
            bk = b_ref[t, k:k + 1, :]
            kdk = kd_ref[t, k:k + 1, :]
            rk = r_ref[t, k:k + 1, :]
            kn = kkn_ref[t, k:k + 1, :]
            hn = hk * wk - bk * u + kdk * vv
            h_ref[k] = hn
            y = y + rk * hn
            un = un + kn * hn
        y_ref[t] = y
        u_ref[...] = un
        return carry

    lax.fori_loop(0, tb, step, 0)


def _rwkv_scan(r, v, kkn, w, b, kd, tb):
    t_tot = r.shape[0]
    spec = pl.BlockSpec((tb, HEAD, V7X_LANES), lambda i: (i, 0, 0))
    blk = tb * HEAD * V7X_LANES * 4
    return pl.pallas_call(
        functools.partial(_scan_kernel, tb=tb),
        grid=(t_tot // tb,),
        in_specs=[spec] * 6,
        out_specs=spec,
        out_shape=jax.ShapeDtypeStruct((t_tot, HEAD, V7X_LANES), F32),
        scratch_shapes=[pltpu.VMEM((HEAD, HEAD, V7X_LANES), F32), pltpu.VMEM((HEAD, V7X_LANES), F32)],
        compiler_params=_cparams(("arbitrary",), 14 * blk + 3 * HEAD * HEAD * V7X_LANES * 4),
        name="rwkv_scan",
    )(r, v, kkn, w, b, kd)


def _readout_kernel(yf_ref, yb_ref, r_ref, v_ref, kdf_ref, kdb_ref, gate_ref, rk_ref, g_ref, b_ref,
                    ones_ref, o_ref):
    ones_bd = ones_ref[...]
    inv = 1.0 / HEAD
    y = yf_ref[0] + yb_ref[0]
    mu = _seg_sum(y, ones_bd) * inv
    yc = y - mu
    var = _seg_sum(yc * yc, ones_bd) * inv
    yn = yc * lax.rsqrt(var + GN_EPS)
    bonus = _seg_sum(r_ref[0] * (kdf_ref[0] + kdb_ref[0]) * rk_ref[...], ones_bd) * v_ref[0]
    o_ref[0] = (yn * g_ref[...] + b_ref[...] + bonus) * gate_ref[0]


def _rwkv_readout(yf, yb, r, v, kdf, kdb, gate, r_k, gn_g, gn_b, ones_bd, tm):
    b, l, c = yf.shape
    spec = pl.BlockSpec((1, tm, c), lambda bi, i: (bi, i, 0))
    full = lambda shape: pl.BlockSpec(shape, lambda bi, i: (0,) * len(shape))
    return pl.pallas_call(
        _readout_kernel,
        grid=(b, l // tm),
        in_specs=[spec] * 7 + [full(r_k.shape), full(gn_g.shape), full(gn_b.shape), full(ones_bd.shape)],
        out_specs=spec,
        out_shape=jax.ShapeDtypeStruct((b, l, c), F32),
        compiler_params=_cparams(("parallel", "parallel"), 30 * tm * c * 4),
        name="rwkv_readout",
    )(yf, yb, r, v, kdf, kdb, gate, r_k, gn_g, gn_b, ones_bd)


def _pool_kernel(z_ref, w_ref, s_ref, o_ref, *, l):
    z = z_ref[0]
    row = lax.broadcasted_iota(jnp.int32, z.shape, 0)
    lane = lax.broadcasted_iota(jnp.int32, z.shape, 1)

    def shifted(x, o):
        y = pltpu.roll(x, (-o) % l, 0)
        return jnp.where((row + o >= 0) & (row + o < l), y, 0.0)

    def count(wd):
        lo = jnp.maximum(row - wd // 2, 0)
        hi = jnp.minimum(row + wd - wd // 2, l)
        return (hi - lo).astype(F32)

    s2 = shifted(z, -1) + z
    pooled = s2 / count(2)
    s4 = shifted(s2, -1) + shifted(s2, 1)
    pooled = jnp.where(lane >= 64, s4 / count(4), pooled)
    s8 = shifted(s4, -2) + shifted(s4, 2)
    pooled = jnp.where(lane >= 128, s8 / count(8), pooled)
    s16 = shifted(s8, -4) + shifted(s8, 4)
    pooled = jnp.where(lane >= 192, s16 / count(16), pooled)
    d = pooled - z
    o_ref[0] = jnp.dot(d.astype(BF16), w_ref[...], preferred_element_type=F32) * s_ref[...]


def _pool_mixer(z, w_bd, scale):
    b, l, c = z.shape
    spec = pl.BlockSpec((1, l, c), lambda bi: (bi, 0, 0))
    return pl.pallas_call(
        functools.partial(_pool_kernel, l=l),
        grid=(b,),
        in_specs=[spec, pl.BlockSpec(w_bd.shape, lambda bi: (0, 0)), pl.BlockSpec(scale.shape, lambda bi: (0, 0))],
        out_specs=spec,
        out_shape=jax.ShapeDtypeStruct((b, l, c), F32),
        compiler_params=_cparams(("parallel",), 20 * l * c * 4),
        name="pool_mixer",
    )(z, w_bd, scale)


def _gelu_tanh(x):
    return 0.5 * x * (1.0 + jnp.tanh(math.sqrt(2.0 / math.pi) * (x + 0.044715 * (x * x * x))))


def _gmlp_kernel(z_ref, g_ref, b_ref, ws_ref, bias_ref, ones_ref, o_ref, *, tm):
    ones_bd = ones_ref[...]
    inv = 1.0 / HEAD
    z = z_ref[0]
    u = _gelu_tanh(z[:, 0:GROUP_W])
    v = _gelu_tanh(z[:, GROUP_W:2 * GROUP_W])
    mu = _seg_sum(v, ones_bd) * inv
    vc = v - mu
    var = _seg_sum(vc * vc, ones_bd) * inv
    vn = (vc * lax.rsqrt(var + LN_EPS) * g_ref[...] + b_ref[...]).astype(BF16)
    lane = lax.broadcasted_iota(jnp.int32, (CHUNK, GROUP_W), 1)
    for c in range(tm // CHUNK):
        vchunk = vn[c * CHUNK:(c + 1) * CHUNK, :]
        sv = bias_ref[...]
        for g in range(4):
            part = jnp.dot(ws_ref[g], vchunk, preferred_element_type=F32)
            sv = sv + jnp.where((lane >= g * HEAD) & (lane < (g + 1) * HEAD), part, 0.0)
        o_ref[0, c * CHUNK:(c + 1) * CHUNK, :] = u[c * CHUNK:(c + 1) * CHUNK, :] * sv


def _gmlp_mixer(z, ln_g, ln_b, ws, bias_tile, ones_bd, tm):
    b, l, c2 = z.shape
    full = lambda shape: pl.BlockSpec(shape, lambda bi, i: (0,) * len(shape))
    return pl.pallas_call(
        functools.partial(_gmlp_kernel, tm=tm),
        grid=(b, l // tm),
        in_specs=[pl.BlockSpec((1, tm, c2), lambda bi, i: (bi, i, 0)), full(ln_g.shape), full(ln_b.shape),
                  full(ws.shape), full(bias_tile.shape), full(ones_bd.shape)],
        out_specs=pl.BlockSpec((1, tm, GROUP_W), lambda bi, i: (bi, i, 0)),
        out_shape=jax.ShapeDtypeStruct((b, l, GROUP_W), F32),
        compiler_params=_cparams(("parallel", "parallel"), 24 * tm * c2 * 4),
        name="gmlp_mixer",
    )(z, ln_g, ln_b, ws, bias_tile, ones_bd)


def _fnet_kernel(z_ref, cs_ref, cc_ref, sc_ref, w_ref, b_ref, o_ref, zcs_ref, *, l, scale):
    @pl.when(pl.program_id(1) == 0)
    def _():
        zb = z_ref[0].astype(BF16)
        zcs_ref[0:l, :] = jnp.dot(zb, cc_ref[...], preferred_element_type=F32).astype(BF16)
        zcs_ref[l:2 * l, :] = jnp.dot(zb, sc_ref[...], preferred_element_type=F32).astype(BF16)

    f = jnp.dot(cs_ref[...], zcs_ref[...], preferred_element_type=F32) * scale
    o_ref[0] = jnp.dot(f.astype(BF16), w_ref[...], preferred_element_type=F32) + b_ref[...]


def _fourier_mixer(z, cs, cc_bd, sc_bd, w_bd, bias, tm):
    b, l, c = z.shape
    full = lambda shape: pl.BlockSpec(shape, lambda bi, i: (0,) * len(shape))
    scale = 1.0 / math.sqrt(l * HEAD)
    vmem = 2 * (l * c * 4 + tm * 2 * l * 2 + tm * c * 4) + 2 * l * c * 2 + 6 * l * c * 4
    return pl.pallas_call(
        functools.partial(_fnet_kernel, l=l, scale=scale),
        grid=(b, l // tm),
        in_specs=[pl.BlockSpec((1, l, c), lambda bi, i: (bi, 0, 0)),
                  pl.BlockSpec((tm, 2 * l), lambda bi, i: (i, 0)),
                  full(cc_bd.shape), full(sc_bd.shape), full(w_bd.shape), full(bias.shape)],
        out_specs=pl.BlockSpec((1, tm, c), lambda bi, i: (bi, i, 0)),
        out_shape=jax.ShapeDtypeStruct((b, l, c), F32),
        scratch_shapes=[pltpu.VMEM((2 * l, c), BF16)],
        compiler_params=_cparams(("parallel", "arbitrary"), vmem),
        name="fourier_mixer",
    )(z, cs, cc_bd, sc_bd, w_bd, bias)


def _outproj_kernel(*refs, alpha, add_pos):
    if add_pos:
        a_ref, p_ref, g_ref, f_ref, x_ref, pos_ref, mod_ref, w_ref, lg_ref, lb_ref, o_ref = refs
        x = x_ref[0] + pos_ref[...]
    else:
        a_ref, p_ref, g_ref, f_ref, x_ref, mod_ref, w_ref, lg_ref, lb_ref, o_ref = refs
        x = x_ref[0]
    mix = jnp.dot(a_ref[0].astype(BF16), w_ref[0:GROUP_W, :], preferred_element_type=F32)
    for j, part in enumerate((p_ref, g_ref, f_ref), start=1):
        mix = mix + jnp.dot(part[0].astype(BF16), w_ref[j * GROUP_W:(j + 1) * GROUP_W, :],
                            preferred_element_type=F32)
    gate1 = mod_ref[0, 2:3, :]
    o_ref[0] = _layer_norm(alpha * x + gate1 * mix, lg_ref[...], lb_ref[...])


def _out_proj(parts, x, pos, mod, w_out, ln_g, ln_b, alpha, tm):
    b, l, d = x.shape
    add_pos = pos is not None
    pspec = pl.BlockSpec((1, tm, GROUP_W), lambda bi, i: (bi, i, 0))
    xspec = pl.BlockSpec((1, tm, d), lambda bi, i: (bi, i, 0))
    full = lambda shape: pl.BlockSpec(shape, lambda bi, i: (0,) * len(shape))
    in_specs = [pspec] * 4 + [xspec]
    args = list(parts) + [x]
    if add_pos:
        in_specs.append(pl.BlockSpec((tm, d), lambda bi, i: (i, 0)))
        args.append(pos)
    in_specs += [pl.BlockSpec((1, 6, d), lambda bi, i: (bi, 0, 0)), full(w_out.shape), full(ln_g.shape),
                 full(ln_b.shape)]
    args += [mod, w_out, ln_g, ln_b]
    vmem = 2 * (4 * tm * GROUP_W * 4 + 3 * tm * d * 4 + d * d * 2) + 6 * tm * d * 4
    return pl.pallas_call(
        functools.partial(_outproj_kernel, alpha=alpha, add_pos=add_pos),
        grid=(b, l // tm),
        in_specs=in_specs,
        out_specs=xspec,
        out_shape=jax.ShapeDtypeStruct((b, l, d), F32),
        compiler_params=_cparams(("parallel", "parallel"), vmem),
        name="out_proj_ln",
    )(*args)


def _ffn_kernel(x_ref, mod_ref, wg_ref, wu_ref, w2_ref, lg_ref, lb_ref, o_ref, h_ref, acc_ref, *, alpha):
    j = pl.program_id(2)

    @pl.when(j == 0)
    def _():
        sh = mod_ref[0, 3:4, :]
        sc = mod_ref[0, 4:5, :]
        h_ref[...] = (x_ref[0] * (1.0 + sc) + sh).astype(BF16)
        acc_ref[...] = jnp.zeros_like(acc_ref)

    h = h_ref[...]
    gate = jnp.dot(h, wg_ref[...], preferred_element_type=F32)
    up = jnp.dot(h, wu_ref[...], preferred_element_type=F32)
    act = (_silu(gate) * up).astype(BF16)
    acc_ref[...] += jnp.dot(act, w2_ref[...], preferred_element_type=F32)

    @pl.when(j == pl.num_programs(2) - 1)
    def _():
        gate2 = mod_ref[0, 5:6, :]
        o_ref[0] = _layer_norm(alpha * x_ref[0] + gate2 * acc_ref[...], lg_ref[...], lb_ref[...])


def _ffn(x, mod, w1, w2, ln_g, ln_b, alpha, tm, tf):
    b, l, d = x.shape
    f = w2.shape[0]
    nf = f // tf
    xspec = pl.BlockSpec((1, tm, d), lambda bi, i, j: (bi, i, 0))
    full = lambda shape: pl.BlockSpec(shape, lambda bi, i, j: (0,) * len(shape))
    vmem = 2 * (2 * tm * d * 4 + 3 * d * tf * 2) + tm * d * 6 + 4 * tm * tf * 4
    return pl.pallas_call(
        functools.partial(_ffn_kernel, alpha=alpha),
        grid=(b, l // tm, nf),
        in_specs=[xspec,
                  pl.BlockSpec((1, 6, d), lambda bi, i, j: (bi, 0, 0)),
                  pl.BlockSpec((d, tf), lambda bi, i, j: (0, j)),
                  pl.BlockSpec((d, tf), lambda bi, i, j: (0, j + nf)),
                  pl.BlockSpec((tf, d), lambda bi, i, j: (j, 0)),
                  full(ln_g.shape), full(ln_b.shape)],
        out_specs=xspec,
        out_shape=jax.ShapeDtypeStruct((b, l, d), F32),
        scratch_shapes=[pltpu.VMEM((tm, d), BF16), pltpu.VMEM((tm, d), F32)],
        compiler_params=_cparams(("parallel", "parallel", "arbitrary"), vmem),
        name="ffn_ln",
    )(x, mod, w1, w1, w2, ln_g, ln_b)


def _pos_embed(n_tok, dim):
    rows = n_tok // GRID_W
    row, col = jnp.meshgrid(jnp.arange(rows, dtype=F32), jnp.arange(GRID_W, dtype=F32), indexing='ij')
    quarter = dim // 4
    freqs = jnp.exp(-math.log(10000.0) * jnp.arange(quarter, dtype=F32) / quarter)

    def enc(p):
        ang = p.reshape(-1, 1) * freqs[None, :]
        return jnp.concatenate([jnp.sin(ang), jnp.cos(ang)], -1)

    return jnp.concatenate([enc(row), enc(col)], -1)


def _block_diag(blocks):
    g, n, m = blocks.shape
    eye = jnp.eye(g, dtype=blocks.dtype)
    return (eye[:, None, :, None] * blocks[:, :, None, :]).reshape(g * n, g * m)


def _dft_tables(n):
    idx = jnp.arange(n, dtype=jnp.int32)
    ang = ((idx[:, None] * idx[None, :]) % n).astype(F32) * (2.0 * math.pi / n)
    return jnp.cos(ang), jnp.sin(ang)


def _to_scan(xf, xb):
    def f(x):
        b, t, _ = x.shape
        return x.reshape(b, t, N_HEADS, HEAD).transpose(1, 3, 0, 2).reshape(t, HEAD, b * N_HEADS)
    return jnp.concatenate([f(xf), f(xb)[::-1]], axis=-1)


def _from_scan(y, b):
    t = y.shape[0]
    half = b * N_HEADS

    def g(x):
        return x.reshape(t, HEAD, b, N_HEADS).transpose(2, 0, 3, 1).reshape(b, t, GROUP_W)
    return g(y[:, :, :half]), g(y[::-1, :, half:])


def kernel(x, c, ctx, c_ctx, w_mod, b_mod, w_in, rkv_conv, decay_w0, decay_w2, iclr_a0, iclr_a2,
           gate_g2, k_k, k_a, r_k, gn_g, gn_b, pool_w, pool_scale, gmlp_ln_g, gmlp_ln_b, gmlp_ws,
           gmlp_bs, fnet_w, fnet_b, w_out, ln1_g, ln1_b, ln2_g, ln2_b, ffn_w1, ffn_w2):
    bsz, seq, d = x.shape
    ctx_len = ctx.shape[1]
    depth = w_in.shape[0]
    d_ff = ffn_w2.shape[1]
    alpha = (2 * depth) ** 0.25
    rwkv_cols = 3 * GROUP_W + LORA_W
    assert bsz * N_HEADS * 2 == V7X_LANES, "scan layout packs (direction, batch, head) on the 128 lanes"

    pos = _pos_embed(seq, d)
    ones_bd = _block_diag(jnp.ones((N_HEADS, HEAD, HEAD), F32))
    cc64, ss64 = _dft_tables(HEAD)
    cc_bd = _block_diag(jnp.tile(cc64[None], (4, 1, 1))).astype(BF16)
    sc_bd = _block_diag(jnp.tile(ss64[None], (4, 1, 1))).astype(BF16)

    def dft_rows(n):
        cn, sn = _dft_tables(n)
        return jnp.concatenate([cn, -sn], axis=1).astype(BF16)

    cs_x = dft_rows(seq)
    cs_c = dft_rows(ctx_len)

    pad_rows = (-(bsz + 1)) % V7X_SUBLANES
    c_all = jnp.concatenate([c, c_ctx[None], jnp.zeros((pad_rows, d), F32)], axis=0)

    tm_x = 512
    tm_c = ctx_len
    tb = 32

    xs, cs = x, ctx
    for l in range(depth):
        last = l == depth - 1
        first = l == 0
        m = _modulation(c_all, w_mod[l].astype(BF16), b_mod[l])
        mod_x = m[:bsz].reshape(bsz, 6, d)
        mod_c = jnp.broadcast_to(m[bsz].reshape(1, 6, d), (bsz, 6, d))

        w_in_l = w_in[l].astype(BF16)
        widths_full = (3 * GROUP_W, LORA_W, GROUP_W, 2 * GROUP_W, GROUP_W)
        px = _in_proj(xs, pos if first else None, mod_x, w_in_l, widths_full, tm_x)
        if last:
            pc = _in_proj(cs, None, mod_c, w_in_l[:, :rwkv_cols], widths_full[:2], tm_c)
        else:
            pc = _in_proj(cs, None, mod_c, w_in_l, widths_full, tm_c)

        wl = jnp.zeros((LORA_W, 5 * GROUP_W), F32)
        wl = wl.at[0:64, 0:256].set(decay_w2[l, 0]).at[64:128, 256:512].set(decay_w2[l, 1])
        wl = wl.at[128:192, 512:768].set(iclr_a2[l, 0]).at[192:256, 768:1024].set(iclr_a2[l, 1])
        wl = wl.at[256:384, 1024:1280].set(gate_g2[l]).astype(BF16)
        prep_args = (rkv_conv[l], wl, decay_w0[l], iclr_a0[l], k_k[l].reshape(1, -1), k_a[l].reshape(1, -1),
                     ones_bd)
        prep_c = _rwkv_prep(pc[0], pc[1], *prep_args, tm_c)
        prep_x = _rwkv_prep(px[0], px[1], *prep_args, tm_x)
        def scan_in(i_f, i_b):
            return jnp.concatenate([_to_scan(prep_c[i_f], prep_c[i_b]), _to_scan(prep_x[i_f], prep_x[i_b])], 0)
        r_s = scan_in(0, 0)
        v_s = scan_in(1, 1)
        kk_s = scan_in(2, 2)
        kkn_s = jnp.concatenate([kk_s[1:], jnp.zeros_like(kk_s[:1])], axis=0)
        w_s = scan_in(3, 4)
        b_s = scan_in(5, 6)
        kd_s = scan_in(7, 8)
        y_s = _rwkv_scan(r_s, v_s, kkn_s, w_s, b_s, kd_s, tb)
        yxf, yxb = _from_scan(y_s[ctx_len:], bsz)

        pool_bd = _block_diag(pool_w[l]).astype(BF16)
        fnet_bd = _block_diag(fnet_w[l]).astype(BF16)
        ws_b = gmlp_ws[l].astype(BF16)
        bias_tile = jnp.repeat(gmlp_bs[l].T, HEAD, axis=1)
        row = lambda a: a.reshape(1, -1)

        def mixers(parts, prep, yf, yb, cs_tab, tm, tmf):
            a = _rwkv_readout(yf, yb, prep[0], prep[1], prep[7], prep[8], prep[9], row(r_k[l]), row(gn_g[l]),
                              row(gn_b[l]), ones_bd, tm)
            p = _pool_mixer(parts[2], pool_bd, row(pool_scale[l]))
            g = _gmlp_mixer(parts[3], row(gmlp_ln_g[l]), row(gmlp_ln_b[l]), ws_b, bias_tile, ones_bd, tm)
            f = _fourier_mixer(parts[4], cs_tab, cc_bd, sc_bd, fnet_bd, row(fnet_b[l]), tmf)
            return a, p, g, f

        w_out_l = w_out[l].astype(BF16)
        w1_l = ffn_w1[l].astype(BF16)
        w2_l = ffn_w2[l].astype(BF16)
        tf = d_ff // 2
        parts_x = mixers(px, prep_x, yxf, yxb, cs_x, tm_x, 256)
        xs = _out_proj(parts_x, xs, pos if first else None, mod_x, w_out_l, row(ln1_g[l]), row(ln1_b[l]), alpha, tm_x)
        xs = _ffn(xs, mod_x, w1_l, w2_l, row(ln2_g[l]), row(ln2_b[l]), alpha, tm_x, tf)
        if not last:
            ycf, ycb = _from_scan(y_s[:ctx_len], bsz)
            parts_c = mixers(pc, prep_c, ycf, ycb, cs_c, tm_c, tm_c)
            cs = _out_proj(parts_c, cs, None, mod_c, w_out_l, row(ln1_g[l]), row(ln1_b[l]), alpha, tm_c)
            cs = _ffn(cs, mod_c, w1_l, w2_l, row(ln2_g[l]), row(ln2_b[l]), alpha, tm_c, tf)
    return xs
```

```python
import functools
import math

import jax
import jax.numpy as jnp
from jax import lax
from jax.experimental import pallas as pl
from jax.experimental.pallas import tpu as pltpu

F32 = jnp.float32
BF16 = jnp.bfloat16

V7X_LANES = 128
V7X_SUBLANES = 8
V7X_VMEM_BYTES = 64 * 1024 * 1024
V7X_VMEM_LIMIT_CAP = 56 * 1024 * 1024

GRID_W = 64
HEAD = 64
GROUP_W = 256
N_HEADS = GROUP_W // HEAD
LORA_W = 384
POOL_WINDOWS = (2, 4, 8, 16)
CHUNK = 128
LN_EPS = 1e-5
GN_EPS = 64e-5
HI = lax.Precision.HIGHEST


def _cparams(sem, vmem_bytes):
    limit = int(min(max(vmem_bytes, 16 * 1024 * 1024), V7X_VMEM_LIMIT_CAP))
    return pltpu.CompilerParams(dimension_semantics=sem, vmem_limit_bytes=limit)


def _silu(x):
    return x * jax.nn.sigmoid(x)


def _seg_sum(x, ones_bd):
    return jnp.dot(x, ones_bd, precision=HI, preferred_element_type=F32)


def _layer_norm(z, g, b):
    mu = jnp.mean(z, axis=-1, keepdims=True)
    zc = z - mu
    var = jnp.mean(zc * zc, axis=-1, keepdims=True)
    return zc * lax.rsqrt(var + LN_EPS) * g + b


def _mod_kernel(c_ref, w_ref, b_ref, o_ref):
    s = _silu(c_ref[...])
    o_ref[...] = jnp.dot(s.astype(BF16), w_ref[...], preferred_element_type=F32) + b_ref[...]


def _modulation(c_all, w_mod, b_mod):
    rows, d = c_all.shape
    n = w_mod.shape[1]
    tn = 1024
    return pl.pallas_call(
        _mod_kernel,
        grid=(n // tn,),
        in_specs=[pl.BlockSpec((rows, d), lambda j: (0, 0)),
                  pl.BlockSpec((d, tn), lambda j: (0, j)),
                  pl.BlockSpec((1, tn), lambda j: (0, j))],
        out_specs=pl.BlockSpec((rows, tn), lambda j: (0, j)),
        out_shape=jax.ShapeDtypeStruct((rows, n), F32),
        compiler_params=_cparams(("parallel",), 4 * d * tn * 2),
        name="modulation",
    )(c_all, w_mod, b_mod.reshape(1, n))


def _inproj_kernel(*refs, widths, add_pos):
    if add_pos:
        x_ref, pos_ref, mod_ref, w_ref = refs[:4]
        outs = refs[4:]
        x = x_ref[0] + pos_ref[...]
    else:
        x_ref, mod_ref, w_ref = refs[:3]
        outs = refs[3:]
        x = x_ref[0]
    sh = mod_ref[0, 0:1, :]
    sc = mod_ref[0, 1:2, :]
    h = (x * (1.0 + sc) + sh).astype(BF16)
    off = 0
    for o_ref, wd in zip(outs, widths):
        o_ref[0] = jnp.dot(h, w_ref[:, off:off + wd], preferred_element_type=F32)
        off += wd


def _in_proj(x, pos, mod, w, widths, tm):
    b, l, d = x.shape
    n = w.shape[1]
    assert sum(widths) == n
    add_pos = pos is not None
    in_specs = [pl.BlockSpec((1, tm, d), lambda bi, i: (bi, i, 0))]
    args = [x]
    if add_pos:
        in_specs.append(pl.BlockSpec((tm, d), lambda bi, i: (i, 0)))
        args.append(pos)
    in_specs += [pl.BlockSpec((1, 6, d), lambda bi, i: (bi, 0, 0)),
                 pl.BlockSpec((d, n), lambda bi, i: (0, 0))]
    args += [mod, w]
    vmem = 2 * (2 * tm * d * 4 + d * n * 2 + tm * n * 4) + tm * d * 8
    return pl.pallas_call(
        functools.partial(_inproj_kernel, widths=widths, add_pos=add_pos),
        grid=(b, l // tm),
        in_specs=in_specs,
        out_specs=[pl.BlockSpec((1, tm, wd), lambda bi, i: (bi, i, 0)) for wd in widths],
        out_shape=[jax.ShapeDtypeStruct((b, l, wd), F32) for wd in widths],
        compiler_params=_cparams(("parallel", "parallel"), vmem),
        name="in_proj",
    )(*args)


def _prep_kernel(zx_ref, zp_ref, zn_ref, zc_ref, lox_ref, loc_ref, conv_ref, wl_ref, w0_ref, a0_ref, kk_ref,
                 ka_ref, ones_ref, r_o, v_o, kk_o, gate_o, w_o, b_o, kd_o, *, tm, nx):
    i = pl.program_id(1)
    is_ctx = i >= nx
    z = jnp.where(is_ctx, zc_ref[0], zx_ref[0])
    lo = jnp.where(is_ctx, loc_ref[0], lox_ref[0])
    row = lax.broadcasted_iota(jnp.int32, z.shape, 0)
    has_prev = jnp.logical_and(i > 0, i < nx).astype(F32)
    has_next = (i < nx - 1).astype(F32)
    prev_row = zp_ref[0, V7X_SUBLANES - 1:V7X_SUBLANES, :] * has_prev
    next_row = zn_ref[0, 0:1, :] * has_next
    zm1 = jnp.where(row == 0, prev_row, pltpu.roll(z, 1, 0))
    zp1 = jnp.where(row == tm - 1, next_row, pltpu.roll(z, tm - 1, 0))
    c = zm1 * conv_ref[0:1, :] + z * conv_ref[1:2, :] + zp1 * conv_ref[2:3, :]
    r = c[:, 0:GROUP_W]
    k = c[:, GROUP_W:2 * GROUP_W]
    v = c[:, 2 * GROUP_W:3 * GROUP_W]

    col = lax.broadcasted_iota(jnp.int32, lo.shape, 1)
    act = jnp.where(col < 128, jnp.tanh(lo), jnp.where(col < 256, lo, jax.nn.sigmoid(lo)))
    pre = jnp.dot(act.astype(BF16), wl_ref[...], preferred_element_type=F32)

    ones_bd = ones_ref[...]
    kk = k * kk_ref[...]
    nrm = jnp.sqrt(_seg_sum(kk * kk, ones_bd))
    kk = kk / jnp.maximum(nrm, 1e-12)
    r_o[0] = r
    v_o[0] = v
    kk_o[0] = kk
    gate_o[0] = pre[:, 4 * GROUP_W:5 * GROUP_W]
    ka = ka_ref[...]
    for d in range(2):
        cols = slice(d * GROUP_W, (d + 1) * GROUP_W)
        xw = -(w0_ref[d:d + 1, :] + pre[:, cols])
        softplus = jnp.maximum(xw, 0.0) + jnp.log1p(jnp.exp(-jnp.abs(xw)))
        logw = -softplus - 0.5
        w_o[0, :, cols] = jnp.exp(-jnp.exp(logw))
        a = jax.nn.sigmoid(a0_ref[d:d + 1, :] + pre[:, (2 + d) * GROUP_W:(3 + d) * GROUP_W])
        kd_o[0, :, cols] = k * (1.0 + (a - 1.0) * ka)
        b_o[0, :, cols] = kk * a


def _rwkv_prep(zx_rkv, zx_lora, zc_rkv, zc_lora, conv, wl, w0, a0, k_k, k_a, ones_bd):
    b, l, c3 = zx_rkv.shape
    tm = zc_rkv.shape[1]
    nx = l // tm
    nblk8 = l // V7X_SUBLANES
    r8 = tm // V7X_SUBLANES
    full = lambda shape: pl.BlockSpec(shape, lambda bi, i: (0,) * len(shape))
    xi = lambda i: jnp.minimum(i, nx - 1)
    in_specs = [
        pl.BlockSpec((1, tm, c3), lambda bi, i: (bi, xi(i), 0)),
        pl.BlockSpec((1, V7X_SUBLANES, c3), lambda bi, i: (bi, jnp.clip(i * r8 - 1, 0, nblk8 - 1), 0)),
        pl.BlockSpec((1, V7X_SUBLANES, c3), lambda bi, i: (bi, jnp.minimum((i + 1) * r8, nblk8 - 1), 0)),
        pl.BlockSpec((1, tm, c3), lambda bi, i: (bi, 0, 0)),
        pl.BlockSpec((1, tm, LORA_W), lambda bi, i: (bi, xi(i), 0)),
        pl.BlockSpec((1, tm, LORA_W), lambda bi, i: (bi, 0, 0)),
        full(conv.shape), full(wl.shape), full(w0.shape), full(a0.shape),
        full(k_k.shape), full(k_a.shape), full(ones_bd.shape),
    ]
    spec1 = pl.BlockSpec((1, tm, GROUP_W), lambda bi, i: (bi, i, 0))
    spec2 = pl.BlockSpec((1, tm, 2 * GROUP_W), lambda bi, i: (bi, i, 0))
    shp1 = jax.ShapeDtypeStruct((b, l + tm, GROUP_W), F32)
    shp2 = jax.ShapeDtypeStruct((b, l + tm, 2 * GROUP_W), F32)
    vmem = 2 * (2 * tm * (c3 + LORA_W) * 4 + 10 * tm * GROUP_W * 4) + 14 * tm * c3 * 4
    return pl.pallas_call(
        functools.partial(_prep_kernel, tm=tm, nx=nx),
        grid=(b, nx + 1),
        in_specs=in_specs,
        out_specs=[spec1] * 4 + [spec2] * 3,
        out_shape=[shp1] * 4 + [shp2] * 3,
        compiler_params=_cparams(("parallel", "arbitrary"), vmem),
        name="rwkv_prep",
    )(zx_rkv, zx_rkv, zx_rkv, zc_rkv, zx_lora, zc_lora, conv, wl, w0, a0, k_k, k_a, ones_bd)


def _scan_kernel(rf, rb, vf, vb, kf, kb, wf, wb, bf, bb, df, db, yf_ref, yb_ref, h_ref, stg_ref, *, tb):
    @pl.when(pl.program_id(0) == 0)
    def _():
        h_ref[...] = jnp.zeros_like(h_ref)

    fwd_lane = lax.broadcasted_iota(jnp.int32, (HEAD, V7X_LANES), 1) < V7X_LANES // 2
    pairs = ((rf, rb), (vf, vb), (kf, kb), (wf, wb), (bf, bb), (df, db))
    R, V, KK, W, B, KD = range(6)

    def stage(t, carry):
        for q, (f_ref, b_ref) in enumerate(pairs):
            stg_ref[q, t] = jnp.where(fwd_lane, f_ref[t], b_ref[tb - 1 - t])
        return carry

    lax.fori_loop(0, tb, stage, 0)

    u0 = jnp.zeros((HEAD, V7X_LANES), F32)
    for k in range(HEAD):
        u0 = u0 + stg_ref[KK, 0, k:k + 1, :] * h_ref[k]

    def step(t, u):
        tn = jnp.minimum(t + 1, tb - 1)
        vv = stg_ref[V, t]
        y = jnp.zeros((HEAD, V7X_LANES), F32)
        un = jnp.zeros((HEAD, V7X_LANES), F32)
        for k in range(HEAD):
            hn = (h_ref[k] * stg_ref[W, t, k:k + 1, :] - stg_ref[B, t, k:k + 1, :] * u
                  + stg_ref[KD, t, k:k + 1, :] * vv)
            h_ref[k] = hn
            y = y + stg_ref[R, t, k:k + 1, :] * hn
            un = un + stg_ref[KK, tn, k:k + 1, :] * hn
        yf_ref[t] = y
        yb_ref[tb - 1 - t] = y
        return un

    lax.fori_loop(0, tb, step, u0)


def _rwkv_scan(r, v, kk, w, b, kd, tb, n_lat, n_ctx):
    t_tot = r.shape[0]
    nl, nc = n_lat // tb, n_ctx // tb
    nblk = nl + nc
    fspec = pl.BlockSpec((tb, HEAD, V7X_LANES), lambda i: (jnp.where(i < nc, nl + i, i - nc), 0, 0))
    bspec = pl.BlockSpec((tb, HEAD, V7X_LANES), lambda i: (nblk - 1 - i, 0, 0))
    blk = tb * HEAD * V7X_LANES * 4
    shp = jax.ShapeDtypeStruct((t_tot, HEAD, V7X_LANES), F32)
    return pl.pallas_call(
        functools.partial(_scan_kernel, tb=tb),
        grid=(nblk,),
        in_specs=[fspec, bspec] * 6,
        out_specs=[fspec, bspec],
        out_shape=[shp, shp],
        scratch_shapes=[pltpu.VMEM((HEAD, HEAD, V7X_LANES), F32), pltpu.VMEM((6, tb, HEAD, V7X_LANES), F32)],
        compiler_params=_cparams(("arbitrary",), 2 * 14 * blk + 6 * blk + 3 * HEAD * HEAD * V7X_LANES * 4),
        name="rwkv_scan",
    )(r, r, v, v, kk, kk, w, w, b, b, kd, kd)


def _readout_kernel(y_ref, r_ref, v_ref, kd_ref, gate_ref, rk_ref, g_ref, b_ref, ones_ref, o_ref):
    ones_bd = ones_ref[...]
    inv = 1.0 / HEAD
    y = y_ref[0, :, 0:GROUP_W] + y_ref[0, :, GROUP_W:2 * GROUP_W]
    kd = kd_ref[0, :, 0:GROUP_W] + kd_ref[0, :, GROUP_W:2 * GROUP_W]
    mu = _seg_sum(y, ones_bd) * inv
    yc = y - mu
    var = _seg_sum(yc * yc, ones_bd) * inv
    yn = yc * lax.rsqrt(var + GN_EPS)
    bonus = _seg_sum(r_ref[0] * kd * rk_ref[...], ones_bd) * v_ref[0]
    o_ref[0] = (yn * g_ref[...] + b_ref[...] + bonus) * gate_ref[0]


def _rwkv_readout(y2, r, v, kd2, gate, r_k, gn_g, gn_b, ones_bd, tm, row0, n_rows):
    b = y2.shape[0]
    off = row0 // tm
    spec1 = pl.BlockSpec((1, tm, GROUP_W), lambda bi, i: (bi, i + off, 0))
    spec2 = pl.BlockSpec((1, tm, 2 * GROUP_W), lambda bi, i: (bi, i + off, 0))
    full = lambda shape: pl.BlockSpec(shape, lambda bi, i: (0,) * len(shape))
    return pl.pallas_call(
        _readout_kernel,
        grid=(b, n_rows // tm),
        in_specs=[spec2, spec1, spec1, spec2, spec1,
                  full(r_k.shape), full(gn_g.shape), full(gn_b.shape), full(ones_bd.shape)],
        out_specs=pl.BlockSpec((1, tm, GROUP_W), lambda bi, i: (bi, i, 0)),
        out_shape=jax.ShapeDtypeStruct((b, n_rows, GROUP_W), F32),
        compiler_params=_cparams(("parallel", "parallel"), 36 * tm * GROUP_W * 4),
        name="rwkv_readout",
    )(y2, r, v, kd2, gate, r_k, gn_g, gn_b, ones_bd)


def _pool_kernel(z_ref, w_ref, s_ref, o_ref, *, l):
    z = z_ref[0]
    row = lax.broadcasted_iota(jnp.int32, z.shape, 0)
    lane = lax.broadcasted_iota(jnp.int32, z.shape, 1)

    def shifted(x, o):
        y = pltpu.roll(x, (-o) % l, 0)
        return jnp.where((row + o >= 0) & (row + o < l), y, 0.0)

    def count(wd):
        lo = jnp.maximum(row - wd // 2, 0)
        hi = jnp.minimum(row + wd - wd // 2, l)
        return (hi - lo).astype(F32)

    s2 = shifted(z, -1) + z
    pooled = s2 / count(2)
    s4 = shifted(s2, -1) + shifted(s2, 1)
    pooled = jnp.where(lane >= 64, s4 / count(4), pooled)
    s8 = shifted(s4, -2) + shifted(s4, 2)
    pooled = jnp.where(lane >= 128, s8 / count(8), pooled)
    s16 = shifted(s8, -4) + shifted(s8, 4)
    pooled = jnp.where(lane >= 192, s16 / count(16), pooled)
    d = pooled - z
    o_ref[0] = jnp.dot(d.astype(BF16), w_ref[...], preferred_element_type=F32) * s_ref[...]


def _pool_mixer(z, w_bd, scale):
    b, l, c = z.shape
    spec = pl.BlockSpec((1, l, c), lambda bi: (bi, 0, 0))
    return pl.pallas_call(
        functools.partial(_pool_kernel, l=l),
        grid=(b,),
        in_specs=[spec, pl.BlockSpec(w_bd.shape, lambda bi: (0, 0)), pl.BlockSpec(scale.shape, lambda bi: (0, 0))],
        out_specs=spec,
        out_shape=jax.ShapeDtypeStruct((b, l, c), F32),
        compiler_params=_cparams(("parallel",), 20 * l * c * 4),
        name="pool_mixer",
    )(z, w_bd, scale)


def _gelu_tanh(x):
    return 0.5 * x * (1.0 + jnp.tanh(math.sqrt(2.0 / math.pi) * (x + 0.044715 * (x * x * x))))


def _gmlp_kernel(z_ref, g_ref, b_ref, ws_ref, bias_ref, ones_ref, o_ref, *, tm):
    ones_bd = ones_ref[...]
    inv = 1.0 / HEAD
    z = z_ref[0]
    u = _gelu_tanh(z[:, 0:GROUP_W])
    v = _gelu_tanh(z[:, GROUP_W:2 * GROUP_W])
    mu = _seg_sum(v, ones_bd) * inv
    vc = v - mu
    var = _seg_sum(vc * vc, ones_bd) * inv
    vn = (vc * lax.rsqrt(var + LN_EPS) * g_ref[...] + b_ref[...]).astype(BF16)
    lane = lax.broadcasted_iota(jnp.int32, (CHUNK, GROUP_W), 1)
    for c in range(tm // CHUNK):
        vchunk = vn[c * CHUNK:(c + 1) * CHUNK, :]
        sv = bias_ref[...]
        for g in range(4):
            part = jnp.dot(ws_ref[g], vchunk, preferred_element_type=F32)
            sv = sv + jnp.where((lane >= g * HEAD) & (lane < (g + 1) * HEAD), part, 0.0)
        o_ref[0, c * CHUNK:(c + 1) * CHUNK, :] = u[c * CHUNK:(c + 1) * CHUNK, :] * sv


def _gmlp_mixer(z, ln_g, ln_b, ws, bias_tile, ones_bd, tm):
    b, l, c2 = z.shape
    full = lambda shape: pl.BlockSpec(shape, lambda bi, i: (0,) * len(shape))
    return pl.pallas_call(
        functools.partial(_gmlp_kernel, tm=tm),
        grid=(b, l // tm),
        in_specs=[pl.BlockSpec((1, tm, c2), lambda bi, i: (bi, i, 0)), full(ln_g.shape), full(ln_b.shape),
                  full(ws.shape), full(bias_tile.shape), full(ones_bd.shape)],
        out_specs=pl.BlockSpec((1, tm, GROUP_W), lambda bi, i: (bi, i, 0)),
        out_shape=jax.ShapeDtypeStruct((b, l, GROUP_W), F32),
        compiler_params=_cparams(("parallel", "parallel"), 24 * tm * c2 * 4),
        name="gmlp_mixer",
    )(z, ln_g, ln_b, ws, bias_tile, ones_bd)


def _fnet_kernel(z_ref, cs_ref, cc_ref, sc_ref, w_ref, b_ref, o_ref, zcs_ref, *, l, scale):
    @pl.when(pl.program_id(1) == 0)
    def _():
        zb = z_ref[0].astype(BF16)
        zcs_ref[0:l, :] = jnp.dot(zb, cc_ref[...], preferred_element_type=F32).astype(BF16)
        zcs_ref[l:2 * l, :] = jnp.dot(zb, sc_ref[...], preferred_element_type=F32).astype(BF16)

    f = jnp.dot(cs_ref[...], zcs_ref[...], preferred_element_type=F32) * scale
    o_ref[0] = jnp.dot(f.astype(BF16), w_ref[...], preferred_element_type=F32) + b_ref[...]


def _fourier_mixer(z, cs, cc_bd, sc_bd, w_bd, bias, tm):
    b, l, c = z.shape
    full = lambda shape: pl.BlockSpec(shape, lambda bi, i: (0,) * len(shape))
    scale = 1.0 / math.sqrt(l * HEAD)
    vmem = 2 * (l * c * 4 + tm * 2 * l * 2 + tm * c * 4) + 2 * l * c * 2 + 6 * l * c * 4
    return pl.pallas_call(
        functools.partial(_fnet_kernel, l=l, scale=scale),
        grid=(b, l // tm),
        in_specs=[pl.BlockSpec((1, l, c), lambda bi, i: (bi, 0, 0)),
                  pl.BlockSpec((tm, 2 * l), lambda bi, i: (i, 0)),
                  full(cc_bd.shape), full(sc_bd.shape), full(w_bd.shape), full(bias.shape)],
        out_specs=pl.BlockSpec((1, tm, c), lambda bi, i: (bi, i, 0)),
        out_shape=jax.ShapeDtypeStruct((b, l, c), F32),
        scratch_shapes=[pltpu.VMEM((2 * l, c), BF16)],
        compiler_params=_cparams(("parallel", "arbitrary"), vmem),
        name="fourier_mixer",
    )(z, cs, cc_bd, sc_bd, w_bd, bias)


def _outproj_kernel(*refs, alpha, add_pos):
    if add_pos:
        a_ref, p_ref, g_ref, f_ref, x_ref, pos_ref, mod_ref, w_ref, lg_ref, lb_ref, o_ref = refs
        x = x_ref[0] + pos_ref[...]
    else:
        a_ref, p_ref, g_ref, f_ref, x_ref, mod_ref, w_ref, lg_ref, lb_ref, o_ref = refs
        x = x_ref[0]
    mix = jnp.dot(a_ref[0].astype(BF16), w_ref[0:GROUP_W, :], preferred_element_type=F32)
    for j, part in enumerate((p_ref, g_ref, f_ref), start=1):
        mix = mix + jnp.dot(part[0].astype(BF16), w_ref[j * GROUP_W:(j + 1) * GROUP_W, :],
                            preferred_element_type=F32)
    gate1 = mod_ref[0, 2:3, :]
    o_ref[0] = _layer_norm(alpha * x + gate1 * mix, lg_ref[...], lb_ref[...])


def _out_proj(parts, x, pos, mod, w_out, ln_g, ln_b, alpha, tm):
    b, l, d = x.shape
    add_pos = pos is not None
    pspec = pl.BlockSpec((1, tm, GROUP_W), lambda bi, i: (bi, i, 0))
    xspec = pl.BlockSpec((1, tm, d), lambda bi, i: (bi, i, 0))
    full = lambda shape: pl.BlockSpec(shape, lambda bi, i: (0,) * len(shape))
    in_specs = [pspec] * 4 + [xspec]
    args = list(parts) + [x]
    if add_pos:
        in_specs.append(pl.BlockSpec((tm, d), lambda bi, i: (i, 0)))
        args.append(pos)
    in_specs += [pl.BlockSpec((1, 6, d), lambda bi, i: (bi, 0, 0)), full(w_out.shape), full(ln_g.shape),
                 full(ln_b.shape)]
    args += [mod, w_out, ln_g, ln_b]
    vmem = 2 * (4 * tm * GROUP_W * 4 + 3 * tm * d * 4 + d * d * 2) + 6 * tm * d * 4
    return pl.pallas_call(
        functools.partial(_outproj_kernel, alpha=alpha, add_pos=add_pos),
        grid=(b, l // tm),
        in_specs=in_specs,
        out_specs=xspec,
        out_shape=jax.ShapeDtypeStruct((b, l, d), F32),
        compiler_params=_cparams(("parallel", "parallel"), vmem),
        name="out_proj_ln",
    )(*args)


def _ffn_kernel(x_ref, mod_ref, wg_ref, wu_ref, w2_ref, lg_ref, lb_ref, o_ref, h_ref, acc_ref, *, alpha):
    j = pl.program_id(2)

    @pl.when(j == 0)
    def _():
        sh = mod_ref[0, 3:4, :]
        sc = mod_ref[0, 4:5, :]
        h_ref[...] = (x_ref[0] * (1.0 + sc) + sh).astype(BF16)
        acc_ref[...] = jnp.zeros_like(acc_ref)

    h = h_ref[...]
    gate = jnp.dot(h, wg_ref[...], preferred_element_type=F32)
    up = jnp.dot(h, wu_ref[...], preferred_element_type=F32)
    act = (_silu(gate) * up).astype(BF16)
    acc_ref[...] += jnp.dot(act, w2_ref[...], preferred_element_type=F32)

    @pl.when(j == pl.num_programs(2) - 1)
    def _():
        gate2 = mod_ref[0, 5:6, :]
        o_ref[0] = _layer_norm(alpha * x_ref[0] + gate2 * acc_ref[...], lg_ref[...], lb_ref[...])


def _ffn(x, mod, w1, w2, ln_g, ln_b, alpha, tm, tf):
    b, l, d = x.shape
    f = w2.shape[0]
    nf = f // tf
    xspec = pl.BlockSpec((1, tm, d), lambda bi, i, j: (bi, i, 0))
    full = lambda shape: pl.BlockSpec(shape, lambda bi, i, j: (0,) * len(shape))
    vmem = 2 * (2 * tm * d * 4 + 3 * d * tf * 2) + tm * d * 6 + 4 * tm * tf * 4
    return pl.pallas_call(
        functools.partial(_ffn_kernel, alpha=alpha),
        grid=(b, l // tm, nf),
        in_specs=[xspec,
                  pl.BlockSpec((1, 6, d), lambda bi, i, j: (bi, 0, 0)),
                  pl.BlockSpec((d, tf), lambda bi, i, j: (0, j)),
                  pl.BlockSpec((d, tf), lambda bi, i, j: (0, j + nf)),
                  pl.BlockSpec((tf, d), lambda bi, i, j: (j, 0)),
                  full(ln_g.shape), full(ln_b.shape)],
        out_specs=xspec,
        out_shape=jax.ShapeDtypeStruct((b, l, d), F32),
        scratch_shapes=[pltpu.VMEM((tm, d), BF16), pltpu.VMEM((tm, d), F32)],
        compiler_params=_cparams(("parallel", "parallel", "arbitrary"), vmem),
        name="ffn_ln",
    )(x, mod, w1, w1, w2, ln_g, ln_b)


def _pos_embed(n_tok, dim):
    rows = n_tok // GRID_W
    row, col = jnp.meshgrid(jnp.arange(rows, dtype=F32), jnp.arange(GRID_W, dtype=F32), indexing='ij')
    quarter = dim // 4
    freqs = jnp.exp(-math.log(10000.0) * jnp.arange(quarter, dtype=F32) / quarter)

    def enc(p):
        ang = p.reshape(-1, 1) * freqs[None, :]
        return jnp.concatenate([jnp.sin(ang), jnp.cos(ang)], -1)

    return jnp.concatenate([enc(row), enc(col)], -1)


def _block_diag(blocks):
    g, n, m = blocks.shape
    eye = jnp.eye(g, dtype=blocks.dtype)
    return (eye[:, None, :, None] * blocks[:, :, None, :]).reshape(g * n, g * m)


def _dft_tables(n):
    def direct(rows, cols, period):
        ang = ((rows[:, None] * cols[None, :]) % period).astype(F32) * (2.0 * math.pi / period)
        return jnp.cos(ang), jnp.sin(ang)

    idx = jnp.arange(n, dtype=jnp.int32)
    if n <= 1024:
        return direct(idx, idx, n)
    lo = 64
    ca, sa = direct(jnp.arange(n // lo, dtype=jnp.int32), idx, n // lo)
    cb, sb = direct(jnp.arange(lo, dtype=jnp.int32), idx, n)
    ca, sa, cb, sb = ca[:, None, :], sa[:, None, :], cb[None], sb[None]
    return (ca * cb - sa * sb).reshape(n, n), (sa * cb + ca * sb).reshape(n, n)


def _to_scan(x, b):
    t = x.shape[1]
    x5 = x.reshape(b, t, -1, N_HEADS, HEAD)
    x5 = jnp.broadcast_to(x5, (b, t, 2, N_HEADS, HEAD))
    return x5.transpose(1, 4, 2, 0, 3).reshape(t, HEAD, 2 * b * N_HEADS)


def _from_scan(yf, yb, b):
    t = yf.shape[0]
    lane = lax.broadcasted_iota(jnp.int32, yf.shape, 2)
    y = jnp.where(lane < b * N_HEADS, yf, yb)
    return y.reshape(t, HEAD, 2, b, N_HEADS).transpose(3, 0, 2, 4, 1).reshape(b, t, 2 * GROUP_W)


def kernel(x, c, ctx, c_ctx, w_mod, b_mod, w_in, rkv_conv, decay_w0, decay_w2, iclr_a0, iclr_a2,
           gate_g2, k_k, k_a, r_k, gn_g, gn_b, pool_w, pool_scale, gmlp_ln_g, gmlp_ln_b, gmlp_ws,
           gmlp_bs, fnet_w, fnet_b, w_out, ln1_g, ln1_b, ln2_g, ln2_b, ffn_w1, ffn_w2):
    bsz, seq, d = x.shape
    ctx_len = ctx.shape[1]
    depth = w_in.shape[0]
    d_ff = ffn_w2.shape[1]
    alpha = (2 * depth) ** 0.25
    rwkv_cols = 3 * GROUP_W + LORA_W
    assert bsz * N_HEADS * 2 == V7X_LANES, "scan layout packs (direction, batch, head) on the 128 lanes"

    pos = _pos_embed(seq, d)
    ones_bd = _block_diag(jnp.ones((N_HEADS, HEAD, HEAD), F32))
    cc64, ss64 = _dft_tables(HEAD)
    cc_bd = _block_diag(jnp.tile(cc64[None], (4, 1, 1))).astype(BF16)
    sc_bd = _block_diag(jnp.tile(ss64[None], (4, 1, 1))).astype(BF16)

    def dft_rows(n):
        cn, sn = _dft_tables(n)
        return jnp.concatenate([cn, -sn], axis=1).astype(BF16)

    cs_x = dft_rows(seq)
    cs_c = dft_rows(ctx_len)

    pad_rows = (-(bsz + 1)) % V7X_SUBLANES
    c_all = jnp.concatenate([c, c_ctx[None], jnp.zeros((pad_rows, d), F32)], axis=0)

    tm_x = 512
    tm_c = ctx_len
    tb = 32

    xs, cs = x, ctx
    for l in range(depth):
        last = l == depth - 1
        first = l == 0
        m = _modulation(c_all, w_mod[l].astype(BF16), b_mod[l])
        mod_x = m[:bsz].reshape(bsz, 6, d)
        mod_c = jnp.broadcast_to(m[bsz].reshape(1, 6, d), (bsz, 6, d))

        w_in_l = w_in[l].astype(BF16)
        widths_full = (3 * GROUP_W, LORA_W, GROUP_W, 2 * GROUP_W, GROUP_W)
        px = _in_proj(xs, pos if first else None, mod_x, w_in_l, widths_full, tm_x)
        if last:
            pc = _in_proj(cs, None, mod_c, w_in_l[:, :rwkv_cols], widths_full[:2], tm_c)
        else:
            pc = _in_proj(cs, None, mod_c, w_in_l, widths_full, tm_c)

        wl = jnp.zeros((LORA_W, 5 * GROUP_W), F32)
        wl = wl.at[0:64, 0:256].set(decay_w2[l, 0]).at[64:128, 256:512].set(decay_w2[l, 1])
        wl = wl.at[128:192, 512:768].set(iclr_a2[l, 0]).at[192:256, 768:1024].set(iclr_a2[l, 1])
        wl = wl.at[256:384, 1024:1280].set(gate_g2[l]).astype(BF16)
        prep_args = (rkv_conv[l], wl, decay_w0[l], iclr_a0[l], k_k[l].reshape(1, -1), k_a[l].reshape(1, -1),
                     ones_bd)
        p_r, p_v, p_kk, p_gate, p_w2, p_b2, p_kd2 = _rwkv_prep(px[0], px[1], pc[0], pc[1], *prep_args)
        yf_s, yb_s = _rwkv_scan(*(_to_scan(a, bsz) for a in (p_r, p_v, p_kk, p_w2, p_b2, p_kd2)), tb, seq, ctx_len)
        y2 = _from_scan(yf_s, yb_s, bsz)

        pool_bd = _block_diag(pool_w[l]).astype(BF16)
        fnet_bd = _block_diag(fnet_w[l]).astype(BF16)
        ws_b = gmlp_ws[l].astype(BF16)
        bias_tile = jnp.repeat(gmlp_bs[l].T, HEAD, axis=1)
        row = lambda a: a.reshape(1, -1)

        def mixers(parts, row0, n_rows, cs_tab, tm, tmf):
            a = _rwkv_readout(y2, p_r, p_v, p_kd2, p_gate, row(r_k[l]), row(gn_g[l]), row(gn_b[l]), ones_bd,
                              tm, row0, n_rows)
            p = _pool_mixer(parts[2], pool_bd, row(pool_scale[l]))
            g = _gmlp_mixer(parts[3], row(gmlp_ln_g[l]), row(gmlp_ln_b[l]), ws_b, bias_tile, ones_bd, tm)
            f = _fourier_mixer(parts[4], cs_tab, cc_bd, sc_bd, fnet_bd, row(fnet_b[l]), tmf)
            return a, p, g, f

        w_out_l = w_out[l].astype(BF16)
        w1_l = ffn_w1[l].astype(BF16)
        w2_l = ffn_w2[l].astype(BF16)
        tf = d_ff // 2
        parts_x = mixers(px, 0, seq, cs_x, tm_x, 256)
        xs = _out_proj(parts_x, xs, pos if first else None, mod_x, w_out_l, row(ln1_g[l]), row(ln1_b[l]), alpha, tm_x)
        xs = _ffn(xs, mod_x, w1_l, w2_l, row(ln2_g[l]), row(ln2_b[l]), alpha, tm_x, tf)
        if not last:
            parts_c = mixers(pc, seq, ctx_len, cs_c, tm_c, tm_c)
            cs = _out_proj(parts_c, cs, None, mod_c, w_out_l, row(ln1_g[l]), row(ln1_b[l]), alpha, tm_c)
            cs = _ffn(cs, mod_c, w1_l, w2_l, row(ln2_g[l]), row(ln2_b[l]), alpha, tm_c, tf)
    return xs
```

```python
import functools
import math

import jax
import jax.numpy as jnp
from jax import lax
from jax.experimental import pallas as pl
from jax.experimental.pallas import tpu as pltpu

F32 = jnp.float32
BF16 = jnp.bfloat16

V7X_LANES = 128
V7X_SUBLANES = 8
V7X_VMEM_BYTES = 64 * 1024 * 1024
V7X_VMEM_LIMIT_CAP = 56 * 1024 * 1024

GRID_W = 64
HEAD = 64
GROUP_W = 256
N_HEADS = GROUP_W // HEAD
LORA_W = 384
POOL_WINDOWS = (2, 4, 8, 16)
CHUNK = 128
LN_EPS = 1e-5
GN_EPS = 64e-5


def _cparams(sem, vmem_bytes):
    limit = int(min(max(vmem_bytes, 16 * 1024 * 1024), V7X_VMEM_LIMIT_CAP))
    return pltpu.CompilerParams(dimension_semantics=sem, vmem_limit_bytes=limit)


def _silu(x):
    return x * jax.nn.sigmoid(x)


def _seg_sum(x, ones_bd):
    hi = x.astype(BF16)
    lo = (x - hi.astype(F32)).astype(BF16)
    return (jnp.dot(hi, ones_bd, preferred_element_type=F32)
            + jnp.dot(lo, ones_bd, preferred_element_type=F32))


def _layer_norm(z, g, b):
    mu = jnp.mean(z, axis=-1, keepdims=True)
    zc = z - mu
    var = jnp.mean(zc * zc, axis=-1, keepdims=True)
    return zc * lax.rsqrt(var + LN_EPS) * g + b


def _mod_kernel(c_ref, w_ref, b_ref, o_ref):
    s = _silu(c_ref[...])
    o_ref[...] = jnp.dot(s.astype(BF16), w_ref[...], preferred_element_type=F32) + b_ref[...]


def _modulation(c_all, w_mod, b_mod):
    rows, d = c_all.shape
    n = w_mod.shape[1]
    tn = 1024
    return pl.pallas_call(
        _mod_kernel,
        grid=(n // tn,),
        in_specs=[pl.BlockSpec((rows, d), lambda j: (0, 0)),
                  pl.BlockSpec((d, tn), lambda j: (0, j)),
                  pl.BlockSpec((1, tn), lambda j: (0, j))],
        out_specs=pl.BlockSpec((rows, tn), lambda j: (0, j)),
        out_shape=jax.ShapeDtypeStruct((rows, n), F32),
        compiler_params=_cparams(("parallel",), 4 * d * tn * 2),
        name="modulation",
    )(c_all, w_mod, b_mod.reshape(1, n))


def _inproj_kernel(*refs, widths, add_pos):
    if add_pos:
        x_ref, pos_ref, mod_ref, w_ref = refs[:4]
        outs = refs[4:]
        x = x_ref[0] + pos_ref[...]
    else:
        x_ref, mod_ref, w_ref = refs[:3]
        outs = refs[3:]
        x = x_ref[0]
    sh = mod_ref[0, 0:1, :]
    sc = mod_ref[0, 1:2, :]
    h = (x * (1.0 + sc) + sh).astype(BF16)
    off = 0
    for o_ref, wd in zip(outs, widths):
        o_ref[0] = jnp.dot(h, w_ref[:, off:off + wd], preferred_element_type=F32)
        off += wd


def _in_proj(x, pos, mod, w, widths, tm):
    b, l, d = x.shape
    n = w.shape[1]
    assert sum(widths) == n
    add_pos = pos is not None
    in_specs = [pl.BlockSpec((1, tm, d), lambda bi, i: (bi, i, 0))]
    args = [x]
    if add_pos:
        in_specs.append(pl.BlockSpec((tm, d), lambda bi, i: (i, 0)))
        args.append(pos)
    in_specs += [pl.BlockSpec((1, 6, d), lambda bi, i: (bi, 0, 0)),
                 pl.BlockSpec((d, n), lambda bi, i: (0, 0))]
    args += [mod, w]
    vmem = 2 * (2 * tm * d * 4 + d * n * 2 + tm * n * 4) + tm * d * 8
    return pl.pallas_call(
        functools.partial(_inproj_kernel, widths=widths, add_pos=add_pos),
        grid=(b, l // tm),
        in_specs=in_specs,
        out_specs=[pl.BlockSpec((1, tm, wd), lambda bi, i: (bi, i, 0)) for wd in widths],
        out_shape=[jax.ShapeDtypeStruct((b, l, wd), F32) for wd in widths],
        compiler_params=_cparams(("parallel", "parallel"), vmem),
        name="in_proj",
    )(*args)


def _prep_kernel(zx_ref, zp_ref, zn_ref, zc_ref, lox_ref, loc_ref, conv_ref, wl_ref, w0_ref, a0_ref, kk_ref,
                 ka_ref, ones_ref, r_o, v_o, kk_o, gate_o, w_o, b_o, kd_o, *, tm, nx):
    i = pl.program_id(1)
    is_ctx = i >= nx
    z = jnp.where(is_ctx, zc_ref[0], zx_ref[0])
    lo = jnp.where(is_ctx, loc_ref[0], lox_ref[0])
    row = lax.broadcasted_iota(jnp.int32, z.shape, 0)
    has_prev = jnp.logical_and(i > 0, i < nx).astype(F32)
    has_next = (i < nx - 1).astype(F32)
    prev_row = zp_ref[0, V7X_SUBLANES - 1:V7X_SUBLANES, :] * has_prev
    next_row = zn_ref[0, 0:1, :] * has_next
    zm1 = jnp.where(row == 0, prev_row, pltpu.roll(z, 1, 0))
    zp1 = jnp.where(row == tm - 1, next_row, pltpu.roll(z, tm - 1, 0))
    c = zm1 * conv_ref[0:1, :] + z * conv_ref[1:2, :] + zp1 * conv_ref[2:3, :]
    r = c[:, 0:GROUP_W]
    k = c[:, GROUP_W:2 * GROUP_W]
    v = c[:, 2 * GROUP_W:3 * GROUP_W]

    col = lax.broadcasted_iota(jnp.int32, lo.shape, 1)
    act = jnp.where(col < 128, jnp.tanh(lo), jnp.where(col < 256, lo, jax.nn.sigmoid(lo)))
    pre = jnp.dot(act.astype(BF16), wl_ref[...], preferred_element_type=F32)

    ones_bd = ones_ref[...]
    kk = k * kk_ref[...]
    nrm = jnp.sqrt(_seg_sum(kk * kk, ones_bd))
    kk = kk / jnp.maximum(nrm, 1e-12)
    gate_o[0] = pre[:, 4 * GROUP_W:5 * GROUP_W]
    ka = ka_ref[...]
    for d in range(2):
        cols = slice(d * GROUP_W, (d + 1) * GROUP_W)
        r_o[0, :, cols] = r
        v_o[0, :, cols] = v
        kk_o[0, :, cols] = kk
        xw = w0_ref[d:d + 1, :] + pre[:, cols]
        w_o[0, :, cols] = jnp.exp(-math.exp(-0.5) * jax.nn.sigmoid(xw))
        a = jax.nn.sigmoid(a0_ref[d:d + 1, :] + pre[:, (2 + d) * GROUP_W:(3 + d) * GROUP_W])
        kd_o[0, :, cols] = k * (1.0 + (a - 1.0) * ka)
        b_o[0, :, cols] = kk * a


def _rwkv_prep(zx_rkv, zx_lora, zc_rkv, zc_lora, conv, wl, w0, a0, k_k, k_a, ones_bd):
    b, l, c3 = zx_rkv.shape
    tm = zc_rkv.shape[1]
    nx = l // tm
    nblk8 = l // V7X_SUBLANES
    r8 = tm // V7X_SUBLANES
    full = lambda shape: pl.BlockSpec(shape, lambda bi, i: (0,) * len(shape))
    xi = lambda i: jnp.minimum(i, nx - 1)
    in_specs = [
        pl.BlockSpec((1, tm, c3), lambda bi, i: (bi, xi(i), 0)),
        pl.BlockSpec((1, V7X_SUBLANES, c3), lambda bi, i: (bi, jnp.clip(i * r8 - 1, 0, nblk8 - 1), 0)),
        pl.BlockSpec((1, V7X_SUBLANES, c3), lambda bi, i: (bi, jnp.minimum((i + 1) * r8, nblk8 - 1), 0)),
        pl.BlockSpec((1, tm, c3), lambda bi, i: (bi, 0, 0)),
        pl.BlockSpec((1, tm, LORA_W), lambda bi, i: (bi, xi(i), 0)),
        pl.BlockSpec((1, tm, LORA_W), lambda bi, i: (bi, 0, 0)),
        full(conv.shape), full(wl.shape), full(w0.shape), full(a0.shape),
        full(k_k.shape), full(k_a.shape), full(ones_bd.shape),
    ]
    spec1 = pl.BlockSpec((1, tm, GROUP_W), lambda bi, i: (bi, i, 0))
    spec2 = pl.BlockSpec((1, tm, 2 * GROUP_W), lambda bi, i: (bi, i, 0))
    shp1 = jax.ShapeDtypeStruct((b, l + tm, GROUP_W), F32)
    shp2 = jax.ShapeDtypeStruct((b, l + tm, 2 * GROUP_W), F32)
    vmem = 2 * (2 * tm * (c3 + LORA_W) * 4 + 13 * tm * GROUP_W * 4) + 14 * tm * c3 * 4
    return pl.pallas_call(
        functools.partial(_prep_kernel, tm=tm, nx=nx),
        grid=(b, nx + 1),
        in_specs=in_specs,
        out_specs=[spec2] * 3 + [spec1] + [spec2] * 3,
        out_shape=[shp2] * 3 + [shp1] + [shp2] * 3,
        compiler_params=_cparams(("parallel", "arbitrary"), vmem),
        name="rwkv_prep",
    )(zx_rkv, zx_rkv, zx_rkv, zc_rkv, zx_lora, zc_lora, conv, wl, w0, a0, k_k, k_a, ones_bd)


def _scan_kernel(rf, rb, vf, vb, kf, kb, wf, wb, bf, bb, df, db, yf_ref, yb_ref, h_ref, stg_ref, *, tb):
    @pl.when(pl.program_id(0) == 0)
    def _():
        h_ref[...] = jnp.zeros_like(h_ref)

    fwd_lane = lax.broadcasted_iota(jnp.int32, (HEAD, V7X_LANES), 1) < V7X_LANES // 2
    pairs = ((rf, rb), (vf, vb), (kf, kb), (wf, wb), (bf, bb), (df, db))
    R, V, KK, W, B, KD = range(6)

    def stage(t, carry):
        for q, (f_ref, b_ref) in enumerate(pairs):
            stg_ref[q, t] = jnp.where(fwd_lane, f_ref[t], b_ref[tb - 1 - t])
        return carry

    lax.fori_loop(0, tb, stage, 0)

    u0 = jnp.zeros((HEAD, V7X_LANES), F32)
    for k in range(HEAD):
        u0 = u0 + stg_ref[KK, 0, k:k + 1, :] * h_ref[k]

    def step(t, u):
        tn = jnp.minimum(t + 1, tb - 1)
        vv = stg_ref[V, t]
        y = jnp.zeros((HEAD, V7X_LANES), F32)
        un = jnp.zeros((HEAD, V7X_LANES), F32)
        for k in range(HEAD):
            hn = (h_ref[k] * stg_ref[W, t, k:k + 1, :] - stg_ref[B, t, k:k + 1, :] * u
                  + stg_ref[KD, t, k:k + 1, :] * vv)
            h_ref[k] = hn
            y = y + stg_ref[R, t, k:k + 1, :] * hn
            un = un + stg_ref[KK, tn, k:k + 1, :] * hn
        yf_ref[t] = y
        yb_ref[tb - 1 - t] = y
        return un

    lax.fori_loop(0, tb, step, u0)


def _rwkv_scan(r, v, kk, w, b, kd, tb, n_lat, n_ctx):
    t_tot = r.shape[0]
    nl, nc = n_lat // tb, n_ctx // tb
    nblk = nl + nc
    fspec = pl.BlockSpec((tb, HEAD, V7X_LANES), lambda i: (jnp.where(i < nc, nl + i, i - nc), 0, 0))
    bspec = pl.BlockSpec((tb, HEAD, V7X_LANES), lambda i: (nblk - 1 - i, 0, 0))
    blk = tb * HEAD * V7X_LANES * 4
    shp = jax.ShapeDtypeStruct((t_tot, HEAD, V7X_LANES), F32)
    return pl.pallas_call(
        functools.partial(_scan_kernel, tb=tb),
        grid=(nblk,),
        in_specs=[fspec, bspec] * 6,
        out_specs=[fspec, bspec],
        out_shape=[shp, shp],
        scratch_shapes=[pltpu.VMEM((HEAD, HEAD, V7X_LANES), F32), pltpu.VMEM((6, tb, HEAD, V7X_LANES), F32)],
        compiler_params=_cparams(("arbitrary",), 2 * 14 * blk + 6 * blk + 3 * HEAD * HEAD * V7X_LANES * 4),
        name="rwkv_scan",
    )(r, r, v, v, kk, kk, w, w, b, b, kd, kd)


def _readout_kernel(y_ref, r_ref, v_ref, kd_ref, gate_ref, rk_ref, g_ref, b_ref, ones_ref, o_ref):
    ones_bd = ones_ref[...]
    inv = 1.0 / HEAD
    y = y_ref[0, :, 0:GROUP_W] + y_ref[0, :, GROUP_W:2 * GROUP_W]
    kd = kd_ref[0, :, 0:GROUP_W] + kd_ref[0, :, GROUP_W:2 * GROUP_W]
    mu = _seg_sum(y, ones_bd) * inv
    yc = y - mu
    var = _seg_sum(yc * yc, ones_bd) * inv
    yn = yc * lax.rsqrt(var + GN_EPS)
    bonus = _seg_sum(r_ref[0] * kd * rk_ref[...], ones_bd) * v_ref[0]
    o_ref[0] = (yn * g_ref[...] + b_ref[...] + bonus) * gate_ref[0]


def _rwkv_readout(y2, r, v, kd2, gate, r_k, gn_g, gn_b, ones_bd, tm, row0, n_rows):
    b = y2.shape[0]
    off = row0 // tm
    spec1 = pl.BlockSpec((1, tm, GROUP_W), lambda bi, i: (bi, i + off, 0))
    spec2 = pl.BlockSpec((1, tm, 2 * GROUP_W), lambda bi, i: (bi, i + off, 0))
    full = lambda shape: pl.BlockSpec(shape, lambda bi, i: (0,) * len(shape))
    return pl.pallas_call(
        _readout_kernel,
        grid=(b, n_rows // tm),
        in_specs=[spec2, spec1, spec1, spec2, spec1,
                  full(r_k.shape), full(gn_g.shape), full(gn_b.shape), full(ones_bd.shape)],
        out_specs=pl.BlockSpec((1, tm, GROUP_W), lambda bi, i: (bi, i, 0)),
        out_shape=jax.ShapeDtypeStruct((b, n_rows, GROUP_W), F32),
        compiler_params=_cparams(("parallel", "parallel"), 36 * tm * GROUP_W * 4),
        name="rwkv_readout",
    )(y2, r, v, kd2, gate, r_k, gn_g, gn_b, ones_bd)


def _pool_kernel(z_ref, w_ref, s_ref, o_ref, *, l):
    z = z_ref[0]
    row = lax.broadcasted_iota(jnp.int32, z.shape, 0)
    lane = lax.broadcasted_iota(jnp.int32, z.shape, 1)

    def shifted(x, o):
        y = pltpu.roll(x, (-o) % l, 0)
        return jnp.where((row + o >= 0) & (row + o < l), y, 0.0)

    def count(wd):
        lo = jnp.maximum(row - wd // 2, 0)
        hi = jnp.minimum(row + wd - wd // 2, l)
        return (hi - lo).astype(F32)

    s2 = shifted(z, -1) + z
    pooled = s2 / count(2)
    s4 = shifted(s2, -1) + shifted(s2, 1)
    pooled = jnp.where(lane >= 64, s4 / count(4), pooled)
    s8 = shifted(s4, -2) + shifted(s4, 2)
    pooled = jnp.where(lane >= 128, s8 / count(8), pooled)
    s16 = shifted(s8, -4) + shifted(s8, 4)
    pooled = jnp.where(lane >= 192, s16 / count(16), pooled)
    d = pooled - z
    o_ref[0] = jnp.dot(d.astype(BF16), w_ref[...], preferred_element_type=F32) * s_ref[...]


def _pool_mixer(z, w_bd, scale):
    b, l, c = z.shape
    spec = pl.BlockSpec((1, l, c), lambda bi: (bi, 0, 0))
    return pl.pallas_call(
        functools.partial(_pool_kernel, l=l),
        grid=(b,),
        in_specs=[spec, pl.BlockSpec(w_bd.shape, lambda bi: (0, 0)), pl.BlockSpec(scale.shape, lambda bi: (0, 0))],
        out_specs=spec,
        out_shape=jax.ShapeDtypeStruct((b, l, c), F32),
        compiler_params=_cparams(("parallel",), 20 * l * c * 4),
        name="pool_mixer",
    )(z, w_bd, scale)


def _gelu_tanh(x):
    return 0.5 * x * (1.0 + jnp.tanh(math.sqrt(2.0 / math.pi) * (x + 0.044715 * (x * x * x))))


def _gmlp_kernel(z_ref, g_ref, b_ref, ws_ref, bias_ref, ones_ref, o_ref, *, tm):
    ones_bd = ones_ref[...]
    inv = 1.0 / HEAD
    z = z_ref[0]
    u = _gelu_tanh(z[:, 0:GROUP_W])
    v = _gelu_tanh(z[:, GROUP_W:2 * GROUP_W])
    mu = _seg_sum(v, ones_bd) * inv
    vc = v - mu
    var = _seg_sum(vc * vc, ones_bd) * inv
    vn = (vc * lax.rsqrt(var + LN_EPS) * g_ref[...] + b_ref[...]).astype(BF16)
    lane = lax.broadcasted_iota(jnp.int32, (CHUNK, GROUP_W), 1)
    for c in range(tm // CHUNK):
        vchunk = vn[c * CHUNK:(c + 1) * CHUNK, :]
        sv = bias_ref[...]
        for g in range(4):
            part = jnp.dot(ws_ref[g], vchunk, preferred_element_type=F32)
            sv = sv + jnp.where((lane >= g * HEAD) & (lane < (g + 1) * HEAD), part, 0.0)
        o_ref[0, c * CHUNK:(c + 1) * CHUNK, :] = u[c * CHUNK:(c + 1) * CHUNK, :] * sv


def _gmlp_mixer(z, ln_g, ln_b, ws, bias_tile, ones_bd, tm):
    b, l, c2 = z.shape
    full = lambda shape: pl.BlockSpec(shape, lambda bi, i: (0,) * len(shape))
    return pl.pallas_call(
        functools.partial(_gmlp_kernel, tm=tm),
        grid=(b, l // tm),
        in_specs=[pl.BlockSpec((1, tm, c2), lambda bi, i: (bi, i, 0)), full(ln_g.shape), full(ln_b.shape),
                  full(ws.shape), full(bias_tile.shape), full(ones_bd.shape)],
        out_specs=pl.BlockSpec((1, tm, GROUP_W), lambda bi, i: (bi, i, 0)),
        out_shape=jax.ShapeDtypeStruct((b, l, GROUP_W), F32),
        compiler_params=_cparams(("parallel", "parallel"), 24 * tm * c2 * 4),
        name="gmlp_mixer",
    )(z, ln_g, ln_b, ws, bias_tile, ones_bd)


def _fnet_kernel(z_ref, cs_ref, cc_ref, sc_ref, w_ref, b_ref, o_ref, zcs_ref, *, l, scale):
    @pl.when(pl.program_id(1) == 0)
    def _():
        zb = z_ref[0].astype(BF16)
        zcs_ref[0:l, :] = jnp.dot(zb, cc_ref[...], preferred_element_type=F32).astype(BF16)
        zcs_ref[l:2 * l, :] = jnp.dot(zb, sc_ref[...], preferred_element_type=F32).astype(BF16)

    f = jnp.dot(cs_ref[...], zcs_ref[...], preferred_element_type=F32) * scale
    o_ref[0] = jnp.dot(f.astype(BF16), w_ref[...], preferred_element_type=F32) + b_ref[...]


def _fourier_mixer(z, cs, cc_bd, sc_bd, w_bd, bias, tm):
    b, l, c = z.shape
    full = lambda shape: pl.BlockSpec(shape, lambda bi, i: (0,) * len(shape))
    scale = 1.0 / math.sqrt(l * HEAD)
    vmem = 2 * (l * c * 4 + tm * 2 * l * 2 + tm * c * 4) + 2 * l * c * 2 + 6 * l * c * 4
    return pl.pallas_call(
        functools.partial(_fnet_kernel, l=l, scale=scale),
        grid=(b, l // tm),
        in_specs=[pl.BlockSpec((1, l, c), lambda bi, i: (bi, 0, 0)),
                  pl.BlockSpec((tm, 2 * l), lambda bi, i: (i, 0)),
                  full(cc_bd.shape), full(sc_bd.shape), full(w_bd.shape), full(bias.shape)],
        out_specs=pl.BlockSpec((1, tm, c), lambda bi, i: (bi, i, 0)),
        out_shape=jax.ShapeDtypeStruct((b, l, c), F32),
        scratch_shapes=[pltpu.VMEM((2 * l, c), BF16)],
        compiler_params=_cparams(("parallel", "arbitrary"), vmem),
        name="fourier_mixer",
    )(z, cs, cc_bd, sc_bd, w_bd, bias)


def _outproj_kernel(*refs, alpha, add_pos):
    if add_pos:
        a_ref, p_ref, g_ref, f_ref, x_ref, pos_ref, mod_ref, w_ref, lg_ref, lb_ref, o_ref = refs
        x = x_ref[0] + pos_ref[...]
    else:
        a_ref, p_ref, g_ref, f_ref, x_ref, mod_ref, w_ref, lg_ref, lb_ref, o_ref = refs
        x = x_ref[0]
    mix = jnp.dot(a_ref[0].astype(BF16), w_ref[0:GROUP_W, :], preferred_element_type=F32)
    for j, part in enumerate((p_ref, g_ref, f_ref), start=1):
        mix = mix + jnp.dot(part[0].astype(BF16), w_ref[j * GROUP_W:(j + 1) * GROUP_W, :],
                            preferred_element_type=F32)
    gate1 = mod_ref[0, 2:3, :]
    o_ref[0] = _layer_norm(alpha * x + gate1 * mix, lg_ref[...], lb_ref[...])


def _out_proj(parts, x, pos, mod, w_out, ln_g, ln_b, alpha, tm):
    b, l, d = x.shape
    add_pos = pos is not None
    pspec = pl.BlockSpec((1, tm, GROUP_W), lambda bi, i: (bi, i, 0))
    xspec = pl.BlockSpec((1, tm, d), lambda bi, i: (bi, i, 0))
    full = lambda shape: pl.BlockSpec(shape, lambda bi, i: (0,) * len(shape))
    in_specs = [pspec] * 4 + [xspec]
    args = list(parts) + [x]
    if add_pos:
        in_specs.append(pl.BlockSpec((tm, d), lambda bi, i: (i, 0)))
        args.append(pos)
    in_specs += [pl.BlockSpec((1, 6, d), lambda bi, i: (bi, 0, 0)), full(w_out.shape), full(ln_g.shape),
                 full(ln_b.shape)]
    args += [mod, w_out, ln_g, ln_b]
    vmem = 2 * (4 * tm * GROUP_W * 4 + 3 * tm * d * 4 + d * d * 2) + 6 * tm * d * 4
    return pl.pallas_call(
        functools.partial(_outproj_kernel, alpha=alpha, add_pos=add_pos),
        grid=(b, l // tm),
        in_specs=in_specs,
        out_specs=xspec,
        out_shape=jax.ShapeDtypeStruct((b, l, d), F32),
        compiler_params=_cparams(("parallel", "parallel"), vmem),
        name="out_proj_ln",
    )(*args)


def _ffn_kernel(x_ref, mod_ref, wg_ref, wu_ref, w2_ref, lg_ref, lb_ref, o_ref, *, alpha, f_chunks):
    x = x_ref[0]
    sh = mod_ref[0, 3:4, :]
    sc = mod_ref[0, 4:5, :]
    h = (x * (1.0 + sc) + sh).astype(BF16)
    acc = None
    for lo, hi in f_chunks:
        gate = jnp.dot(h, wg_ref[:, lo:hi], preferred_element_type=F32)
        up = jnp.dot(h, wu_ref[:, lo:hi], preferred_element_type=F32)
        act = (_silu(gate) * up).astype(BF16)
        part = jnp.dot(act, w2_ref[lo:hi, :], preferred_element_type=F32)
        acc = part if acc is None else acc + part
    gate2 = mod_ref[0, 5:6, :]
    o_ref[0] = _layer_norm(alpha * x + gate2 * acc, lg_ref[...], lb_ref[...])


def _ffn(x, mod, w1, w2, ln_g, ln_b, alpha, tm):
    b, l, d = x.shape
    f = w2.shape[0]
    mxu_n = 256
    half = (f // mxu_n + 1) // 2 * mxu_n
    f_chunks = ((0, half), (half, f))
    xspec = pl.BlockSpec((1, tm, d), lambda bi, i: (bi, i, 0))
    full = lambda shape: pl.BlockSpec(shape, lambda bi, i: (0,) * len(shape))
    once = pl.Buffered(1)
    vmem = 4 * tm * d * 4 + 3 * d * f * 2 + 3 * tm * d * 4 + 4 * tm * half * 4
    return pl.pallas_call(
        functools.partial(_ffn_kernel, alpha=alpha, f_chunks=f_chunks),
        grid=(b, l // tm),
        in_specs=[xspec,
                  pl.BlockSpec((1, 6, d), lambda bi, i: (bi, 0, 0)),
                  pl.BlockSpec((d, f), lambda bi, i: (0, 0), pipeline_mode=once),
                  pl.BlockSpec((d, f), lambda bi, i: (0, 1), pipeline_mode=once),
                  pl.BlockSpec((f, d), lambda bi, i: (0, 0), pipeline_mode=once),
                  full(ln_g.shape), full(ln_b.shape)],
        out_specs=xspec,
        out_shape=jax.ShapeDtypeStruct((b, l, d), F32),
        compiler_params=_cparams(("parallel", "parallel"), vmem),
        name="ffn_ln",
    )(x, mod, w1, w1, w2, ln_g, ln_b)


def _pos_embed(n_tok, dim):
    rows = n_tok // GRID_W
    row, col = jnp.meshgrid(jnp.arange(rows, dtype=F32), jnp.arange(GRID_W, dtype=F32), indexing='ij')
    quarter = dim // 4
    freqs = jnp.exp(-math.log(10000.0) * jnp.arange(quarter, dtype=F32) / quarter)

    def enc(p):
        ang = p.reshape(-1, 1) * freqs[None, :]
        return jnp.concatenate([jnp.sin(ang), jnp.cos(ang)], -1)

    return jnp.concatenate([enc(row), enc(col)], -1)


def _block_diag(blocks):
    g, n, m = blocks.shape
    eye = jnp.eye(g, dtype=blocks.dtype)
    return (eye[:, None, :, None] * blocks[:, :, None, :]).reshape(g * n, g * m)


def _dft_tables(n):
    def direct(rows, cols, period):
        ang = ((rows[:, None] * cols[None, :]) % period).astype(F32) * (2.0 * math.pi / period)
        return jnp.cos(ang), jnp.sin(ang)

    idx = jnp.arange(n, dtype=jnp.int32)
    if n <= 1024:
        return direct(idx, idx, n)
    lo = 64
    ca, sa = direct(jnp.arange(n // lo, dtype=jnp.int32), idx, n // lo)
    cb, sb = direct(jnp.arange(lo, dtype=jnp.int32), idx, n)
    ca, sa, cb, sb = ca[:, None, :], sa[:, None, :], cb[None], sb[None]
    return (ca * cb - sa * sb).reshape(n, n), (sa * cb + ca * sb).reshape(n, n)


def _to_scan(x, b):
    t = x.shape[1]
    x5 = x.reshape(b, t, -1, N_HEADS, HEAD)
    x5 = jnp.broadcast_to(x5, (b, t, 2, N_HEADS, HEAD))
    return x5.transpose(1, 4, 2, 0, 3).reshape(t, HEAD, 2 * b * N_HEADS)


def _from_scan(yf, yb, b):
    t = yf.shape[0]
    lane = lax.broadcasted_iota(jnp.int32, yf.shape, 2)
    y = jnp.where(lane < b * N_HEADS, yf, yb)
    return y.reshape(t, HEAD, 2, b, N_HEADS).transpose(3, 0, 2, 4, 1).reshape(b, t, 2 * GROUP_W)


def kernel(x, c, ctx, c_ctx, w_mod, b_mod, w_in, rkv_conv, decay_w0, decay_w2, iclr_a0, iclr_a2,
           gate_g2, k_k, k_a, r_k, gn_g, gn_b, pool_w, pool_scale, gmlp_ln_g, gmlp_ln_b, gmlp_ws,
           gmlp_bs, fnet_w, fnet_b, w_out, ln1_g, ln1_b, ln2_g, ln2_b, ffn_w1, ffn_w2):
    bsz, seq, d = x.shape
    ctx_len = ctx.shape[1]
    depth = w_in.shape[0]
    d_ff = ffn_w2.shape[1]
    alpha = (2 * depth) ** 0.25
    rwkv_cols = 3 * GROUP_W + LORA_W
    assert bsz * N_HEADS * 2 == V7X_LANES, "scan layout packs (direction, batch, head) on the 128 lanes"

    pos = _pos_embed(seq, d)
    ones_bd = _block_diag(jnp.ones((N_HEADS, HEAD, HEAD), F32)).astype(BF16)
    cc64, ss64 = _dft_tables(HEAD)
    cc_bd = _block_diag(jnp.tile(cc64[None], (4, 1, 1))).astype(BF16)
    sc_bd = _block_diag(jnp.tile(ss64[None], (4, 1, 1))).astype(BF16)

    def dft_rows(n):
        cn, sn = _dft_tables(n)
        return jnp.concatenate([cn, -sn], axis=1).astype(BF16)

    cs_x = dft_rows(seq)
    cs_c = dft_rows(ctx_len)

    pad_rows = (-(bsz + 1)) % V7X_SUBLANES
    c_all = jnp.concatenate([c, c_ctx[None], jnp.zeros((pad_rows, d), F32)], axis=0)

    tm_x = 512
    tm_c = ctx_len
    tb = 32

    xs, cs = x, ctx
    for l in range(depth):
        last = l == depth - 1
        first = l == 0
        m = _modulation(c_all, w_mod[l].astype(BF16), b_mod[l])
        mod_x = m[:bsz].reshape(bsz, 6, d)
        mod_c = jnp.broadcast_to(m[bsz].reshape(1, 6, d), (bsz, 6, d))

        w_in_l = w_in[l].astype(BF16)
        widths_full = (3 * GROUP_W, LORA_W, GROUP_W, 2 * GROUP_W, GROUP_W)
        px = _in_proj(xs, pos if first else None, mod_x, w_in_l, widths_full, tm_x)
        if last:
            pc = _in_proj(cs, None, mod_c, w_in_l[:, :rwkv_cols], widths_full[:2], tm_c)
        else:
            pc = _in_proj(cs, None, mod_c, w_in_l, widths_full, tm_c)

        wl = jnp.zeros((LORA_W, 5 * GROUP_W), F32)
        wl = wl.at[0:64, 0:256].set(decay_w2[l, 0]).at[64:128, 256:512].set(decay_w2[l, 1])
        wl = wl.at[128:192, 512:768].set(iclr_a2[l, 0]).at[192:256, 768:1024].set(iclr_a2[l, 1])
        wl = wl.at[256:384, 1024:1280].set(gate_g2[l]).astype(BF16)
        prep_args = (rkv_conv[l], wl, decay_w0[l], iclr_a0[l], k_k[l].reshape(1, -1), k_a[l].reshape(1, -1),
                     ones_bd)
        p_r, p_v, p_kk, p_gate, p_w2, p_b2, p_kd2 = _rwkv_prep(px[0], px[1], pc[0], pc[1], *prep_args)
        yf_s, yb_s = _rwkv_scan(*(_to_scan(a, bsz) for a in (p_r, p_v, p_kk, p_w2, p_b2, p_kd2)), tb, seq, ctx_len)
        y2 = _from_scan(yf_s, yb_s, bsz)

        pool_bd = _block_diag(pool_w[l]).astype(BF16)
        fnet_bd = _block_diag(fnet_w[l]).astype(BF16)
        ws_b = gmlp_ws[l].astype(BF16)
        bias_tile = jnp.repeat(gmlp_bs[l].T, HEAD, axis=1)
        row = lambda a: a.reshape(1, -1)

        def mixers(parts, row0, n_rows, cs_tab, tm, tmf):
            a = _rwkv_readout(y2, p_r, p_v, p_kd2, p_gate, row(r_k[l]), row(gn_g[l]), row(gn_b[l]), ones_bd,
                              tm, row0, n_rows)
            p = _pool_mixer(parts[2], pool_bd, row(pool_scale[l]))
            g = _gmlp_mixer(parts[3], row(gmlp_ln_g[l]), row(gmlp_ln_b[l]), ws_b, bias_tile, ones_bd, tm)
            f = _fourier_mixer(parts[4], cs_tab, cc_bd, sc_bd, fnet_bd, row(fnet_b[l]), tmf)
            return a, p, g, f

        w_out_l = w_out[l].astype(BF16)
        w1_l = ffn_w1[l].astype(BF16)
        w2_l = ffn_w2[l].astype(BF16)
        parts_x = mixers(px, 0, seq, cs_x, tm_x, 256)
        xs = _out_proj(parts_x, xs, pos if first else None, mod_x, w_out_l, row(ln1_g[l]), row(ln1_b[l]), alpha, tm_x)
        xs = _ffn(xs, mod_x, w1_l, w2_l, row(ln2_g[l]), row(ln2_b[l]), alpha, tm_x)
        if not last:
            parts_c = mixers(pc, seq, ctx_len, cs_c, tm_c, tm_c)
            cs = _out_proj(parts_c, cs, None, mod_c, w_out_l, row(ln1_g[l]), row(ln1_b[l]), alpha, tm_c)
            cs = _ffn(cs, mod_c, w1_l, w2_l, row(ln2_g[l]), row(ln2_b[l]), alpha, tm_c)
    return xs
```

```python
import functools
import math

import jax
import jax.numpy as jnp
from jax import lax
from jax.experimental import pallas as pl
from jax.experimental.pallas import tpu as pltpu

F32 = jnp.float32
BF16 = jnp.bfloat16

V7X_LANES = 128
V7X_SUBLANES = 8
V7X_VMEM_BYTES = 64 * 1024 * 1024
V7X_VMEM_LIMIT_CAP = 56 * 1024 * 1024

GRID_W = 64
HEAD = 64
GROUP_W = 256
N_HEADS = GROUP_W // HEAD
LORA_W = 384
POOL_WINDOWS = (2, 4, 8, 16)
CHUNK = 128
LN_EPS = 1e-5
GN_EPS = 64e-5


def _cparams(sem, vmem_bytes):
    limit = int(min(max(vmem_bytes, 16 * 1024 * 1024), V7X_VMEM_LIMIT_CAP))
    return pltpu.CompilerParams(dimension_semantics=sem, vmem_limit_bytes=limit)


def _silu(x):
    return x * jax.nn.sigmoid(x)


def _seg_sum(x, ones_bd):
    hi = x.astype(BF16)
    lo = (x - hi.astype(F32)).astype(BF16)
    return (jnp.dot(hi, ones_bd, preferred_element_type=F32)
            + jnp.dot(lo, ones_bd, preferred_element_type=F32))


def _layer_norm(z, g, b):
    mu = jnp.mean(z, axis=-1, keepdims=True)
    zc = z - mu
    var = jnp.mean(zc * zc, axis=-1, keepdims=True)
    return zc * lax.rsqrt(var + LN_EPS) * g + b


def _mod_kernel(c_ref, w_ref, b_ref, o_ref):
    s = _silu(c_ref[...])
    o_ref[...] = jnp.dot(s.astype(BF16), w_ref[...], preferred_element_type=F32) + b_ref[...]


def _modulation(c_all, w_mod, b_mod):
    rows, d = c_all.shape
    n = w_mod.shape[1]
    tn = 1024
    return pl.pallas_call(
        _mod_kernel,
        grid=(n // tn,),
        in_specs=[pl.BlockSpec((rows, d), lambda j: (0, 0)),
                  pl.BlockSpec((d, tn), lambda j: (0, j)),
                  pl.BlockSpec((1, tn), lambda j: (0, j))],
        out_specs=pl.BlockSpec((rows, tn), lambda j: (0, j)),
        out_shape=jax.ShapeDtypeStruct((rows, n), F32),
        compiler_params=_cparams(("parallel",), 4 * d * tn * 2),
        name="modulation",
    )(c_all, w_mod, b_mod.reshape(1, n))


def _inproj_kernel(*refs, widths, add_pos):
    if add_pos:
        x_ref, pos_ref, mod_ref, w_ref = refs[:4]
        outs = refs[4:]
        x = x_ref[0] + pos_ref[...]
    else:
        x_ref, mod_ref, w_ref = refs[:3]
        outs = refs[3:]
        x = x_ref[0]
    sh = mod_ref[0, 0:1, :]
    sc = mod_ref[0, 1:2, :]
    h = (x * (1.0 + sc) + sh).astype(BF16)
    off = 0
    for o_ref, wd in zip(outs, widths):
        o_ref[0] = jnp.dot(h, w_ref[:, off:off + wd], preferred_element_type=F32)
        off += wd


def _in_proj(x, pos, mod, w, widths, tm):
    b, l, d = x.shape
    n = w.shape[1]
    assert sum(widths) == n
    add_pos = pos is not None
    in_specs = [pl.BlockSpec((1, tm, d), lambda bi, i: (bi, i, 0))]
    args = [x]
    if add_pos:
        in_specs.append(pl.BlockSpec((tm, d), lambda bi, i: (i, 0)))
        args.append(pos)
    in_specs += [pl.BlockSpec((1, 6, d), lambda bi, i: (bi, 0, 0)),
                 pl.BlockSpec((d, n), lambda bi, i: (0, 0))]
    args += [mod, w]
    vmem = 2 * (2 * tm * d * 4 + d * n * 2 + tm * n * 4) + tm * d * 8
    return pl.pallas_call(
        functools.partial(_inproj_kernel, widths=widths, add_pos=add_pos),
        grid=(b, l // tm),
        in_specs=in_specs,
        out_specs=[pl.BlockSpec((1, tm, wd), lambda bi, i: (bi, i, 0)) for wd in widths],
        out_shape=[jax.ShapeDtypeStruct((b, l, wd), F32) for wd in widths],
        compiler_params=_cparams(("parallel", "parallel"), vmem),
        name="in_proj",
    )(*args)


def _prep_kernel(zx_ref, zp_ref, zn_ref, zc_ref, lox_ref, loc_ref, conv_ref, wl_ref, w0_ref, a0_ref, kk_ref,
                 ka_ref, ones_ref, r_o, v_o, kk_o, gate_o, w_o, b_o, kd_o, *, tm, nx):
    i = pl.program_id(1)
    is_ctx = i >= nx
    z = jnp.where(is_ctx, zc_ref[0], zx_ref[0])
    lo = jnp.where(is_ctx, loc_ref[0], lox_ref[0])
    row = lax.broadcasted_iota(jnp.int32, z.shape, 0)
    has_prev = jnp.logical_and(i > 0, i < nx).astype(F32)
    has_next = (i < nx - 1).astype(F32)
    prev_row = zp_ref[0, V7X_SUBLANES - 1:V7X_SUBLANES, :] * has_prev
    next_row = zn_ref[0, 0:1, :] * has_next
    zm1 = jnp.where(row == 0, prev_row, pltpu.roll(z, 1, 0))
    zp1 = jnp.where(row == tm - 1, next_row, pltpu.roll(z, tm - 1, 0))
    c = zm1 * conv_ref[0:1, :] + z * conv_ref[1:2, :] + zp1 * conv_ref[2:3, :]
    r = c[:, 0:GROUP_W]
    k = c[:, GROUP_W:2 * GROUP_W]
    v = c[:, 2 * GROUP_W:3 * GROUP_W]

    col = lax.broadcasted_iota(jnp.int32, lo.shape, 1)
    act = jnp.where(col < 128, jnp.tanh(lo), jnp.where(col < 256, lo, jax.nn.sigmoid(lo)))
    pre = jnp.dot(act.astype(BF16), wl_ref[...], preferred_element_type=F32)

    ones_bd = ones_ref[...]
    kk = k * kk_ref[...]
    nrm = jnp.sqrt(_seg_sum(kk * kk, ones_bd))
    kk = kk / jnp.maximum(nrm, 1e-12)
    gate_o[0] = pre[:, 4 * GROUP_W:5 * GROUP_W]
    ka = ka_ref[...]
    for d in range(2):
        cols = slice(d * GROUP_W, (d + 1) * GROUP_W)
        r_o[0, :, cols] = r
        v_o[0, :, cols] = v
        kk_o[0, :, cols] = kk
        xw = w0_ref[d:d + 1, :] + pre[:, cols]
        w_o[0, :, cols] = jnp.exp(-math.exp(-0.5) * jax.nn.sigmoid(xw))
        a = jax.nn.sigmoid(a0_ref[d:d + 1, :] + pre[:, (2 + d) * GROUP_W:(3 + d) * GROUP_W])
        kd_o[0, :, cols] = k * (1.0 + (a - 1.0) * ka)
        b_o[0, :, cols] = kk * a


def _rwkv_prep(zx_rkv, zx_lora, zc_rkv, zc_lora, conv, wl, w0, a0, k_k, k_a, ones_bd):
    b, l, c3 = zx_rkv.shape
    tm = zc_rkv.shape[1]
    nx = l // tm
    nblk8 = l // V7X_SUBLANES
    r8 = tm // V7X_SUBLANES
    full = lambda shape: pl.BlockSpec(shape, lambda bi, i: (0,) * len(shape))
    xi = lambda i: jnp.minimum(i, nx - 1)
    in_specs = [
        pl.BlockSpec((1, tm, c3), lambda bi, i: (bi, xi(i), 0)),
        pl.BlockSpec((1, V7X_SUBLANES, c3), lambda bi, i: (bi, jnp.clip(i * r8 - 1, 0, nblk8 - 1), 0)),
        pl.BlockSpec((1, V7X_SUBLANES, c3), lambda bi, i: (bi, jnp.minimum((i + 1) * r8, nblk8 - 1), 0)),
        pl.BlockSpec((1, tm, c3), lambda bi, i: (bi, 0, 0)),
        pl.BlockSpec((1, tm, LORA_W), lambda bi, i: (bi, xi(i), 0)),
        pl.BlockSpec((1, tm, LORA_W), lambda bi, i: (bi, 0, 0)),
        full(conv.shape), full(wl.shape), full(w0.shape), full(a0.shape),
        full(k_k.shape), full(k_a.shape), full(ones_bd.shape),
    ]
    spec1 = pl.BlockSpec((1, tm, GROUP_W), lambda bi, i: (bi, i, 0))
    spec2 = pl.BlockSpec((1, tm, 2 * GROUP_W), lambda bi, i: (bi, i, 0))
    shp1 = jax.ShapeDtypeStruct((b, l + tm, GROUP_W), F32)
    shp2 = jax.ShapeDtypeStruct((b, l + tm, 2 * GROUP_W), F32)
    vmem = 2 * (2 * tm * (c3 + LORA_W) * 4 + 13 * tm * GROUP_W * 4) + 14 * tm * c3 * 4
    return pl.pallas_call(
        functools.partial(_prep_kernel, tm=tm, nx=nx),
        grid=(b, nx + 1),
        in_specs=in_specs,
        out_specs=[spec2] * 3 + [spec1] + [spec2] * 3,
        out_shape=[shp2] * 3 + [shp1] + [shp2] * 3,
        compiler_params=_cparams(("parallel", "arbitrary"), vmem),
        name="rwkv_prep",
    )(zx_rkv, zx_rkv, zx_rkv, zc_rkv, zx_lora, zc_lora, conv, wl, w0, a0, k_k, k_a, ones_bd)


SCAN_ROW_PITCH = 72


def _to_scan_kernel(x_ref, o_ref, y_ref, *, nb, tt):
    groups = 2 * N_HEADS
    for b in range(nb):
        xt = x_ref[b].T
        for g in range(groups):
            y_ref[pl.ds((b * groups + g) * SCAN_ROW_PITCH, HEAD), :] = xt[g * HEAD:(g + 1) * HEAD, :]
    for k in range(HEAD):
        z = y_ref[pl.ds(k, nb * groups, stride=SCAN_ROW_PITCH), :]
        o_ref[k] = z.T


def _to_scan_layout(x):
    nb, t, c = x.shape
    tt = V7X_LANES
    assert nb * 2 * N_HEADS == V7X_LANES and c == 2 * GROUP_W and t % tt == 0
    return pl.pallas_call(
        functools.partial(_to_scan_kernel, nb=nb, tt=tt),
        grid=(t // tt,),
        in_specs=[pl.BlockSpec((nb, tt, c), lambda i: (0, i, 0))],
        out_specs=pl.BlockSpec((HEAD, tt, V7X_LANES), lambda i: (0, i, 0)),
        out_shape=jax.ShapeDtypeStruct((HEAD, t, V7X_LANES), F32),
        scratch_shapes=[pltpu.VMEM((V7X_LANES * SCAN_ROW_PITCH, tt), F32)],
        compiler_params=_cparams(("parallel",), 2 * (nb * tt * c * 4 + HEAD * tt * V7X_LANES * 4)
                                 + 3 * V7X_LANES * SCAN_ROW_PITCH * tt * 4),
        name="to_scan_layout",
    )(x)


def _scan_kernel(rf, rb, vf, vb, kf, kb, wf, wb, bf, bb, df, db, yf_ref, yb_ref, h_ref, vt_ref, *, tb):
    @pl.when(pl.program_id(0) == 0)
    def _():
        h_ref[...] = jnp.zeros_like(h_ref)

    tile = (V7X_SUBLANES, V7X_LANES)
    fwd_lane = (lax.broadcasted_iota(jnp.int32, tile, 1) & N_HEADS) == 0

    def row(f_ref, b_ref, k, t, tr):
        one = jnp.where(fwd_lane, jnp.broadcast_to(f_ref[k, pl.ds(t, 1), :], tile),
                        jnp.broadcast_to(b_ref[k, pl.ds(tr, 1), :], tile))
        return jnp.concatenate([one] * (HEAD // V7X_SUBLANES), axis=0)

    u0 = jnp.zeros((HEAD, V7X_LANES), F32)
    for k in range(HEAD):
        u0 = u0 + row(kf, kb, k, 0, tb - 1) * h_ref[k]

    def step(t, u):
        tr = tb - 1 - t
        tn = jnp.minimum(t + 1, tb - 1)
        trn = jnp.maximum(tr - 1, 0)
        for v in range(HEAD):
            vt_ref[v:v + 1, :] = jnp.where(fwd_lane[0:1], vf[v, pl.ds(t, 1), :], vb[v, pl.ds(tr, 1), :])
        vv = vt_ref[...]
        y = jnp.zeros((HEAD, V7X_LANES), F32)
        un = jnp.zeros((HEAD, V7X_LANES), F32)
        for k in range(HEAD):
            hn = h_ref[k] * row(wf, wb, k, t, tr) - row(bf, bb, k, t, tr) * u + row(df, db, k, t, tr) * vv
            h_ref[k] = hn
            y = y + row(rf, rb, k, t, tr) * hn
            un = un + row(kf, kb, k, tn, trn) * hn
        yf_ref[t] = y
        yb_ref[tr] = y
        return un

    lax.fori_loop(0, tb, step, u0)


def _rwkv_scan(r, v, kk, w, b, kd, tb, n_lat, n_ctx):
    t_tot = r.shape[1]
    nl, nc = n_lat // tb, n_ctx // tb
    nblk = nl + nc
    fblk = lambda i: jnp.where(i < nc, nl + i, i - nc)
    bblk = lambda i: nblk - 1 - i
    fspec = pl.BlockSpec((HEAD, tb, V7X_LANES), lambda i: (0, fblk(i), 0))
    bspec = pl.BlockSpec((HEAD, tb, V7X_LANES), lambda i: (0, bblk(i), 0))
    blk = tb * HEAD * V7X_LANES * 4
    shp = jax.ShapeDtypeStruct((t_tot, HEAD, V7X_LANES), F32)
    return pl.pallas_call(
        functools.partial(_scan_kernel, tb=tb),
        grid=(nblk,),
        in_specs=[fspec, bspec] * 6,
        out_specs=[pl.BlockSpec((tb, HEAD, V7X_LANES), lambda i: (fblk(i), 0, 0)),
                   pl.BlockSpec((tb, HEAD, V7X_LANES), lambda i: (bblk(i), 0, 0))],
        out_shape=[shp, shp],
        scratch_shapes=[pltpu.VMEM((HEAD, HEAD, V7X_LANES), F32), pltpu.VMEM((HEAD, V7X_LANES), F32)],
        compiler_params=_cparams(("arbitrary",), 2 * 14 * blk + 3 * HEAD * HEAD * V7X_LANES * 4),
        name="rwkv_scan",
    )(r, r, v, v, kk, kk, w, w, b, b, kd, kd)


def _readout_kernel(y_ref, r_ref, v_ref, kd_ref, gate_ref, rk_ref, g_ref, b_ref, ones_ref, o_ref):
    ones_bd = ones_ref[...]
    inv = 1.0 / HEAD
    y = y_ref[0, :, 0:GROUP_W] + y_ref[0, :, GROUP_W:2 * GROUP_W]
    kd = kd_ref[0, :, 0:GROUP_W] + kd_ref[0, :, GROUP_W:2 * GROUP_W]
    mu = _seg_sum(y, ones_bd) * inv
    yc = y - mu
    var = _seg_sum(yc * yc, ones_bd) * inv
    yn = yc * lax.rsqrt(var + GN_EPS)
    bonus = _seg_sum(r_ref[0] * kd * rk_ref[...], ones_bd) * v_ref[0]
    o_ref[0] = (yn * g_ref[...] + b_ref[...] + bonus) * gate_ref[0]


def _rwkv_readout(y2, r, v, kd2, gate, r_k, gn_g, gn_b, ones_bd, tm, row0, n_rows):
    b = y2.shape[0]
    off = row0 // tm
    spec1 = pl.BlockSpec((1, tm, GROUP_W), lambda bi, i: (bi, i + off, 0))
    spec2 = pl.BlockSpec((1, tm, 2 * GROUP_W), lambda bi, i: (bi, i + off, 0))
    full = lambda shape: pl.BlockSpec(shape, lambda bi, i: (0,) * len(shape))
    return pl.pallas_call(
        _readout_kernel,
        grid=(b, n_rows // tm),
        in_specs=[spec2, spec1, spec1, spec2, spec1,
                  full(r_k.shape), full(gn_g.shape), full(gn_b.shape), full(ones_bd.shape)],
        out_specs=pl.BlockSpec((1, tm, GROUP_W), lambda bi, i: (bi, i, 0)),
        out_shape=jax.ShapeDtypeStruct((b, n_rows, GROUP_W), F32),
        compiler_params=_cparams(("parallel", "parallel"), 36 * tm * GROUP_W * 4),
        name="rwkv_readout",
    )(y2, r, v, kd2, gate, r_k, gn_g, gn_b, ones_bd)


def _pool_kernel(z_ref, w_ref, s_ref, o_ref, *, l):
    z = z_ref[0]
    row = lax.broadcasted_iota(jnp.int32, z.shape, 0)
    lane = lax.broadcasted_iota(jnp.int32, z.shape, 1)

    def shifted(x, o):
        y = pltpu.roll(x, (-o) % l, 0)
        return jnp.where((row + o >= 0) & (row + o < l), y, 0.0)

    def count(wd):
        lo = jnp.maximum(row - wd // 2, 0)
        hi = jnp.minimum(row + wd - wd // 2, l)
        return (hi - lo).astype(F32)

    s2 = shifted(z, -1) + z
    pooled = s2 / count(2)
    s4 = shifted(s2, -1) + shifted(s2, 1)
    pooled = jnp.where(lane >= 64, s4 / count(4), pooled)
    s8 = shifted(s4, -2) + shifted(s4, 2)
    pooled = jnp.where(lane >= 128, s8 / count(8), pooled)
    s16 = shifted(s8, -4) + shifted(s8, 4)
    pooled = jnp.where(lane >= 192, s16 / count(16), pooled)
    d = pooled - z
    o_ref[0] = jnp.dot(d.astype(BF16), w_ref[...], preferred_element_type=F32) * s_ref[...]


def _pool_mixer(z, w_bd, scale):
    b, l, c = z.shape
    spec = pl.BlockSpec((1, l, c), lambda bi: (bi, 0, 0))
    return pl.pallas_call(
        functools.partial(_pool_kernel, l=l),
        grid=(b,),
        in_specs=[spec, pl.BlockSpec(w_bd.shape, lambda bi: (0, 0)), pl.BlockSpec(scale.shape, lambda bi: (0, 0))],
        out_specs=spec,
        out_shape=jax.ShapeDtypeStruct((b, l, c), F32),
        compiler_params=_cparams(("parallel",), 20 * l * c * 4),
        name="pool_mixer",
    )(z, w_bd, scale)


def _gelu_tanh(x):
    return 0.5 * x * (1.0 + jnp.tanh(math.sqrt(2.0 / math.pi) * (x + 0.044715 * (x * x * x))))


def _gmlp_kernel(z_ref, g_ref, b_ref, ws_ref, bias_ref, ones_ref, o_ref, *, tm):
    ones_bd = ones_ref[...]
    inv = 1.0 / HEAD
    z = z_ref[0]
    u = _gelu_tanh(z[:, 0:GROUP_W])
    v = _gelu_tanh(z[:, GROUP_W:2 * GROUP_W])
    mu = _seg_sum(v, ones_bd) * inv
    vc = v - mu
    var = _seg_sum(vc * vc, ones_bd) * inv
    vn = (vc * lax.rsqrt(var + LN_EPS) * g_ref[...] + b_ref[...]).astype(BF16)
    lane = lax.broadcasted_iota(jnp.int32, (CHUNK, GROUP_W), 1)
    for c in range(tm // CHUNK):
        vchunk = vn[c * CHUNK:(c + 1) * CHUNK, :]
        sv = bias_ref[...]
        for g in range(4):
            part = jnp.dot(ws_ref[g], vchunk, preferred_element_type=F32)
            sv = sv + jnp.where((lane >= g * HEAD) & (lane < (g + 1) * HEAD), part, 0.0)
        o_ref[0, c * CHUNK:(c + 1) * CHUNK, :] = u[c * CHUNK:(c + 1) * CHUNK, :] * sv


def _gmlp_mixer(z, ln_g, ln_b, ws, bias_tile, ones_bd, tm):
    b, l, c2 = z.shape
    full = lambda shape: pl.BlockSpec(shape, lambda bi, i: (0,) * len(shape))
    return pl.pallas_call(
        functools.partial(_gmlp_kernel, tm=tm),
        grid=(b, l // tm),
        in_specs=[pl.BlockSpec((1, tm, c2), lambda bi, i: (bi, i, 0)), full(ln_g.shape), full(ln_b.shape),
                  full(ws.shape), full(bias_tile.shape), full(ones_bd.shape)],
        out_specs=pl.BlockSpec((1, tm, GROUP_W), lambda bi, i: (bi, i, 0)),
        out_shape=jax.ShapeDtypeStruct((b, l, GROUP_W), F32),
        compiler_params=_cparams(("parallel", "parallel"), 24 * tm * c2 * 4),
        name="gmlp_mixer",
    )(z, ln_g, ln_b, ws, bias_tile, ones_bd)


def _fnet_kernel(z_ref, cs_ref, cc_ref, sc_ref, w_ref, b_ref, o_ref, zcs_ref, *, l, scale):
    @pl.when(pl.program_id(1) == 0)
    def _():
        zb = z_ref[0].astype(BF16)
        zcs_ref[0:l, :] = jnp.dot(zb, cc_ref[...], preferred_element_type=F32).astype(BF16)
        zcs_ref[l:2 * l, :] = jnp.dot(zb, sc_ref[...], preferred_element_type=F32).astype(BF16)

    f = jnp.dot(cs_ref[...], zcs_ref[...], preferred_element_type=F32) * scale
    o_ref[0] = jnp.dot(f.astype(BF16), w_ref[...], preferred_element_type=F32) + b_ref[...]


def _fourier_mixer(z, cs, cc_bd, sc_bd, w_bd, bias, tm):
    b, l, c = z.shape
    full = lambda shape: pl.BlockSpec(shape, lambda bi, i: (0,) * len(shape))
    scale = 1.0 / math.sqrt(l * HEAD)
    vmem = 2 * (l * c * 4 + tm * 2 * l * 2 + tm * c * 4) + 2 * l * c * 2 + 6 * l * c * 4
    return pl.pallas_call(
        functools.partial(_fnet_kernel, l=l, scale=scale),
        grid=(b, l // tm),
        in_specs=[pl.BlockSpec((1, l, c), lambda bi, i: (bi, 0, 0)),
                  pl.BlockSpec((tm, 2 * l), lambda bi, i: (i, 0)),
                  full(cc_bd.shape), full(sc_bd.shape), full(w_bd.shape), full(bias.shape)],
        out_specs=pl.BlockSpec((1, tm, c), lambda bi, i: (bi, i, 0)),
        out_shape=jax.ShapeDtypeStruct((b, l, c), F32),
        scratch_shapes=[pltpu.VMEM((2 * l, c), BF16)],
        compiler_params=_cparams(("parallel", "arbitrary"), vmem),
        name="fourier_mixer",
    )(z, cs, cc_bd, sc_bd, w_bd, bias)


def _outproj_kernel(*refs, alpha, add_pos):
    if add_pos:
        a_ref, p_ref, g_ref, f_ref, x_ref, pos_ref, mod_ref, w_ref, lg_ref, lb_ref, o_ref = refs
        x = x_ref[0] + pos_ref[...]
    else:
        a_ref, p_ref, g_ref, f_ref, x_ref, mod_ref, w_ref, lg_ref, lb_ref, o_ref = refs
        x = x_ref[0]
    mix = jnp.dot(a_ref[0].astype(BF16), w_ref[0:GROUP_W, :], preferred_element_type=F32)
    for j, part in enumerate((p_ref, g_ref, f_ref), start=1):
        mix = mix + jnp.dot(part[0].astype(BF16), w_ref[j * GROUP_W:(j + 1) * GROUP_W, :],
                            preferred_element_type=F32)
    gate1 = mod_ref[0, 2:3, :]
    o_ref[0] = _layer_norm(alpha * x + gate1 * mix, lg_ref[...], lb_ref[...])


def _out_proj(parts, x, pos, mod, w_out, ln_g, ln_b, alpha, tm):
    b, l, d = x.shape
    add_pos = pos is not None
    pspec = pl.BlockSpec((1, tm, GROUP_W), lambda bi, i: (bi, i, 0))
    xspec = pl.BlockSpec((1, tm, d), lambda bi, i: (bi, i, 0))
    full = lambda shape: pl.BlockSpec(shape, lambda bi, i: (0,) * len(shape))
    in_specs = [pspec] * 4 + [xspec]
    args = list(parts) + [x]
    if add_pos:
        in_specs.append(pl.BlockSpec((tm, d), lambda bi, i: (i, 0)))
        args.append(pos)
    in_specs += [pl.BlockSpec((1, 6, d), lambda bi, i: (bi, 0, 0)), full(w_out.shape), full(ln_g.shape),
                 full(ln_b.shape)]
    args += [mod, w_out, ln_g, ln_b]
    vmem = 2 * (4 * tm * GROUP_W * 4 + 3 * tm * d * 4 + d * d * 2) + 6 * tm * d * 4
    return pl.pallas_call(
        functools.partial(_outproj_kernel, alpha=alpha, add_pos=add_pos),
        grid=(b, l // tm),
        in_specs=in_specs,
        out_specs=xspec,
        out_shape=jax.ShapeDtypeStruct((b, l, d), F32),
        compiler_params=_cparams(("parallel", "parallel"), vmem),
        name="out_proj_ln",
    )(*args)


def _ffn_kernel(x_ref, mod_ref, wg_ref, wu_ref, w2_ref, lg_ref, lb_ref, o_ref, *, alpha, f_chunks):
    x = x_ref[0]
    sh = mod_ref[0, 3:4, :]
    sc = mod_ref[0, 4:5, :]
    h = (x * (1.0 + sc) + sh).astype(BF16)
    acc = None
    for lo, hi in f_chunks:
        gate = jnp.dot(h, wg_ref[:, lo:hi], preferred_element_type=F32)
        up = jnp.dot(h, wu_ref[:, lo:hi], preferred_element_type=F32)
        act = (_silu(gate) * up).astype(BF16)
        part = jnp.dot(act, w2_ref[lo:hi, :], preferred_element_type=F32)
        acc = part if acc is None else acc + part
    gate2 = mod_ref[0, 5:6, :]
    o_ref[0] = _layer_norm(alpha * x + gate2 * acc, lg_ref[...], lb_ref[...])


def _ffn(x, mod, w1, w2, ln_g, ln_b, alpha, tm):
    b, l, d = x.shape
    f = w2.shape[0]
    mxu_n = 256
    half = (f // mxu_n + 1) // 2 * mxu_n
    f_chunks = ((0, half), (half, f))
    xspec = pl.BlockSpec((1, tm, d), lambda bi, i: (bi, i, 0))
    full = lambda shape: pl.BlockSpec(shape, lambda bi, i: (0,) * len(shape))
    once = pl.Buffered(1)
    vmem = 4 * tm * d * 4 + 3 * d * f * 2 + 3 * tm * d * 4 + 4 * tm * half * 4
    return pl.pallas_call(
        functools.partial(_ffn_kernel, alpha=alpha, f_chunks=f_chunks),
        grid=(b, l // tm),
        in_specs=[xspec,
                  pl.BlockSpec((1, 6, d), lambda bi, i: (bi, 0, 0)),
                  pl.BlockSpec((d, f), lambda bi, i: (0, 0), pipeline_mode=once),
                  pl.BlockSpec((d, f), lambda bi, i: (0, 1), pipeline_mode=once),
                  pl.BlockSpec((f, d), lambda bi, i: (0, 0), pipeline_mode=once),
                  full(ln_g.shape), full(ln_b.shape)],
        out_specs=xspec,
        out_shape=jax.ShapeDtypeStruct((b, l, d), F32),
        compiler_params=_cparams(("parallel", "parallel"), vmem),
        name="ffn_ln",
    )(x, mod, w1, w1, w2, ln_g, ln_b)


def _pos_embed(n_tok, dim):
    rows = n_tok // GRID_W
    row, col = jnp.meshgrid(jnp.arange(rows, dtype=F32), jnp.arange(GRID_W, dtype=F32), indexing='ij')
    quarter = dim // 4
    freqs = jnp.exp(-math.log(10000.0) * jnp.arange(quarter, dtype=F32) / quarter)

    def enc(p):
        ang = p.reshape(-1, 1) * freqs[None, :]
        return jnp.concatenate([jnp.sin(ang), jnp.cos(ang)], -1)

    return jnp.concatenate([enc(row), enc(col)], -1)


def _block_diag(blocks):
    g, n, m = blocks.shape
    eye = jnp.eye(g, dtype=blocks.dtype)
    return (eye[:, None, :, None] * blocks[:, :, None, :]).reshape(g * n, g * m)


def _dft_tables(n):
    def direct(rows, cols, period):
        ang = ((rows[:, None] * cols[None, :]) % period).astype(F32) * (2.0 * math.pi / period)
        return jnp.cos(ang), jnp.sin(ang)

    idx = jnp.arange(n, dtype=jnp.int32)
    if n <= 1024:
        return direct(idx, idx, n)
    lo = 64
    ca, sa = direct(jnp.arange(n // lo, dtype=jnp.int32), idx, n // lo)
    cb, sb = direct(jnp.arange(lo, dtype=jnp.int32), idx, n)
    ca, sa, cb, sb = ca[:, None, :], sa[:, None, :], cb[None], sb[None]
    return (ca * cb - sa * sb).reshape(n, n), (sa * cb + ca * sb).reshape(n, n)


def _from_scan(yf, yb, b):
    t = yf.shape[0]
    lane = lax.broadcasted_iota(jnp.int32, yf.shape, 2)
    y = jnp.where((lane & N_HEADS) == 0, yf, yb)
    return y.reshape(t, HEAD, b, 2, N_HEADS).transpose(2, 0, 3, 4, 1).reshape(b, t, 2 * GROUP_W)


def kernel(x, c, ctx, c_ctx, w_mod, b_mod, w_in, rkv_conv, decay_w0, decay_w2, iclr_a0, iclr_a2,
           gate_g2, k_k, k_a, r_k, gn_g, gn_b, pool_w, pool_scale, gmlp_ln_g, gmlp_ln_b, gmlp_ws,
           gmlp_bs, fnet_w, fnet_b, w_out, ln1_g, ln1_b, ln2_g, ln2_b, ffn_w1, ffn_w2):
    bsz, seq, d = x.shape
    ctx_len = ctx.shape[1]
    depth = w_in.shape[0]
    d_ff = ffn_w2.shape[1]
    alpha = (2 * depth) ** 0.25
    rwkv_cols = 3 * GROUP_W + LORA_W
    assert bsz * N_HEADS * 2 == V7X_LANES, "scan layout packs (direction, batch, head) on the 128 lanes"

    pos = _pos_embed(seq, d)
    ones_bd = _block_diag(jnp.ones((N_HEADS, HEAD, HEAD), F32)).astype(BF16)
    cc64, ss64 = _dft_tables(HEAD)
    cc_bd = _block_diag(jnp.tile(cc64[None], (4, 1, 1))).astype(BF16)
    sc_bd = _block_diag(jnp.tile(ss64[None], (4, 1, 1))).astype(BF16)

    def dft_rows(n):
        cn, sn = _dft_tables(n)
        return jnp.concatenate([cn, -sn], axis=1).astype(BF16)

    cs_x = dft_rows(seq)
    cs_c = dft_rows(ctx_len)

    pad_rows = (-(bsz + 1)) % V7X_SUBLANES
    c_all = jnp.concatenate([c, c_ctx[None], jnp.zeros((pad_rows, d), F32)], axis=0)

    tm_x = 512
    tm_c = ctx_len
    tb = 32

    xs, cs = x, ctx
    for l in range(depth):
        last = l == depth - 1
        first = l == 0
        m = _modulation(c_all, w_mod[l].astype(BF16), b_mod[l])
        mod_x = m[:bsz].reshape(bsz, 6, d)
        mod_c = jnp.broadcast_to(m[bsz].reshape(1, 6, d), (bsz, 6, d))

        w_in_l = w_in[l].astype(BF16)
        widths_full = (3 * GROUP_W, LORA_W, GROUP_W, 2 * GROUP_W, GROUP_W)
        px = _in_proj(xs, pos if first else None, mod_x, w_in_l, widths_full, tm_x)
        if last:
            pc = _in_proj(cs, None, mod_c, w_in_l[:, :rwkv_cols], widths_full[:2], tm_c)
        else:
            pc = _in_proj(cs, None, mod_c, w_in_l, widths_full, tm_c)

        wl = jnp.zeros((LORA_W, 5 * GROUP_W), F32)
        wl = wl.at[0:64, 0:256].set(decay_w2[l, 0]).at[64:128, 256:512].set(decay_w2[l, 1])
        wl = wl.at[128:192, 512:768].set(iclr_a2[l, 0]).at[192:256, 768:1024].set(iclr_a2[l, 1])
        wl = wl.at[256:384, 1024:1280].set(gate_g2[l]).astype(BF16)
        prep_args = (rkv_conv[l], wl, decay_w0[l], iclr_a0[l], k_k[l].reshape(1, -1), k_a[l].reshape(1, -1),
                     ones_bd)
        p_r, p_v, p_kk, p_gate, p_w2, p_b2, p_kd2 = _rwkv_prep(px[0], px[1], pc[0], pc[1], *prep_args)
        yf_s, yb_s = _rwkv_scan(*(_to_scan_layout(a) for a in (p_r, p_v, p_kk, p_w2, p_b2, p_kd2)), tb, seq, ctx_len)
        y2 = _from_scan(yf_s, yb_s, bsz)

        pool_bd = _block_diag(pool_w[l]).astype(BF16)
        fnet_bd = _block_diag(fnet_w[l]).astype(BF16)
        ws_b = gmlp_ws[l].astype(BF16)
        bias_tile = jnp.repeat(gmlp_bs[l].T, HEAD, axis=1)
        row = lambda a: a.reshape(1, -1)

        def mixers(parts, row0, n_rows, cs_tab, tm, tmf):
            a = _rwkv_readout(y2, p_r, p_v, p_kd2, p_gate, row(r_k[l]), row(gn_g[l]), row(gn_b[l]), ones_bd,
                              tm, row0, n_rows)
            p = _pool_mixer(parts[2], pool_bd, row(pool_scale[l]))
            g = _gmlp_mixer(parts[3], row(gmlp_ln_g[l]), row(gmlp_ln_b[l]), ws_b, bias_tile, ones_bd, tm)
            f = _fourier_mixer(parts[4], cs_tab, cc_bd, sc_bd, fnet_bd, row(fnet_b[l]), tmf)
            return a, p, g, f

        w_out_l = w_out[l].astype(BF16)
        w1_l = ffn_w1[l].astype(BF16)
        w2_l = ffn_w2[l].astype(BF16)
        parts_x = mixers(px, 0, seq, cs_x, tm_x, 256)
        xs = _out_proj(parts_x, xs, pos if first else None, mod_x, w_out_l, row(ln1_g[l]), row(ln1_b[l]), alpha, tm_x)
        xs = _ffn(xs, mod_x, w1_l, w2_l, row(ln2_g[l]), row(ln2_b[l]), alpha, tm_x)
        if not last:
            parts_c = mixers(pc, seq, ctx_len, cs_c, tm_c, tm_c)
            cs = _out_proj(parts_c, cs, None, mod_c, w_out_l, row(ln1_g[l]), row(ln1_b[l]), alpha, tm_c)
            cs = _ffn(cs, mod_c, w1_l, w2_l, row(ln2_g[l]), row(ln2_b[l]), alpha, tm_c)
    return xs
```

```python
import functools
import math

import jax
import jax.numpy as jnp
from jax import lax
from jax.experimental import pallas as pl
from jax.experimental.pallas import tpu as pltpu

F32 = jnp.float32
BF16 = jnp.bfloat16

V7X_LANES = 128
V7X_SUBLANES = 8
V7X_VMEM_BYTES = 64 * 1024 * 1024
V7X_VMEM_LIMIT_CAP = 56 * 1024 * 1024

GRID_W = 64
HEAD = 64
GROUP_W = 256
N_HEADS = GROUP_W // HEAD
LORA_W = 384
POOL_WINDOWS = (2, 4, 8, 16)
CHUNK = 128
LN_EPS = 1e-5
GN_EPS = 64e-5


def _cparams(sem, vmem_bytes):
    limit = int(min(max(vmem_bytes, 16 * 1024 * 1024), V7X_VMEM_LIMIT_CAP))
    return pltpu.CompilerParams(dimension_semantics=sem, vmem_limit_bytes=limit)


def _silu(x):
    return x * jax.nn.sigmoid(x)


def _seg_sum(x, ones_bd):
    hi = x.astype(BF16)
    lo = (x - hi.astype(F32)).astype(BF16)
    return (jnp.dot(hi, ones_bd, preferred_element_type=F32)
            + jnp.dot(lo, ones_bd, preferred_element_type=F32))


def _layer_norm(z, g, b):
    mu = jnp.mean(z, axis=-1, keepdims=True)
    zc = z - mu
    var = jnp.mean(zc * zc, axis=-1, keepdims=True)
    return zc * lax.rsqrt(var + LN_EPS) * g + b


def _mod_kernel(c_ref, w_ref, b_ref, o_ref):
    s = _silu(c_ref[...])
    o_ref[...] = jnp.dot(s.astype(BF16), w_ref[...], preferred_element_type=F32) + b_ref[...]


def _modulation(c_all, w_mod, b_mod):
    rows, d = c_all.shape
    n = w_mod.shape[1]
    tn = 1024
    return pl.pallas_call(
        _mod_kernel,
        grid=(n // tn,),
        in_specs=[pl.BlockSpec((rows, d), lambda j: (0, 0)),
                  pl.BlockSpec((d, tn), lambda j: (0, j)),
                  pl.BlockSpec((1, tn), lambda j: (0, j))],
        out_specs=pl.BlockSpec((rows, tn), lambda j: (0, j)),
        out_shape=jax.ShapeDtypeStruct((rows, n), F32),
        compiler_params=_cparams(("parallel",), 4 * d * tn * 2),
        name="modulation",
    )(c_all, w_mod, b_mod.reshape(1, n))


def _inproj_kernel(*refs, widths, add_pos):
    if add_pos:
        x_ref, pos_ref, mod_ref, w_ref = refs[:4]
        outs = refs[4:]
        x = x_ref[0] + pos_ref[...]
    else:
        x_ref, mod_ref, w_ref = refs[:3]
        outs = refs[3:]
        x = x_ref[0]
    sh = mod_ref[0, 0:1, :]
    sc = mod_ref[0, 1:2, :]
    h = (x * (1.0 + sc) + sh).astype(BF16)
    off = 0
    for o_ref, wd in zip(outs, widths):
        o_ref[0] = jnp.dot(h, w_ref[:, off:off + wd], preferred_element_type=F32)
        off += wd


def _in_proj(x, pos, mod, w, widths, tm):
    b, l, d = x.shape
    n = w.shape[1]
    assert sum(widths) == n
    add_pos = pos is not None
    in_specs = [pl.BlockSpec((1, tm, d), lambda bi, i: (bi, i, 0))]
    args = [x]
    if add_pos:
        in_specs.append(pl.BlockSpec((tm, d), lambda bi, i: (i, 0)))
        args.append(pos)
    in_specs += [pl.BlockSpec((1, 6, d), lambda bi, i: (bi, 0, 0)),
                 pl.BlockSpec((d, n), lambda bi, i: (0, 0))]
    args += [mod, w]
    vmem = 2 * (2 * tm * d * 4 + d * n * 2 + tm * n * 4) + tm * d * 8
    return pl.pallas_call(
        functools.partial(_inproj_kernel, widths=widths, add_pos=add_pos),
        grid=(b, l // tm),
        in_specs=in_specs,
        out_specs=[pl.BlockSpec((1, tm, wd), lambda bi, i: (bi, i, 0)) for wd in widths],
        out_shape=[jax.ShapeDtypeStruct((b, l, wd), F32) for wd in widths],
        compiler_params=_cparams(("parallel", "parallel"), vmem),
        name="in_proj",
    )(*args)


def _prep_kernel(zx_ref, zp_ref, zn_ref, zc_ref, lox_ref, loc_ref, conv_ref, wl_ref, w0_ref, a0_ref, kk_ref,
                 ka_ref, ones_ref, r_o, v_o, kk_o, gate_o, w_o, b_o, kd_o, *, tm, nx):
    i = pl.program_id(1)
    is_ctx = i >= nx
    z = jnp.where(is_ctx, zc_ref[0], zx_ref[0])
    lo = jnp.where(is_ctx, loc_ref[0], lox_ref[0])
    row = lax.broadcasted_iota(jnp.int32, z.shape, 0)
    has_prev = jnp.logical_and(i > 0, i < nx).astype(F32)
    has_next = (i < nx - 1).astype(F32)
    prev_row = zp_ref[0, V7X_SUBLANES - 1:V7X_SUBLANES, :] * has_prev
    next_row = zn_ref[0, 0:1, :] * has_next
    zm1 = jnp.where(row == 0, prev_row, pltpu.roll(z, 1, 0))
    zp1 = jnp.where(row == tm - 1, next_row, pltpu.roll(z, tm - 1, 0))
    c = zm1 * conv_ref[0:1, :] + z * conv_ref[1:2, :] + zp1 * conv_ref[2:3, :]
    r = c[:, 0:GROUP_W]
    k = c[:, GROUP_W:2 * GROUP_W]
    v = c[:, 2 * GROUP_W:3 * GROUP_W]

    col = lax.broadcasted_iota(jnp.int32, lo.shape, 1)
    act = jnp.where(col < 128, jnp.tanh(lo), jnp.where(col < 256, lo, jax.nn.sigmoid(lo)))
    pre = jnp.dot(act.astype(BF16), wl_ref[...], preferred_element_type=F32)

    ones_bd = ones_ref[...]
    kk = k * kk_ref[...]
    nrm = jnp.sqrt(_seg_sum(kk * kk, ones_bd))
    kk = kk / jnp.maximum(nrm, 1e-12)
    gate_o[0] = pre[:, 4 * GROUP_W:5 * GROUP_W]
    ka = ka_ref[...]
    for d in range(2):
        cols = slice(d * GROUP_W, (d + 1) * GROUP_W)
        r_o[0, :, cols] = r
        v_o[0, :, cols] = v
        kk_o[0, :, cols] = kk
        xw = w0_ref[d:d + 1, :] + pre[:, cols]
        w_o[0, :, cols] = jnp.exp(-math.exp(-0.5) * jax.nn.sigmoid(xw))
        a = jax.nn.sigmoid(a0_ref[d:d + 1, :] + pre[:, (2 + d) * GROUP_W:(3 + d) * GROUP_W])
        kd_o[0, :, cols] = k * (1.0 + (a - 1.0) * ka)
        b_o[0, :, cols] = kk * a


def _rwkv_prep(zx_rkv, zx_lora, zc_rkv, zc_lora, conv, wl, w0, a0, k_k, k_a, ones_bd):
    b, l, c3 = zx_rkv.shape
    tm = zc_rkv.shape[1]
    nx = l // tm
    nblk8 = l // V7X_SUBLANES
    r8 = tm // V7X_SUBLANES
    full = lambda shape: pl.BlockSpec(shape, lambda bi, i: (0,) * len(shape))
    xi = lambda i: jnp.minimum(i, nx - 1)
    in_specs = [
        pl.BlockSpec((1, tm, c3), lambda bi, i: (bi, xi(i), 0)),
        pl.BlockSpec((1, V7X_SUBLANES, c3), lambda bi, i: (bi, jnp.clip(i * r8 - 1, 0, nblk8 - 1), 0)),
        pl.BlockSpec((1, V7X_SUBLANES, c3), lambda bi, i: (bi, jnp.minimum((i + 1) * r8, nblk8 - 1), 0)),
        pl.BlockSpec((1, tm, c3), lambda bi, i: (bi, 0, 0)),
        pl.BlockSpec((1, tm, LORA_W), lambda bi, i: (bi, xi(i), 0)),
        pl.BlockSpec((1, tm, LORA_W), lambda bi, i: (bi, 0, 0)),
        full(conv.shape), full(wl.shape), full(w0.shape), full(a0.shape),
        full(k_k.shape), full(k_a.shape), full(ones_bd.shape),
    ]
    spec1 = pl.BlockSpec((1, tm, GROUP_W), lambda bi, i: (bi, i, 0))
    spec2 = pl.BlockSpec((1, tm, 2 * GROUP_W), lambda bi, i: (bi, i, 0))
    shp1 = jax.ShapeDtypeStruct((b, l + tm, GROUP_W), F32)
    shp2 = jax.ShapeDtypeStruct((b, l + tm, 2 * GROUP_W), F32)
    vmem = 2 * (2 * tm * (c3 + LORA_W) * 4 + 13 * tm * GROUP_W * 4) + 14 * tm * c3 * 4
    return pl.pallas_call(
        functools.partial(_prep_kernel, tm=tm, nx=nx),
        grid=(b, nx + 1),
        in_specs=in_specs,
        out_specs=[spec2] * 3 + [spec1] + [spec2] * 3,
        out_shape=[shp2] * 3 + [shp1] + [shp2] * 3,
        compiler_params=_cparams(("parallel", "arbitrary"), vmem),
        name="rwkv_prep",
    )(zx_rkv, zx_rkv, zx_rkv, zc_rkv, zx_lora, zc_lora, conv, wl, w0, a0, k_k, k_a, ones_bd)


SCAN_ROW_PITCH = 72


def _to_scan_kernel(xf_ref, xb_ref, o_ref, y_ref, *, nb, tt):
    rev = (tt - 1) - lax.broadcasted_iota(jnp.int32, (GROUP_W, tt), 1)
    for b in range(nb):
        xf = xf_ref[b].T
        xb = jnp.take_along_axis(xb_ref[b].T, rev, axis=1)
        for h in range(N_HEADS):
            y_ref[pl.ds((b * 2 * N_HEADS + h) * SCAN_ROW_PITCH, HEAD), :] = xf[h * HEAD:(h + 1) * HEAD, :]
            y_ref[pl.ds((b * 2 * N_HEADS + N_HEADS + h) * SCAN_ROW_PITCH, HEAD), :] = xb[h * HEAD:(h + 1) * HEAD, :]
    for k in range(HEAD):
        z = y_ref[pl.ds(k, nb * 2 * N_HEADS, stride=SCAN_ROW_PITCH), :]
        o_ref[k] = z.T


def _to_scan_layout(x, n_lat, n_ctx):
    nb, t, c = x.shape
    tt = V7X_LANES
    assert nb * 2 * N_HEADS == V7X_LANES and c == 2 * GROUP_W and n_lat % tt == 0 and n_ctx % tt == 0
    nl, nc = n_lat // tt, n_ctx // tt
    n = nl + nc
    return pl.pallas_call(
        functools.partial(_to_scan_kernel, nb=nb, tt=tt),
        grid=(n,),
        in_specs=[pl.BlockSpec((nb, tt, GROUP_W), lambda j: (0, jnp.where(j < nc, nl + j, j - nc), 0)),
                  pl.BlockSpec((nb, tt, GROUP_W), lambda j: (0, n - 1 - j, 1))],
        out_specs=pl.BlockSpec((HEAD, tt, V7X_LANES), lambda j: (0, j, 0)),
        out_shape=jax.ShapeDtypeStruct((HEAD, t, V7X_LANES), F32),
        scratch_shapes=[pltpu.VMEM((V7X_LANES * SCAN_ROW_PITCH, tt), F32)],
        compiler_params=_cparams(("parallel",), 2 * (nb * tt * c * 4 + HEAD * tt * V7X_LANES * 4)
                                 + 3 * V7X_LANES * SCAN_ROW_PITCH * tt * 4),
        name="to_scan_layout",
    )(x, x)


def _scan_kernel(r_ref, v_ref, kk_ref, w_ref, b_ref, kd_ref, y_ref, h_ref, vt_ref, *, tb):
    @pl.when(pl.program_id(0) == 0)
    def _():
        h_ref[...] = jnp.zeros_like(h_ref)

    u0 = jnp.zeros((HEAD, V7X_LANES), F32)
    for k in range(HEAD):
        u0 = u0 + kk_ref[k, 0:1, :] * h_ref[k]

    def step(t, u):
        tn = jnp.minimum(t + 1, tb - 1)
        for v in range(HEAD):
            vt_ref[v:v + 1, :] = v_ref[v, pl.ds(t, 1), :]
        vv = vt_ref[...]
        y = jnp.zeros((HEAD, V7X_LANES), F32)
        un = jnp.zeros((HEAD, V7X_LANES), F32)
        for k in range(HEAD):
            hn = (h_ref[k] * w_ref[k, pl.ds(t, 1), :] - b_ref[k, pl.ds(t, 1), :] * u
                  + kd_ref[k, pl.ds(t, 1), :] * vv)
            h_ref[k] = hn
            y = y + r_ref[k, pl.ds(t, 1), :] * hn
            un = un + kk_ref[k, pl.ds(tn, 1), :] * hn
        y_ref[t] = y
        return un

    lax.fori_loop(0, tb, step, u0)


def _rwkv_scan(r, v, kk, w, b, kd, tb):
    t_tot = r.shape[1]
    ispec = pl.BlockSpec((HEAD, tb, V7X_LANES), lambda i: (0, i, 0))
    blk = tb * HEAD * V7X_LANES * 4
    return pl.pallas_call(
        functools.partial(_scan_kernel, tb=tb),
        grid=(t_tot // tb,),
        in_specs=[ispec] * 6,
        out_specs=pl.BlockSpec((tb, HEAD, V7X_LANES), lambda i: (i, 0, 0)),
        out_shape=jax.ShapeDtypeStruct((t_tot, HEAD, V7X_LANES), F32),
        scratch_shapes=[pltpu.VMEM((HEAD, HEAD, V7X_LANES), F32), pltpu.VMEM((HEAD, V7X_LANES), F32)],
        compiler_params=_cparams(("arbitrary",), 14 * blk + 3 * HEAD * HEAD * V7X_LANES * 4),
        name="rwkv_scan",
    )(r, v, kk, w, b, kd)


def _readout_kernel(y_ref, r_ref, v_ref, kd_ref, gate_ref, rk_ref, g_ref, b_ref, ones_ref, o_ref):
    ones_bd = ones_ref[...]
    inv = 1.0 / HEAD
    y = y_ref[0, :, 0:GROUP_W] + y_ref[0, :, GROUP_W:2 * GROUP_W]
    kd = kd_ref[0, :, 0:GROUP_W] + kd_ref[0, :, GROUP_W:2 * GROUP_W]
    mu = _seg_sum(y, ones_bd) * inv
    yc = y - mu
    var = _seg_sum(yc * yc, ones_bd) * inv
    yn = yc * lax.rsqrt(var + GN_EPS)
    bonus = _seg_sum(r_ref[0] * kd * rk_ref[...], ones_bd) * v_ref[0]
    o_ref[0] = (yn * g_ref[...] + b_ref[...] + bonus) * gate_ref[0]


def _rwkv_readout(y2, r, v, kd2, gate, r_k, gn_g, gn_b, ones_bd, tm, row0, n_rows):
    b = y2.shape[0]
    off = row0 // tm
    spec1 = pl.BlockSpec((1, tm, GROUP_W), lambda bi, i: (bi, i + off, 0))
    spec2 = pl.BlockSpec((1, tm, 2 * GROUP_W), lambda bi, i: (bi, i + off, 0))
    full = lambda shape: pl.BlockSpec(shape, lambda bi, i: (0,) * len(shape))
    return pl.pallas_call(
        _readout_kernel,
        grid=(b, n_rows // tm),
        in_specs=[spec2, spec1, spec1, spec2, spec1,
                  full(r_k.shape), full(gn_g.shape), full(gn_b.shape), full(ones_bd.shape)],
        out_specs=pl.BlockSpec((1, tm, GROUP_W), lambda bi, i: (bi, i, 0)),
        out_shape=jax.ShapeDtypeStruct((b, n_rows, GROUP_W), F32),
        compiler_params=_cparams(("parallel", "parallel"), 36 * tm * GROUP_W * 4),
        name="rwkv_readout",
    )(y2, r, v, kd2, gate, r_k, gn_g, gn_b, ones_bd)


def _pool_kernel(z_ref, w_ref, s_ref, o_ref, *, l):
    z = z_ref[0]
    row = lax.broadcasted_iota(jnp.int32, z.shape, 0)
    lane = lax.broadcasted_iota(jnp.int32, z.shape, 1)

    def shifted(x, o):
        y = pltpu.roll(x, (-o) % l, 0)
        return jnp.where((row + o >= 0) & (row + o < l), y, 0.0)

    def count(wd):
        lo = jnp.maximum(row - wd // 2, 0)
        hi = jnp.minimum(row + wd - wd // 2, l)
        return (hi - lo).astype(F32)

    s2 = shifted(z, -1) + z
    pooled = s2 / count(2)
    s4 = shifted(s2, -1) + shifted(s2, 1)
    pooled = jnp.where(lane >= 64, s4 / count(4), pooled)
    s8 = shifted(s4, -2) + shifted(s4, 2)
    pooled = jnp.where(lane >= 128, s8 / count(8), pooled)
    s16 = shifted(s8, -4) + shifted(s8, 4)
    pooled = jnp.where(lane >= 192, s16 / count(16), pooled)
    d = pooled - z
    o_ref[0] = jnp.dot(d.astype(BF16), w_ref[...], preferred_element_type=F32) * s_ref[...]


def _pool_mixer(z, w_bd, scale):
    b, l, c = z.shape
    spec = pl.BlockSpec((1, l, c), lambda bi: (bi, 0, 0))
    return pl.pallas_call(
        functools.partial(_pool_kernel, l=l),
        grid=(b,),
        in_specs=[spec, pl.BlockSpec(w_bd.shape, lambda bi: (0, 0)), pl.BlockSpec(scale.shape, lambda bi: (0, 0))],
        out_specs=spec,
        out_shape=jax.ShapeDtypeStruct((b, l, c), F32),
        compiler_params=_cparams(("parallel",), 20 * l * c * 4),
        name="pool_mixer",
    )(z, w_bd, scale)


def _gelu_tanh(x):
    return 0.5 * x * (1.0 + jnp.tanh(math.sqrt(2.0 / math.pi) * (x + 0.044715 * (x * x * x))))


def _gmlp_kernel(z_ref, g_ref, b_ref, ws_ref, bias_ref, ones_ref, o_ref, *, tm):
    ones_bd = ones_ref[...]
    inv = 1.0 / HEAD
    z = z_ref[0]
    u = _gelu_tanh(z[:, 0:GROUP_W])
    v = _gelu_tanh(z[:, GROUP_W:2 * GROUP_W])
    mu = _seg_sum(v, ones_bd) * inv
    vc = v - mu
    var = _seg_sum(vc * vc, ones_bd) * inv
    vn = (vc * lax.rsqrt(var + LN_EPS) * g_ref[...] + b_ref[...]).astype(BF16)
    lane = lax.broadcasted_iota(jnp.int32, (CHUNK, GROUP_W), 1)
    for c in range(tm // CHUNK):
        vchunk = vn[c * CHUNK:(c + 1) * CHUNK, :]
        sv = bias_ref[...]
        for g in range(4):
            part = jnp.dot(ws_ref[g], vchunk, preferred_element_type=F32)
            sv = sv + jnp.where((lane >= g * HEAD) & (lane < (g + 1) * HEAD), part, 0.0)
        o_ref[0, c * CHUNK:(c + 1) * CHUNK, :] = u[c * CHUNK:(c + 1) * CHUNK, :] * sv


def _gmlp_mixer(z, ln_g, ln_b, ws, bias_tile, ones_bd, tm):
    b, l, c2 = z.shape
    full = lambda shape: pl.BlockSpec(shape, lambda bi, i: (0,) * len(shape))
    return pl.pallas_call(
        functools.partial(_gmlp_kernel, tm=tm),
        grid=(b, l // tm),
        in_specs=[pl.BlockSpec((1, tm, c2), lambda bi, i: (bi, i, 0)), full(ln_g.shape), full(ln_b.shape),
                  full(ws.shape), full(bias_tile.shape), full(ones_bd.shape)],
        out_specs=pl.BlockSpec((1, tm, GROUP_W), lambda bi, i: (bi, i, 0)),
        out_shape=jax.ShapeDtypeStruct((b, l, GROUP_W), F32),
        compiler_params=_cparams(("parallel", "parallel"), 24 * tm * c2 * 4),
        name="gmlp_mixer",
    )(z, ln_g, ln_b, ws, bias_tile, ones_bd)


def _fnet_kernel(z_ref, cs_ref, cc_ref, sc_ref, w_ref, b_ref, o_ref, zcs_ref, *, l, scale):
    @pl.when(pl.program_id(1) == 0)
    def _():
        zb = z_ref[0].astype(BF16)
        zcs_ref[0:l, :] = jnp.dot(zb, cc_ref[...], preferred_element_type=F32).astype(BF16)
        zcs_ref[l:2 * l, :] = jnp.dot(zb, sc_ref[...], preferred_element_type=F32).astype(BF16)

    f = jnp.dot(cs_ref[...], zcs_ref[...], preferred_element_type=F32) * scale
    o_ref[0] = jnp.dot(f.astype(BF16), w_ref[...], preferred_element_type=F32) + b_ref[...]


def _fourier_mixer(z, cs, cc_bd, sc_bd, w_bd, bias, tm):
    b, l, c = z.shape
    full = lambda shape: pl.BlockSpec(shape, lambda bi, i: (0,) * len(shape))
    scale = 1.0 / math.sqrt(l * HEAD)
    vmem = 2 * (l * c * 4 + tm * 2 * l * 2 + tm * c * 4) + 2 * l * c * 2 + 6 * l * c * 4
    return pl.pallas_call(
        functools.partial(_fnet_kernel, l=l, scale=scale),
        grid=(b, l // tm),
        in_specs=[pl.BlockSpec((1, l, c), lambda bi, i: (bi, 0, 0)),
                  pl.BlockSpec((tm, 2 * l), lambda bi, i: (i, 0)),
                  full(cc_bd.shape), full(sc_bd.shape), full(w_bd.shape), full(bias.shape)],
        out_specs=pl.BlockSpec((1, tm, c), lambda bi, i: (bi, i, 0)),
        out_shape=jax.ShapeDtypeStruct((b, l, c), F32),
        scratch_shapes=[pltpu.VMEM((2 * l, c), BF16)],
        compiler_params=_cparams(("parallel", "arbitrary"), vmem),
        name="fourier_mixer",
    )(z, cs, cc_bd, sc_bd, w_bd, bias)


def _outproj_kernel(*refs, alpha, add_pos):
    if add_pos:
        a_ref, p_ref, g_ref, f_ref, x_ref, pos_ref, mod_ref, w_ref, lg_ref, lb_ref, o_ref = refs
        x = x_ref[0] + pos_ref[...]
    else:
        a_ref, p_ref, g_ref, f_ref, x_ref, mod_ref, w_ref, lg_ref, lb_ref, o_ref = refs
        x = x_ref[0]
    mix = jnp.dot(a_ref[0].astype(BF16), w_ref[0:GROUP_W, :], preferred_element_type=F32)
    for j, part in enumerate((p_ref, g_ref, f_ref), start=1):
        mix = mix + jnp.dot(part[0].astype(BF16), w_ref[j * GROUP_W:(j + 1) * GROUP_W, :],
                            preferred_element_type=F32)
    gate1 = mod_ref[0, 2:3, :]
    o_ref[0] = _layer_norm(alpha * x + gate1 * mix, lg_ref[...], lb_ref[...])


def _out_proj(parts, x, pos, mod, w_out, ln_g, ln_b, alpha, tm):
    b, l, d = x.shape
    add_pos = pos is not None
    pspec = pl.BlockSpec((1, tm, GROUP_W), lambda bi, i: (bi, i, 0))
    xspec = pl.BlockSpec((1, tm, d), lambda bi, i: (bi, i, 0))
    full = lambda shape: pl.BlockSpec(shape, lambda bi, i: (0,) * len(shape))
    in_specs = [pspec] * 4 + [xspec]
    args = list(parts) + [x]
    if add_pos:
        in_specs.append(pl.BlockSpec((tm, d), lambda bi, i: (i, 0)))
        args.append(pos)
    in_specs += [pl.BlockSpec((1, 6, d), lambda bi, i: (bi, 0, 0)), full(w_out.shape), full(ln_g.shape),
                 full(ln_b.shape)]
    args += [mod, w_out, ln_g, ln_b]
    vmem = 2 * (4 * tm * GROUP_W * 4 + 3 * tm * d * 4 + d * d * 2) + 6 * tm * d * 4
    return pl.pallas_call(
        functools.partial(_outproj_kernel, alpha=alpha, add_pos=add_pos),
        grid=(b, l // tm),
        in_specs=in_specs,
        out_specs=xspec,
        out_shape=jax.ShapeDtypeStruct((b, l, d), F32),
        compiler_params=_cparams(("parallel", "parallel"), vmem),
        name="out_proj_ln",
    )(*args)


def _ffn_kernel(x_ref, mod_ref, wg_ref, wu_ref, w2_ref, lg_ref, lb_ref, o_ref, *, alpha, f_chunks):
    x = x_ref[0]
    sh = mod_ref[0, 3:4, :]
    sc = mod_ref[0, 4:5, :]
    h = (x * (1.0 + sc) + sh).astype(BF16)
    acc = None
    for lo, hi in f_chunks:
        gate = jnp.dot(h, wg_ref[:, lo:hi], preferred_element_type=F32)
        up = jnp.dot(h, wu_ref[:, lo:hi], preferred_element_type=F32)
        act = (_silu(gate) * up).astype(BF16)
        part = jnp.dot(act, w2_ref[lo:hi, :], preferred_element_type=F32)
        acc = part if acc is None else acc + part
    gate2 = mod_ref[0, 5:6, :]
    o_ref[0] = _layer_norm(alpha * x + gate2 * acc, lg_ref[...], lb_ref[...])


def _ffn(x, mod, w1, w2, ln_g, ln_b, alpha, tm):
    b, l, d = x.shape
    f = w2.shape[0]
    mxu_n = 256
    half = (f // mxu_n + 1) // 2 * mxu_n
    f_chunks = ((0, half), (half, f))
    xspec = pl.BlockSpec((1, tm, d), lambda bi, i: (bi, i, 0))
    full = lambda shape: pl.BlockSpec(shape, lambda bi, i: (0,) * len(shape))
    once = pl.Buffered(1)
    vmem = 4 * tm * d * 4 + 3 * d * f * 2 + 3 * tm * d * 4 + 4 * tm * half * 4
    return pl.pallas_call(
        functools.partial(_ffn_kernel, alpha=alpha, f_chunks=f_chunks),
        grid=(b, l // tm),
        in_specs=[xspec,
                  pl.BlockSpec((1, 6, d), lambda bi, i: (bi, 0, 0)),
                  pl.BlockSpec((d, f), lambda bi, i: (0, 0), pipeline_mode=once),
                  pl.BlockSpec((d, f), lambda bi, i: (0, 1), pipeline_mode=once),
                  pl.BlockSpec((f, d), lambda bi, i: (0, 0), pipeline_mode=once),
                  full(ln_g.shape), full(ln_b.shape)],
        out_specs=xspec,
        out_shape=jax.ShapeDtypeStruct((b, l, d), F32),
        compiler_params=_cparams(("parallel", "parallel"), vmem),
        name="ffn_ln",
    )(x, mod, w1, w1, w2, ln_g, ln_b)


def _pos_embed(n_tok, dim):
    rows = n_tok // GRID_W
    row, col = jnp.meshgrid(jnp.arange(rows, dtype=F32), jnp.arange(GRID_W, dtype=F32), indexing='ij')
    quarter = dim // 4
    freqs = jnp.exp(-math.log(10000.0) * jnp.arange(quarter, dtype=F32) / quarter)

    def enc(p):
        ang = p.reshape(-1, 1) * freqs[None, :]
        return jnp.concatenate([jnp.sin(ang), jnp.cos(ang)], -1)

    return jnp.concatenate([enc(row), enc(col)], -1)


def _block_diag(blocks):
    g, n, m = blocks.shape
    eye = jnp.eye(g, dtype=blocks.dtype)
    return (eye[:, None, :, None] * blocks[:, :, None, :]).reshape(g * n, g * m)


def _dft_tables(n):
    def direct(rows, cols, period):
        ang = ((rows[:, None] * cols[None, :]) % period).astype(F32) * (2.0 * math.pi / period)
        return jnp.cos(ang), jnp.sin(ang)

    idx = jnp.arange(n, dtype=jnp.int32)
    if n <= 1024:
        return direct(idx, idx, n)
    lo = 64
    ca, sa = direct(jnp.arange(n // lo, dtype=jnp.int32), idx, n // lo)
    cb, sb = direct(jnp.arange(lo, dtype=jnp.int32), idx, n)
    ca, sa, cb, sb = ca[:, None, :], sa[:, None, :], cb[None], sb[None]
    return (ca * cb - sa * sb).reshape(n, n), (sa * cb + ca * sb).reshape(n, n)


def _from_scan(y, b, n_ctx):
    t = y.shape[0]
    yf = jnp.concatenate([y[n_ctx:], y[:n_ctx]], axis=0)
    yb = jnp.concatenate([y[n_ctx:][::-1], y[:n_ctx][::-1]], axis=0)
    lane = lax.broadcasted_iota(jnp.int32, y.shape, 2)
    yn = jnp.where((lane & N_HEADS) == 0, yf, yb)
    return yn.reshape(t, HEAD, b, 2, N_HEADS).transpose(2, 0, 3, 4, 1).reshape(b, t, 2 * GROUP_W)


def kernel(x, c, ctx, c_ctx, w_mod, b_mod, w_in, rkv_conv, decay_w0, decay_w2, iclr_a0, iclr_a2,
           gate_g2, k_k, k_a, r_k, gn_g, gn_b, pool_w, pool_scale, gmlp_ln_g, gmlp_ln_b, gmlp_ws,
           gmlp_bs, fnet_w, fnet_b, w_out, ln1_g, ln1_b, ln2_g, ln2_b, ffn_w1, ffn_w2):
    bsz, seq, d = x.shape
    ctx_len = ctx.shape[1]
    depth = w_in.shape[0]
    d_ff = ffn_w2.shape[1]
    alpha = (2 * depth) ** 0.25
    rwkv_cols = 3 * GROUP_W + LORA_W
    assert bsz * N_HEADS * 2 == V7X_LANES, "scan layout packs (direction, batch, head) on the 128 lanes"

    pos = _pos_embed(seq, d)
    ones_bd = _block_diag(jnp.ones((N_HEADS, HEAD, HEAD), F32)).astype(BF16)
    cc64, ss64 = _dft_tables(HEAD)
    cc_bd = _block_diag(jnp.tile(cc64[None], (4, 1, 1))).astype(BF16)
    sc_bd = _block_diag(jnp.tile(ss64[None], (4, 1, 1))).astype(BF16)

    def dft_rows(n):
        cn, sn = _dft_tables(n)
        return jnp.concatenate([cn, -sn], axis=1).astype(BF16)

    cs_x = dft_rows(seq)
    cs_c = dft_rows(ctx_len)

    pad_rows = (-(bsz + 1)) % V7X_SUBLANES
    c_all = jnp.concatenate([c, c_ctx[None], jnp.zeros((pad_rows, d), F32)], axis=0)

    tm_x = 512
    tm_c = ctx_len
    tb = 32

    xs, cs = x, ctx
    for l in range(depth):
        last = l == depth - 1
        first = l == 0
        m = _modulation(c_all, w_mod[l].astype(BF16), b_mod[l])
        mod_x = m[:bsz].reshape(bsz, 6, d)
        mod_c = jnp.broadcast_to(m[bsz].reshape(1, 6, d), (bsz, 6, d))

        w_in_l = w_in[l].astype(BF16)
        widths_full = (3 * GROUP_W, LORA_W, GROUP_W, 2 * GROUP_W, GROUP_W)
        px = _in_proj(xs, pos if first else None, mod_x, w_in_l, widths_full, tm_x)
        if last:
            pc = _in_proj(cs, None, mod_c, w_in_l[:, :rwkv_cols], widths_full[:2], tm_c)
        else:
            pc = _in_proj(cs, None, mod_c, w_in_l, widths_full, tm_c)

        wl = jnp.zeros((LORA_W, 5 * GROUP_W), F32)
        wl = wl.at[0:64, 0:256].set(decay_w2[l, 0]).at[64:128, 256:512].set(decay_w2[l, 1])
        wl = wl.at[128:192, 512:768].set(iclr_a2[l, 0]).at[192:256, 768:1024].set(iclr_a2[l, 1])
        wl = wl.at[256:384, 1024:1280].set(gate_g2[l]).astype(BF16)
        prep_args = (rkv_conv[l], wl, decay_w0[l], iclr_a0[l], k_k[l].reshape(1, -1), k_a[l].reshape(1, -1),
                     ones_bd)
        p_r, p_v, p_kk, p_gate, p_w2, p_b2, p_kd2 = _rwkv_prep(px[0], px[1], pc[0], pc[1], *prep_args)
        y_s = _rwkv_scan(*(_to_scan_layout(a, seq, ctx_len) for a in (p_r, p_v, p_kk, p_w2, p_b2, p_kd2)), tb)
        y2 = _from_scan(y_s, bsz, ctx_len)

        pool_bd = _block_diag(pool_w[l]).astype(BF16)
        fnet_bd = _block_diag(fnet_w[l]).astype(BF16)
        ws_b = gmlp_ws[l].astype(BF16)
        bias_tile = jnp.repeat(gmlp_bs[l].T, HEAD, axis=1)
        row = lambda a: a.reshape(1, -1)

        def mixers(parts, row0, n_rows, cs_tab, tm, tmf):
            a = _rwkv_readout(y2, p_r, p_v, p_kd2, p_gate, row(r_k[l]), row(gn_g[l]), row(gn_b[l]), ones_bd,
                              tm, row0, n_rows)
            p = _pool_mixer(parts[2], pool_bd, row(pool_scale[l]))
            g = _gmlp_mixer(parts[3], row(gmlp_ln_g[l]), row(gmlp_ln_b[l]), ws_b, bias_tile, ones_bd, tm)
            f = _fourier_mixer(parts[4], cs_tab, cc_bd, sc_bd, fnet_bd, row(fnet_b[l]), tmf)
            return a, p, g, f

        w_out_l = w_out[l].astype(BF16)
        w1_l = ffn_w1[l].astype(BF16)
        w2_l = ffn_w2[l].astype(BF16)
        parts_x = mixers(px, 0, seq, cs_x, tm_x, 256)
        xs = _out_proj(parts_x, xs, pos if first else None, mod_x, w_out_l, row(ln1_g[l]), row(ln1_b[l]), alpha, tm_x)
        xs = _ffn(xs, mod_x, w1_l, w2_l, row(ln2_g[l]), row(ln2_b[l]), alpha, tm_x)
        if not last:
            parts_c = mixers(pc, seq, ctx_len, cs_c, tm_c, tm_c)
            cs = _out_proj(parts_c, cs, None, mod_c, w_out_l, row(ln1_g[l]), row(ln1_b[l]), alpha, tm_c)
            cs = _ffn(cs, mod_c, w1_l, w2_l, row(ln2_g[l]), row(ln2_b[l]), alpha, tm_c)
    return xs
```

```python
import functools
import math

import jax
import jax.numpy as jnp
from jax import lax
from jax.experimental import pallas as pl
from jax.experimental.pallas import tpu as pltpu

F32 = jnp.float32
BF16 = jnp.bfloat16

V7X_LANES = 128
V7X_SUBLANES = 8
V7X_VMEM_BYTES = 64 * 1024 * 1024
V7X_VMEM_LIMIT_CAP = 56 * 1024 * 1024

GRID_W = 64
HEAD = 64
GROUP_W = 256
N_HEADS = GROUP_W // HEAD
LORA_W = 384
POOL_WINDOWS = (2, 4, 8, 16)
CHUNK = 128
LN_EPS = 1e-5
GN_EPS = 64e-5


def _cparams(sem, vmem_bytes):
    limit = int(min(max(vmem_bytes, 16 * 1024 * 1024), V7X_VMEM_LIMIT_CAP))
    return pltpu.CompilerParams(dimension_semantics=sem, vmem_limit_bytes=limit)


def _silu(x):
    return x * jax.nn.sigmoid(x)


def _seg_sum(x, ones_bd):
    hi = x.astype(BF16)
    lo = (x - hi.astype(F32)).astype(BF16)
    return (jnp.dot(hi, ones_bd, preferred_element_type=F32)
            + jnp.dot(lo, ones_bd, preferred_element_type=F32))


def _layer_norm(z, g, b):
    mu = jnp.mean(z, axis=-1, keepdims=True)
    zc = z - mu
    var = jnp.mean(zc * zc, axis=-1, keepdims=True)
    return zc * lax.rsqrt(var + LN_EPS) * g + b


def _mod_kernel(c_ref, w_ref, b_ref, o_ref):
    s = _silu(c_ref[...])
    o_ref[...] = jnp.dot(s.astype(BF16), w_ref[...], preferred_element_type=F32) + b_ref[...]


def _modulation(c_all, w_mod, b_mod):
    rows, d = c_all.shape
    n = w_mod.shape[1]
    tn = 1024
    return pl.pallas_call(
        _mod_kernel,
        grid=(n // tn,),
        in_specs=[pl.BlockSpec((rows, d), lambda j: (0, 0)),
                  pl.BlockSpec((d, tn), lambda j: (0, j)),
                  pl.BlockSpec((1, tn), lambda j: (0, j))],
        out_specs=pl.BlockSpec((rows, tn), lambda j: (0, j)),
        out_shape=jax.ShapeDtypeStruct((rows, n), F32),
        compiler_params=_cparams(("parallel",), 4 * d * tn * 2),
        name="modulation",
    )(c_all, w_mod, b_mod.reshape(1, n))


def _inproj_kernel(*refs, widths, add_pos):
    if add_pos:
        x_ref, pos_ref, mod_ref, w_ref = refs[:4]
        outs = refs[4:]
        x = x_ref[0] + pos_ref[...]
    else:
        x_ref, mod_ref, w_ref = refs[:3]
        outs = refs[3:]
        x = x_ref[0]
    sh = mod_ref[0, 0:1, :]
    sc = mod_ref[0, 1:2, :]
    h = (x * (1.0 + sc) + sh).astype(BF16)
    off = 0
    for o_ref, wd in zip(outs, widths):
        o_ref[0] = jnp.dot(h, w_ref[:, off:off + wd], preferred_element_type=F32)
        off += wd


def _in_proj(x, pos, mod, w, widths, tm):
    b, l, d = x.shape
    n = w.shape[1]
    assert sum(widths) == n
    add_pos = pos is not None
    in_specs = [pl.BlockSpec((1, tm, d), lambda bi, i: (bi, i, 0))]
    args = [x]
    if add_pos:
        in_specs.append(pl.BlockSpec((tm, d), lambda bi, i: (i, 0)))
        args.append(pos)
    in_specs += [pl.BlockSpec((1, 6, d), lambda bi, i: (bi, 0, 0)),
                 pl.BlockSpec((d, n), lambda bi, i: (0, 0))]
    args += [mod, w]
    vmem = 2 * (2 * tm * d * 4 + d * n * 2 + tm * n * 4) + tm * d * 8
    return pl.pallas_call(
        functools.partial(_inproj_kernel, widths=widths, add_pos=add_pos),
        grid=(b, l // tm),
        in_specs=in_specs,
        out_specs=[pl.BlockSpec((1, tm, wd), lambda bi, i: (bi, i, 0)) for wd in widths],
        out_shape=[jax.ShapeDtypeStruct((b, l, wd), F32) for wd in widths],
        compiler_params=_cparams(("parallel", "parallel"), vmem),
        name="in_proj",
    )(*args)


def _prep_kernel(zx_ref, zp_ref, zn_ref, zc_ref, lox_ref, loc_ref, conv_ref, wl_ref, w0_ref, a0_ref, kk_ref,
                 ka_ref, ones_ref, r_o, v_o, kk_o, gate_o, w_o, b_o, kd_o, *, tm, nx):
    i = pl.program_id(1)
    is_ctx = i >= nx
    z = jnp.where(is_ctx, zc_ref[0], zx_ref[0])
    lo = jnp.where(is_ctx, loc_ref[0], lox_ref[0])
    row = lax.broadcasted_iota(jnp.int32, z.shape, 0)
    has_prev = jnp.logical_and(i > 0, i < nx).astype(F32)
    has_next = (i < nx - 1).astype(F32)
    prev_row = zp_ref[0, V7X_SUBLANES - 1:V7X_SUBLANES, :] * has_prev
    next_row = zn_ref[0, 0:1, :] * has_next
    zm1 = jnp.where(row == 0, prev_row, pltpu.roll(z, 1, 0))
    zp1 = jnp.where(row == tm - 1, next_row, pltpu.roll(z, tm - 1, 0))
    c = zm1 * conv_ref[0:1, :] + z * conv_ref[1:2, :] + zp1 * conv_ref[2:3, :]
    r = c[:, 0:GROUP_W]
    k = c[:, GROUP_W:2 * GROUP_W]
    v = c[:, 2 * GROUP_W:3 * GROUP_W]

    col = lax.broadcasted_iota(jnp.int32, lo.shape, 1)
    act = jnp.where(col < 128, jnp.tanh(lo), jnp.where(col < 256, lo, jax.nn.sigmoid(lo)))
    pre = jnp.dot(act.astype(BF16), wl_ref[...], preferred_element_type=F32)

    ones_bd = ones_ref[...]
    kk = k * kk_ref[...]
    nrm = jnp.sqrt(_seg_sum(kk * kk, ones_bd))
    kk = kk / jnp.maximum(nrm, 1e-12)
    gate_o[0] = pre[:, 4 * GROUP_W:5 * GROUP_W]
    ka = ka_ref[...]
    for d in range(2):
        cols = slice(d * GROUP_W, (d + 1) * GROUP_W)
        r_o[0, :, cols] = r
        v_o[0, :, cols] = v
        kk_o[0, :, cols] = kk
        xw = w0_ref[d:d + 1, :] + pre[:, cols]
        w_o[0, :, cols] = jnp.exp(-math.exp(-0.5) * jax.nn.sigmoid(xw))
        a = jax.nn.sigmoid(a0_ref[d:d + 1, :] + pre[:, (2 + d) * GROUP_W:(3 + d) * GROUP_W])
        kd_o[0, :, cols] = k * (1.0 + (a - 1.0) * ka)
        b_o[0, :, cols] = kk * a


def _rwkv_prep(zx_rkv, zx_lora, zc_rkv, zc_lora, conv, wl, w0, a0, k_k, k_a, ones_bd):
    b, l, c3 = zx_rkv.shape
    tm = zc_rkv.shape[1]
    nx = l // tm
    nblk8 = l // V7X_SUBLANES
    r8 = tm // V7X_SUBLANES
    full = lambda shape: pl.BlockSpec(shape, lambda bi, i: (0,) * len(shape))
    xi = lambda i: jnp.minimum(i, nx - 1)
    in_specs = [
        pl.BlockSpec((1, tm, c3), lambda bi, i: (bi, xi(i), 0)),
        pl.BlockSpec((1, V7X_SUBLANES, c3), lambda bi, i: (bi, jnp.clip(i * r8 - 1, 0, nblk8 - 1), 0)),
        pl.BlockSpec((1, V7X_SUBLANES, c3), lambda bi, i: (bi, jnp.minimum((i + 1) * r8, nblk8 - 1), 0)),
        pl.BlockSpec((1, tm, c3), lambda bi, i: (bi, 0, 0)),
        pl.BlockSpec((1, tm, LORA_W), lambda bi, i: (bi, xi(i), 0)),
        pl.BlockSpec((1, tm, LORA_W), lambda bi, i: (bi, 0, 0)),
        full(conv.shape), full(wl.shape), full(w0.shape), full(a0.shape),
        full(k_k.shape), full(k_a.shape), full(ones_bd.shape),
    ]
    spec1 = pl.BlockSpec((1, tm, GROUP_W), lambda bi, i: (bi, i, 0))
    spec2 = pl.BlockSpec((1, tm, 2 * GROUP_W), lambda bi, i: (bi, i, 0))
    shp1 = jax.ShapeDtypeStruct((b, l + tm, GROUP_W), F32)
    shp2 = jax.ShapeDtypeStruct((b, l + tm, 2 * GROUP_W), F32)
    vmem = 2 * (2 * tm * (c3 + LORA_W) * 4 + 13 * tm * GROUP_W * 4) + 14 * tm * c3 * 4
    return pl.pallas_call(
        functools.partial(_prep_kernel, tm=tm, nx=nx),
        grid=(b, nx + 1),
        in_specs=in_specs,
        out_specs=[spec2] * 3 + [spec1] + [spec2] * 3,
        out_shape=[shp2] * 3 + [shp1] + [shp2] * 3,
        compiler_params=_cparams(("parallel", "arbitrary"), vmem),
        name="rwkv_prep",
    )(zx_rkv, zx_rkv, zx_rkv, zc_rkv, zx_lora, zc_lora, conv, wl, w0, a0, k_k, k_a, ones_bd)


SCAN_ROW_PITCH = 72


def _to_scan_kernel(xf_ref, xb_ref, o_ref, y_ref, *, nb, tt, step_major):
    rev = (tt - 1) - lax.broadcasted_iota(jnp.int32, (GROUP_W, tt), 1)
    for b in range(nb):
        xf = xf_ref[b].T
        xb = jnp.take_along_axis(xb_ref[b].T, rev, axis=1)
        for h in range(N_HEADS):
            y_ref[pl.ds((b * 2 * N_HEADS + h) * SCAN_ROW_PITCH, HEAD), :] = xf[h * HEAD:(h + 1) * HEAD, :]
            y_ref[pl.ds((b * 2 * N_HEADS + N_HEADS + h) * SCAN_ROW_PITCH, HEAD), :] = xb[h * HEAD:(h + 1) * HEAD, :]
    for k in range(HEAD):
        z = y_ref[pl.ds(k, nb * 2 * N_HEADS, stride=SCAN_ROW_PITCH), :]
        if step_major:
            o_ref[pl.ds(k, tt, stride=HEAD), :] = z.T
        else:
            o_ref[k] = z.T


def _to_scan_layout(x, n_lat, n_ctx, step_major=False):
    nb, t, c = x.shape
    tt = V7X_LANES
    assert nb * 2 * N_HEADS == V7X_LANES and c == 2 * GROUP_W and n_lat % tt == 0 and n_ctx % tt == 0
    nl, nc = n_lat // tt, n_ctx // tt
    n = nl + nc
    if step_major:
        out_spec = pl.BlockSpec((tt * HEAD, V7X_LANES), lambda j: (j, 0))
        out_shape = jax.ShapeDtypeStruct((t * HEAD, V7X_LANES), F32)
    else:
        out_spec = pl.BlockSpec((HEAD, tt, V7X_LANES), lambda j: (0, j, 0))
        out_shape = jax.ShapeDtypeStruct((HEAD, t, V7X_LANES), F32)
    out = pl.pallas_call(
        functools.partial(_to_scan_kernel, nb=nb, tt=tt, step_major=step_major),
        grid=(n,),
        in_specs=[pl.BlockSpec((nb, tt, GROUP_W), lambda j: (0, jnp.where(j < nc, nl + j, j - nc), 0)),
                  pl.BlockSpec((nb, tt, GROUP_W), lambda j: (0, n - 1 - j, 1))],
        out_specs=out_spec,
        out_shape=out_shape,
        scratch_shapes=[pltpu.VMEM((V7X_LANES * SCAN_ROW_PITCH, tt), F32)],
        compiler_params=_cparams(("parallel",), 2 * (nb * tt * c * 4 + HEAD * tt * V7X_LANES * 4)
                                 + 3 * V7X_LANES * SCAN_ROW_PITCH * tt * 4),
        name="to_scan_layout",
    )(x, x)
    return out.reshape(t, HEAD, V7X_LANES) if step_major else out


def _from_scan_kernel(yf_ref, yb_ref, o_ref, s_ref, *, nb, tt):
    rev = (tt - 1) - lax.broadcasted_iota(jnp.int32, (V7X_LANES, tt), 1)
    fwd_row = (lax.broadcasted_iota(jnp.int32, (V7X_LANES, tt), 0) & N_HEADS) == 0
    for v in range(HEAD):
        zf = yf_ref[pl.ds(v, tt, stride=HEAD), :].T
        zb = jnp.take_along_axis(yb_ref[pl.ds(v, tt, stride=HEAD), :].T, rev, axis=1)
        s_ref[pl.ds(v, V7X_LANES, stride=SCAN_ROW_PITCH), :] = jnp.where(fwd_row, zf, zb)
    groups = 2 * N_HEADS
    for b in range(nb):
        rows = [s_ref[pl.ds((b * groups + g) * SCAN_ROW_PITCH, HEAD), :] for g in range(groups)]
        o_ref[b] = jnp.concatenate(rows, axis=0).T


def _from_scan_layout(y, nb, n_lat, n_ctx):
    t = y.shape[0]
    tt = V7X_LANES
    nl, nc = n_lat // tt, n_ctx // tt
    n = nl + nc
    y2d = y.reshape(t * HEAD, V7X_LANES)
    blk = pl.BlockSpec((tt * HEAD, V7X_LANES), lambda j: (jnp.where(j < nl, j + nc, j - nl), 0))
    rblk = pl.BlockSpec((tt * HEAD, V7X_LANES), lambda j: (n - 1 - j, 0))
    return pl.pallas_call(
        functools.partial(_from_scan_kernel, nb=nb, tt=tt),
        grid=(n,),
        in_specs=[blk, rblk],
        out_specs=pl.BlockSpec((nb, tt, 2 * GROUP_W), lambda j: (0, j, 0)),
        out_shape=jax.ShapeDtypeStruct((nb, t, 2 * GROUP_W), F32),
        scratch_shapes=[pltpu.VMEM((V7X_LANES * SCAN_ROW_PITCH, tt), F32)],
        compiler_params=_cparams(("parallel",), 2 * (2 * tt * HEAD * V7X_LANES * 4 + nb * tt * 2 * GROUP_W * 4)
                                 + 3 * V7X_LANES * SCAN_ROW_PITCH * tt * 4),
        name="from_scan_layout",
    )(y2d, y2d)


def _scan_kernel(r_ref, v_ref, kk_ref, w_ref, b_ref, kd_ref, y_ref, h_ref, *, tb):
    @pl.when(pl.program_id(0) == 0)
    def _():
        h_ref[...] = jnp.zeros_like(h_ref)

    u0 = jnp.zeros((HEAD, V7X_LANES), F32)
    for k in range(HEAD):
        u0 = u0 + kk_ref[k, 0:1, :] * h_ref[k]

    def step(t, u):
        tn = jnp.minimum(t + 1, tb - 1)
        vv = v_ref[t]
        y = jnp.zeros((HEAD, V7X_LANES), F32)
        un = jnp.zeros((HEAD, V7X_LANES), F32)
        for k in range(HEAD):
            hn = (h_ref[k] * w_ref[k, pl.ds(t, 1), :] - b_ref[k, pl.ds(t, 1), :] * u
                  + kd_ref[k, pl.ds(t, 1), :] * vv)
            h_ref[k] = hn
            y = y + r_ref[k, pl.ds(t, 1), :] * hn
            un = un + kk_ref[k, pl.ds(tn, 1), :] * hn
        y_ref[t] = y
        return un

    lax.fori_loop(0, tb, step, u0)


def _rwkv_scan(r, v, kk, w, b, kd, tb):
    t_tot = r.shape[1]
    kspec = pl.BlockSpec((HEAD, tb, V7X_LANES), lambda i: (0, i, 0))
    sspec = pl.BlockSpec((tb, HEAD, V7X_LANES), lambda i: (i, 0, 0))
    blk = tb * HEAD * V7X_LANES * 4
    return pl.pallas_call(
        functools.partial(_scan_kernel, tb=tb),
        grid=(t_tot // tb,),
        in_specs=[kspec, sspec, kspec, kspec, kspec, kspec],
        out_specs=sspec,
        out_shape=jax.ShapeDtypeStruct((t_tot, HEAD, V7X_LANES), F32),
        scratch_shapes=[pltpu.VMEM((HEAD, HEAD, V7X_LANES), F32)],
        compiler_params=_cparams(("arbitrary",), 14 * blk + 3 * HEAD * HEAD * V7X_LANES * 4),
        name="rwkv_scan",
    )(r, v, kk, w, b, kd)


def _readout_kernel(y_ref, r_ref, v_ref, kd_ref, gate_ref, rk_ref, g_ref, b_ref, ones_ref, o_ref):
    ones_bd = ones_ref[...]
    inv = 1.0 / HEAD
    y = y_ref[0, :, 0:GROUP_W] + y_ref[0, :, GROUP_W:2 * GROUP_W]
    kd = kd_ref[0, :, 0:GROUP_W] + kd_ref[0, :, GROUP_W:2 * GROUP_W]
    mu = _seg_sum(y, ones_bd) * inv
    yc = y - mu
    var = _seg_sum(yc * yc, ones_bd) * inv
    yn = yc * lax.rsqrt(var + GN_EPS)
    bonus = _seg_sum(r_ref[0] * kd * rk_ref[...], ones_bd) * v_ref[0]
    o_ref[0] = (yn * g_ref[...] + b_ref[...] + bonus) * gate_ref[0]


def _rwkv_readout(y2, r, v, kd2, gate, r_k, gn_g, gn_b, ones_bd, tm, row0, n_rows):
    b = y2.shape[0]
    off = row0 // tm
    spec1 = pl.BlockSpec((1, tm, GROUP_W), lambda bi, i: (bi, i + off, 0))
    spec2 = pl.BlockSpec((1, tm, 2 * GROUP_W), lambda bi, i: (bi, i + off, 0))
    full = lambda shape: pl.BlockSpec(shape, lambda bi, i: (0,) * len(shape))
    return pl.pallas_call(
        _readout_kernel,
        grid=(b, n_rows // tm),
        in_specs=[spec2, spec1, spec1, spec2, spec1,
                  full(r_k.shape), full(gn_g.shape), full(gn_b.shape), full(ones_bd.shape)],
        out_specs=pl.BlockSpec((1, tm, GROUP_W), lambda bi, i: (bi, i, 0)),
        out_shape=jax.ShapeDtypeStruct((b, n_rows, GROUP_W), F32),
        compiler_params=_cparams(("parallel", "parallel"), 36 * tm * GROUP_W * 4),
        name="rwkv_readout",
    )(y2, r, v, kd2, gate, r_k, gn_g, gn_b, ones_bd)


def _pool_kernel(z_ref, w_ref, s_ref, o_ref, *, l):
    z = z_ref[0]
    row = lax.broadcasted_iota(jnp.int32, z.shape, 0)
    lane = lax.broadcasted_iota(jnp.int32, z.shape, 1)

    def shifted(x, o):
        y = pltpu.roll(x, (-o) % l, 0)
        return jnp.where((row + o >= 0) & (row + o < l), y, 0.0)

    def count(wd):
        lo = jnp.maximum(row - wd // 2, 0)
        hi = jnp.minimum(row + wd - wd // 2, l)
        return (hi - lo).astype(F32)

    s2 = shifted(z, -1) + z
    pooled = s2 / count(2)
    s4 = shifted(s2, -1) + shifted(s2, 1)
    pooled = jnp.where(lane >= 64, s4 / count(4), pooled)
    s8 = shifted(s4, -2) + shifted(s4, 2)
    pooled = jnp.where(lane >= 128, s8 / count(8), pooled)
    s16 = shifted(s8, -4) + shifted(s8, 4)
    pooled = jnp.where(lane >= 192, s16 / count(16), pooled)
    d = pooled - z
    o_ref[0] = jnp.dot(d.astype(BF16), w_ref[...], preferred_element_type=F32) * s_ref[...]


def _pool_mixer(z, w_bd, scale):
    b, l, c = z.shape
    spec = pl.BlockSpec((1, l, c), lambda bi: (bi, 0, 0))
    return pl.pallas_call(
        functools.partial(_pool_kernel, l=l),
        grid=(b,),
        in_specs=[spec, pl.BlockSpec(w_bd.shape, lambda bi: (0, 0)), pl.BlockSpec(scale.shape, lambda bi: (0, 0))],
        out_specs=spec,
        out_shape=jax.ShapeDtypeStruct((b, l, c), F32),
        compiler_params=_cparams(("parallel",), 20 * l * c * 4),
        name="pool_mixer",
    )(z, w_bd, scale)


def _gelu_tanh(x):
    return 0.5 * x * (1.0 + jnp.tanh(math.sqrt(2.0 / math.pi) * (x + 0.044715 * (x * x * x))))


def _gmlp_kernel(z_ref, g_ref, b_ref, ws_ref, bias_ref, ones_ref, o_ref, *, tm):
    ones_bd = ones_ref[...]
    inv = 1.0 / HEAD
    z = z_ref[0]
    u = _gelu_tanh(z[:, 0:GROUP_W])
    v = _gelu_tanh(z[:, GROUP_W:2 * GROUP_W])
    mu = _seg_sum(v, ones_bd) * inv
    vc = v - mu
    var = _seg_sum(vc * vc, ones_bd) * inv
    vn = (vc * lax.rsqrt(var + LN_EPS) * g_ref[...] + b_ref[...]).astype(BF16)
    lane = lax.broadcasted_iota(jnp.int32, (CHUNK, GROUP_W), 1)
    for c in range(tm // CHUNK):
        vchunk = vn[c * CHUNK:(c + 1) * CHUNK, :]
        sv = bias_ref[...]
        for g in range(4):
            part = jnp.dot(ws_ref[g], vchunk, preferred_element_type=F32)
            sv = sv + jnp.where((lane >= g * HEAD) & (lane < (g + 1) * HEAD), part, 0.0)
        o_ref[0, c * CHUNK:(c + 1) * CHUNK, :] = u[c * CHUNK:(c + 1) * CHUNK, :] * sv


def _gmlp_mixer(z, ln_g, ln_b, ws, bias_tile, ones_bd, tm):
    b, l, c2 = z.shape
    full = lambda shape: pl.BlockSpec(shape, lambda bi, i: (0,) * len(shape))
    return pl.pallas_call(
        functools.partial(_gmlp_kernel, tm=tm),
        grid=(b, l // tm),
        in_specs=[pl.BlockSpec((1, tm, c2), lambda bi, i: (bi, i, 0)), full(ln_g.shape), full(ln_b.shape),
                  full(ws.shape), full(bias_tile.shape), full(ones_bd.shape)],
        out_specs=pl.BlockSpec((1, tm, GROUP_W), lambda bi, i: (bi, i, 0)),
        out_shape=jax.ShapeDtypeStruct((b, l, GROUP_W), F32),
        compiler_params=_cparams(("parallel", "parallel"), 24 * tm * c2 * 4),
        name="gmlp_mixer",
    )(z, ln_g, ln_b, ws, bias_tile, ones_bd)


def _fnet_kernel(z_ref, cs_ref, cc_ref, sc_ref, w_ref, b_ref, o_ref, zcs_ref, *, l, scale):
    @pl.when(pl.program_id(1) == 0)
    def _():
        zb = z_ref[0].astype(BF16)
        zcs_ref[0:l, :] = jnp.dot(zb, cc_ref[...], preferred_element_type=F32).astype(BF16)
        zcs_ref[l:2 * l, :] = jnp.dot(zb, sc_ref[...], preferred_element_type=F32).astype(BF16)

    f = jnp.dot(cs_ref[...], zcs_ref[...], preferred_element_type=F32) * scale
    o_ref[0] = jnp.dot(f.astype(BF16), w_ref[...], preferred_element_type=F32) + b_ref[...]


def _fourier_mixer(z, cs, cc_bd, sc_bd, w_bd, bias, tm):
    b, l, c = z.shape
    full = lambda shape: pl.BlockSpec(shape, lambda bi, i: (0,) * len(shape))
    scale = 1.0 / math.sqrt(l * HEAD)
    vmem = 2 * (l * c * 4 + tm * 2 * l * 2 + tm * c * 4) + 2 * l * c * 2 + 6 * l * c * 4
    return pl.pallas_call(
        functools.partial(_fnet_kernel, l=l, scale=scale),
        grid=(b, l // tm),
        in_specs=[pl.BlockSpec((1, l, c), lambda bi, i: (bi, 0, 0)),
                  pl.BlockSpec((tm, 2 * l), lambda bi, i: (i, 0)),
                  full(cc_bd.shape), full(sc_bd.shape), full(w_bd.shape), full(bias.shape)],
        out_specs=pl.BlockSpec((1, tm, c), lambda bi, i: (bi, i, 0)),
        out_shape=jax.ShapeDtypeStruct((b, l, c), F32),
        scratch_shapes=[pltpu.VMEM((2 * l, c), BF16)],
        compiler_params=_cparams(("parallel", "arbitrary"), vmem),
        name="fourier_mixer",
    )(z, cs, cc_bd, sc_bd, w_bd, bias)


def _outproj_kernel(*refs, alpha, add_pos):
    if add_pos:
        a_ref, p_ref, g_ref, f_ref, x_ref, pos_ref, mod_ref, w_ref, lg_ref, lb_ref, o_ref = refs
        x = x_ref[0] + pos_ref[...]
    else:
        a_ref, p_ref, g_ref, f_ref, x_ref, mod_ref, w_ref, lg_ref, lb_ref, o_ref = refs
        x = x_ref[0]
    mix = jnp.dot(a_ref[0].astype(BF16), w_ref[0:GROUP_W, :], preferred_element_type=F32)
    for j, part in enumerate((p_ref, g_ref, f_ref), start=1):
        mix = mix + jnp.dot(part[0].astype(BF16), w_ref[j * GROUP_W:(j + 1) * GROUP_W, :],
                            preferred_element_type=F32)
    gate1 = mod_ref[0, 2:3, :]
    o_ref[0] = _layer_norm(alpha * x + gate1 * mix, lg_ref[...], lb_ref[...])


def _out_proj(parts, x, pos, mod, w_out, ln_g, ln_b, alpha, tm):
    b, l, d = x.shape
    add_pos = pos is not None
    pspec = pl.BlockSpec((1, tm, GROUP_W), lambda bi, i: (bi, i, 0))
    xspec = pl.BlockSpec((1, tm, d), lambda bi, i: (bi, i, 0))
    full = lambda shape: pl.BlockSpec(shape, lambda bi, i: (0,) * len(shape))
    in_specs = [pspec] * 4 + [xspec]
    args = list(parts) + [x]
    if add_pos:
        in_specs.append(pl.BlockSpec((tm, d), lambda bi, i: (i, 0)))
        args.append(pos)
    in_specs += [pl.BlockSpec((1, 6, d), lambda bi, i: (bi, 0, 0)), full(w_out.shape), full(ln_g.shape),
                 full(ln_b.shape)]
    args += [mod, w_out, ln_g, ln_b]
    vmem = 2 * (4 * tm * GROUP_W * 4 + 3 * tm * d * 4 + d * d * 2) + 6 * tm * d * 4
    return pl.pallas_call(
        functools.partial(_outproj_kernel, alpha=alpha, add_pos=add_pos),
        grid=(b, l // tm),
        in_specs=in_specs,
        out_specs=xspec,
        out_shape=jax.ShapeDtypeStruct((b, l, d), F32),
        compiler_params=_cparams(("parallel", "parallel"), vmem),
        name="out_proj_ln",
    )(*args)


def _ffn_kernel(x_ref, mod_ref, wg_ref, wu_ref, w2_ref, lg_ref, lb_ref, o_ref, *, alpha, f_chunks):
    x = x_ref[0]
    sh = mod_ref[0, 3:4, :]
    sc = mod_ref[0, 4:5, :]
    h = (x * (1.0 + sc) + sh).astype(BF16)
    acc = None
    for lo, hi in f_chunks:
        gate = jnp.dot(h, wg_ref[:, lo:hi], preferred_element_type=F32)
        up = jnp.dot(h, wu_ref[:, lo:hi], preferred_element_type=F32)
        act = (_silu(gate) * up).astype(BF16)
        part = jnp.dot(act, w2_ref[lo:hi, :], preferred_element_type=F32)
        acc = part if acc is None else acc + part
    gate2 = mod_ref[0, 5:6, :]
    o_ref[0] = _layer_norm(alpha * x + gate2 * acc, lg_ref[...], lb_ref[...])


def _ffn(x, mod, w1, w2, ln_g, ln_b, alpha, tm):
    b, l, d = x.shape
    f = w2.shape[0]
    mxu_n = 256
    half = (f // mxu_n + 1) // 2 * mxu_n
    f_chunks = ((0, half), (half, f))
    xspec = pl.BlockSpec((1, tm, d), lambda bi, i: (bi, i, 0))
    full = lambda shape: pl.BlockSpec(shape, lambda bi, i: (0,) * len(shape))
    once = pl.Buffered(1)
    vmem = 4 * tm * d * 4 + 3 * d * f * 2 + 3 * tm * d * 4 + 4 * tm * half * 4
    return pl.pallas_call(
        functools.partial(_ffn_kernel, alpha=alpha, f_chunks=f_chunks),
        grid=(b, l // tm),
        in_specs=[xspec,
                  pl.BlockSpec((1, 6, d), lambda bi, i: (bi, 0, 0)),
                  pl.BlockSpec((d, f), lambda bi, i: (0, 0), pipeline_mode=once),
                  pl.BlockSpec((d, f), lambda bi, i: (0, 1), pipeline_mode=once),
                  pl.BlockSpec((f, d), lambda bi, i: (0, 0), pipeline_mode=once),
                  full(ln_g.shape), full(ln_b.shape)],
        out_specs=xspec,
        out_shape=jax.ShapeDtypeStruct((b, l, d), F32),
        compiler_params=_cparams(("parallel", "parallel"), vmem),
        name="ffn_ln",
    )(x, mod, w1, w1, w2, ln_g, ln_b)


def _pos_embed(n_tok, dim):
    rows = n_tok // GRID_W
    row, col = jnp.meshgrid(jnp.arange(rows, dtype=F32), jnp.arange(GRID_W, dtype=F32), indexing='ij')
    quarter = dim // 4
    freqs = jnp.exp(-math.log(10000.0) * jnp.arange(quarter, dtype=F32) / quarter)

    def enc(p):
        ang = p.reshape(-1, 1) * freqs[None, :]
        return jnp.concatenate([jnp.sin(ang), jnp.cos(ang)], -1)

    return jnp.concatenate([enc(row), enc(col)], -1)


def _block_diag(blocks):
    g, n, m = blocks.shape
    eye = jnp.eye(g, dtype=blocks.dtype)
    return (eye[:, None, :, None] * blocks[:, :, None, :]).reshape(g * n, g * m)


def _dft_tables(n):
    def direct(rows, cols, period):
        ang = ((rows[:, None] * cols[None, :]) % period).astype(F32) * (2.0 * math.pi / period)
        return jnp.cos(ang), jnp.sin(ang)

    idx = jnp.arange(n, dtype=jnp.int32)
    if n <= 1024:
        return direct(idx, idx, n)
    lo = 64
    ca, sa = direct(jnp.arange(n // lo, dtype=jnp.int32), idx, n // lo)
    cb, sb = direct(jnp.arange(lo, dtype=jnp.int32), idx, n)
    ca, sa, cb, sb = ca[:, None, :], sa[:, None, :], cb[None], sb[None]
    return (ca * cb - sa * sb).reshape(n, n), (sa * cb + ca * sb).reshape(n, n)


def kernel(x, c, ctx, c_ctx, w_mod, b_mod, w_in, rkv_conv, decay_w0, decay_w2, iclr_a0, iclr_a2,
           gate_g2, k_k, k_a, r_k, gn_g, gn_b, pool_w, pool_scale, gmlp_ln_g, gmlp_ln_b, gmlp_ws,
           gmlp_bs, fnet_w, fnet_b, w_out, ln1_g, ln1_b, ln2_g, ln2_b, ffn_w1, ffn_w2):
    bsz, seq, d = x.shape
    ctx_len = ctx.shape[1]
    depth = w_in.shape[0]
    d_ff = ffn_w2.shape[1]
    alpha = (2 * depth) ** 0.25
    rwkv_cols = 3 * GROUP_W + LORA_W
    assert bsz * N_HEADS * 2 == V7X_LANES, "scan layout packs (direction, batch, head) on the 128 lanes"

    pos = _pos_embed(seq, d)
    ones_bd = _block_diag(jnp.ones((N_HEADS, HEAD, HEAD), F32)).astype(BF16)
    cc64, ss64 = _dft_tables(HEAD)
    cc_bd = _block_diag(jnp.tile(cc64[None], (4, 1, 1))).astype(BF16)
    sc_bd = _block_diag(jnp.tile(ss64[None], (4, 1, 1))).astype(BF16)

    def dft_rows(n):
        cn, sn = _dft_tables(n)
        return jnp.concatenate([cn, -sn], axis=1).astype(BF16)

    cs_x = dft_rows(seq)
    cs_c = dft_rows(ctx_len)

    pad_rows = (-(bsz + 1)) % V7X_SUBLANES
    c_all = jnp.concatenate([c, c_ctx[None], jnp.zeros((pad_rows, d), F32)], axis=0)

    tm_x = 512
    tm_c = ctx_len
    tb = 64

    xs, cs = x, ctx
    for l in range(depth):
        last = l == depth - 1
        first = l == 0
        m = _modulation(c_all, w_mod[l].astype(BF16), b_mod[l])
        mod_x = m[:bsz].reshape(bsz, 6, d)
        mod_c = jnp.broadcast_to(m[bsz].reshape(1, 6, d), (bsz, 6, d))

        w_in_l = w_in[l].astype(BF16)
        widths_full = (3 * GROUP_W, LORA_W, GROUP_W, 2 * GROUP_W, GROUP_W)
        px = _in_proj(xs, pos if first else None, mod_x, w_in_l, widths_full, tm_x)
        if last:
            pc = _in_proj(cs, None, mod_c, w_in_l[:, :rwkv_cols], widths_full[:2], tm_c)
        else:
            pc = _in_proj(cs, None, mod_c, w_in_l, widths_full, tm_c)

        wl = jnp.zeros((LORA_W, 5 * GROUP_W), F32)
        wl = wl.at[0:64, 0:256].set(decay_w2[l, 0]).at[64:128, 256:512].set(decay_w2[l, 1])
        wl = wl.at[128:192, 512:768].set(iclr_a2[l, 0]).at[192:256, 768:1024].set(iclr_a2[l, 1])
        wl = wl.at[256:384, 1024:1280].set(gate_g2[l]).astype(BF16)
        prep_args = (rkv_conv[l], wl, decay_w0[l], iclr_a0[l], k_k[l].reshape(1, -1), k_a[l].reshape(1, -1),
                     ones_bd)
        p_r, p_v, p_kk, p_gate, p_w2, p_b2, p_kd2 = _rwkv_prep(px[0], px[1], pc[0], pc[1], *prep_args)
        to_scan = functools.partial(_to_scan_layout, n_lat=seq, n_ctx=ctx_len)
        y_s = _rwkv_scan(to_scan(p_r), to_scan(p_v, step_major=True), to_scan(p_kk), to_scan(p_w2), to_scan(p_b2),
                         to_scan(p_kd2), tb)
        y2 = _from_scan_layout(y_s, bsz, seq, ctx_len)

        pool_bd = _block_diag(pool_w[l]).astype(BF16)
        fnet_bd = _block_diag(fnet_w[l]).astype(BF16)
        ws_b = gmlp_ws[l].astype(BF16)
        bias_tile = jnp.repeat(gmlp_bs[l].T, HEAD, axis=1)
        row = lambda a: a.reshape(1, -1)

        def mixers(parts, row0, n_rows, cs_tab, tm, tmf):
            a = _rwkv_readout(y2, p_r, p_v, p_kd2, p_gate, row(r_k[l]), row(gn_g[l]), row(gn_b[l]), ones_bd,
                              tm, row0, n_rows)
            p = _pool_mixer(parts[2], pool_bd, row(pool_scale[l]))
            g = _gmlp_mixer(parts[3], row(gmlp_ln_g[l]), row(gmlp_ln_b[l]), ws_b, bias_tile, ones_bd, tm)
            f = _fourier_mixer(parts[4], cs_tab, cc_bd, sc_bd, fnet_bd, row(fnet_b[l]), tmf)
            return a, p, g, f

        w_out_l = w_out[l].astype(BF16)
        w1_l = ffn_w1[l].astype(BF16)
        w2_l = ffn_w2[l].astype(BF16)
        parts_x = mixers(px, 0, seq, cs_x, tm_x, 256)
        xs = _out_proj(parts_x, xs, pos if first else None, mod_x, w_out_l, row(ln1_g[l]), row(ln1_b[l]), alpha, tm_x)
        xs = _ffn(xs, mod_x, w1_l, w2_l, row(ln2_g[l]), row(ln2_b[l]), alpha, tm_x)
        if not last:
            parts_c = mixers(pc, seq, ctx_len, cs_c, tm_c, tm_c)
            cs = _out_proj(parts_c, cs, None, mod_c, w_out_l, row(ln1_g[l]), row(ln1_b[l]), alpha, tm_c)
            cs = _ffn(cs, mod_c, w1_l, w2_l, row(ln2_g[l]), row(ln2_b[l]), alpha, tm_c)
    return xs
```

```python
import functools
import math

import jax
import jax.numpy as jnp
from jax import lax
from jax.experimental import pallas as pl
from jax.experimental.pallas import tpu as pltpu

F32 = jnp.float32
BF16 = jnp.bfloat16

V7X_LANES = 128
V7X_SUBLANES = 8
V7X_VMEM_BYTES = 64 * 1024 * 1024
V7X_VMEM_LIMIT_CAP = 56 * 1024 * 1024

GRID_W = 64
HEAD = 64
GROUP_W = 256
N_HEADS = GROUP_W // HEAD
LORA_W = 384
POOL_WINDOWS = (2, 4, 8, 16)
CHUNK = 128
LN_EPS = 1e-5
GN_EPS = 64e-5


def _cparams(sem, vmem_bytes):
    limit = int(min(max(vmem_bytes, 16 * 1024 * 1024), V7X_VMEM_LIMIT_CAP))
    return pltpu.CompilerParams(dimension_semantics=sem, vmem_limit_bytes=limit)


def _silu(x):
    return x * jax.nn.sigmoid(x)


def _seg_sum(x, ones_bd):
    hi = x.astype(BF16)
    lo = (x - hi.astype(F32)).astype(BF16)
    return (jnp.dot(hi, ones_bd, preferred_element_type=F32)
            + jnp.dot(lo, ones_bd, preferred_element_type=F32))


def _layer_norm(z, g, b):
    mu = jnp.mean(z, axis=-1, keepdims=True)
    zc = z - mu
    var = jnp.mean(zc * zc, axis=-1, keepdims=True)
    return zc * lax.rsqrt(var + LN_EPS) * g + b


def _mod_kernel(c_ref, w_ref, b_ref, o_ref):
    s = _silu(c_ref[...])
    o_ref[...] = jnp.dot(s.astype(BF16), w_ref[...], preferred_element_type=F32) + b_ref[...]


def _modulation(c_all, w_mod, b_mod):
    rows, d = c_all.shape
    n = w_mod.shape[1]
    tn = 1024
    return pl.pallas_call(
        _mod_kernel,
        grid=(n // tn,),
        in_specs=[pl.BlockSpec((rows, d), lambda j: (0, 0)),
                  pl.BlockSpec((d, tn), lambda j: (0, j)),
                  pl.BlockSpec((1, tn), lambda j: (0, j))],
        out_specs=pl.BlockSpec((rows, tn), lambda j: (0, j)),
        out_shape=jax.ShapeDtypeStruct((rows, n), F32),
        compiler_params=_cparams(("parallel",), 4 * d * tn * 2),
        name="modulation",
    )(c_all, w_mod, b_mod.reshape(1, n))


def _inproj_kernel(*refs, widths, add_pos):
    if add_pos:
        x_ref, pos_ref, mod_ref, w_ref = refs[:4]
        outs = refs[4:]
        x = x_ref[0] + pos_ref[...]
    else:
        x_ref, mod_ref, w_ref = refs[:3]
        outs = refs[3:]
        x = x_ref[0]
    sh = mod_ref[0, 0:1, :]
    sc = mod_ref[0, 1:2, :]
    h = (x * (1.0 + sc) + sh).astype(BF16)
    off = 0
    for o_ref, wd in zip(outs, widths):
        o_ref[0] = jnp.dot(h, w_ref[:, off:off + wd], preferred_element_type=F32)
        off += wd


def _in_proj(x, pos, mod, w, widths, tm):
    b, l, d = x.shape
    n = w.shape[1]
    assert sum(widths) == n
    add_pos = pos is not None
    in_specs = [pl.BlockSpec((1, tm, d), lambda bi, i: (bi, i, 0))]
    args = [x]
    if add_pos:
        in_specs.append(pl.BlockSpec((tm, d), lambda bi, i: (i, 0)))
        args.append(pos)
    in_specs += [pl.BlockSpec((1, 6, d), lambda bi, i: (bi, 0, 0)),
                 pl.BlockSpec((d, n), lambda bi, i: (0, 0))]
    args += [mod, w]
    vmem = 2 * (2 * tm * d * 4 + d * n * 2 + tm * n * 4) + tm * d * 8
    return pl.pallas_call(
        functools.partial(_inproj_kernel, widths=widths, add_pos=add_pos),
        grid=(b, l // tm),
        in_specs=in_specs,
        out_specs=[pl.BlockSpec((1, tm, wd), lambda bi, i: (bi, i, 0)) for wd in widths],
        out_shape=[jax.ShapeDtypeStruct((b, l, wd), F32) for wd in widths],
        compiler_params=_cparams(("parallel", "parallel"), vmem),
        name="in_proj",
    )(*args)


def _prep_kernel(zx_ref, zp_ref, zn_ref, zc_ref, lox_ref, loc_ref, conv_ref, wl_ref, w0_ref, a0_ref, kk_ref,
                 ka_ref, ones_ref, r_o, v_o, kk_o, gate_o, w_o, b_o, kd_o, *, tm, nx):
    i = pl.program_id(1)
    is_ctx = i >= nx
    z = jnp.where(is_ctx, zc_ref[0], zx_ref[0])
    lo = jnp.where(is_ctx, loc_ref[0], lox_ref[0])
    row = lax.broadcasted_iota(jnp.int32, z.shape, 0)
    has_prev = jnp.logical_and(i > 0, i < nx).astype(F32)
    has_next = (i < nx - 1).astype(F32)
    prev_row = zp_ref[0, V7X_SUBLANES - 1:V7X_SUBLANES, :] * has_prev
    next_row = zn_ref[0, 0:1, :] * has_next
    zm1 = jnp.where(row == 0, prev_row, pltpu.roll(z, 1, 0))
    zp1 = jnp.where(row == tm - 1, next_row, pltpu.roll(z, tm - 1, 0))
    c = zm1 * conv_ref[0:1, :] + z * conv_ref[1:2, :] + zp1 * conv_ref[2:3, :]
    r = c[:, 0:GROUP_W]
    k = c[:, GROUP_W:2 * GROUP_W]
    v = c[:, 2 * GROUP_W:3 * GROUP_W]

    col = lax.broadcasted_iota(jnp.int32, lo.shape, 1)
    act = jnp.where(col < 128, jnp.tanh(lo), jnp.where(col < 256, lo, jax.nn.sigmoid(lo)))
    pre = jnp.dot(act.astype(BF16), wl_ref[...], preferred_element_type=F32)

    ones_bd = ones_ref[...]
    kk = k * kk_ref[...]
    nrm = jnp.sqrt(_seg_sum(kk * kk, ones_bd))
    kk = kk / jnp.maximum(nrm, 1e-12)
    gate_o[0] = pre[:, 4 * GROUP_W:5 * GROUP_W]
    ka = ka_ref[...]
    for d in range(2):
        cols = slice(d * GROUP_W, (d + 1) * GROUP_W)
        r_o[0, :, cols] = r
        v_o[0, :, cols] = v
        kk_o[0, :, cols] = kk
        xw = w0_ref[d:d + 1, :] + pre[:, cols]
        w_o[0, :, cols] = -math.exp(-0.5) * jax.nn.sigmoid(xw)
        a = jax.nn.sigmoid(a0_ref[d:d + 1, :] + pre[:, (2 + d) * GROUP_W:(3 + d) * GROUP_W])
        kd_o[0, :, cols] = k * (1.0 + (a - 1.0) * ka)
        b_o[0, :, cols] = kk * a


def _rwkv_prep(zx_rkv, zx_lora, zc_rkv, zc_lora, conv, wl, w0, a0, k_k, k_a, ones_bd):
    b, l, c3 = zx_rkv.shape
    tm = zc_rkv.shape[1]
    nx = l // tm
    nblk8 = l // V7X_SUBLANES
    r8 = tm // V7X_SUBLANES
    full = lambda shape: pl.BlockSpec(shape, lambda bi, i: (0,) * len(shape))
    xi = lambda i: jnp.minimum(i, nx - 1)
    in_specs = [
        pl.BlockSpec((1, tm, c3), lambda bi, i: (bi, xi(i), 0)),
        pl.BlockSpec((1, V7X_SUBLANES, c3), lambda bi, i: (bi, jnp.clip(i * r8 - 1, 0, nblk8 - 1), 0)),
        pl.BlockSpec((1, V7X_SUBLANES, c3), lambda bi, i: (bi, jnp.minimum((i + 1) * r8, nblk8 - 1), 0)),
        pl.BlockSpec((1, tm, c3), lambda bi, i: (bi, 0, 0)),
        pl.BlockSpec((1, tm, LORA_W), lambda bi, i: (bi, xi(i), 0)),
        pl.BlockSpec((1, tm, LORA_W), lambda bi, i: (bi, 0, 0)),
        full(conv.shape), full(wl.shape), full(w0.shape), full(a0.shape),
        full(k_k.shape), full(k_a.shape), full(ones_bd.shape),
    ]
    spec1 = pl.BlockSpec((1, tm, GROUP_W), lambda bi, i: (bi, i, 0))
    spec2 = pl.BlockSpec((1, tm, 2 * GROUP_W), lambda bi, i: (bi, i, 0))
    shp1 = jax.ShapeDtypeStruct((b, l + tm, GROUP_W), F32)
    shp2 = jax.ShapeDtypeStruct((b, l + tm, 2 * GROUP_W), F32)
    vmem = 2 * (2 * tm * (c3 + LORA_W) * 4 + 13 * tm * GROUP_W * 4) + 14 * tm * c3 * 4
    return pl.pallas_call(
        functools.partial(_prep_kernel, tm=tm, nx=nx),
        grid=(b, nx + 1),
        in_specs=in_specs,
        out_specs=[spec2] * 3 + [spec1] + [spec2] * 3,
        out_shape=[shp2] * 3 + [shp1] + [shp2] * 3,
        compiler_params=_cparams(("parallel", "arbitrary"), vmem),
        name="rwkv_prep",
    )(zx_rkv, zx_rkv, zx_rkv, zc_rkv, zx_lora, zc_lora, conv, wl, w0, a0, k_k, k_a, ones_bd)


SCAN_ROW_PITCH = 72


def _to_scan_kernel(xf_ref, xb_ref, o_ref, y_ref, *, nb, tt, step_major):
    rev = (tt - 1) - lax.broadcasted_iota(jnp.int32, (GROUP_W, tt), 1)
    for b in range(nb):
        xf = xf_ref[b].T
        xb = jnp.take_along_axis(xb_ref[b].T, rev, axis=1)
        for h in range(N_HEADS):
            y_ref[pl.ds((b * 2 * N_HEADS + h) * SCAN_ROW_PITCH, HEAD), :] = xf[h * HEAD:(h + 1) * HEAD, :]
            y_ref[pl.ds((b * 2 * N_HEADS + N_HEADS + h) * SCAN_ROW_PITCH, HEAD), :] = xb[h * HEAD:(h + 1) * HEAD, :]
    for k in range(HEAD):
        z = y_ref[pl.ds(k, nb * 2 * N_HEADS, stride=SCAN_ROW_PITCH), :]
        if step_major:
            o_ref[pl.ds(k, tt, stride=HEAD), :] = z.T
        else:
            o_ref[k] = z.T


def _to_scan_layout(x, n_lat, n_ctx, step_major=False):
    nb, t, c = x.shape
    tt = V7X_LANES
    assert nb * 2 * N_HEADS == V7X_LANES and c == 2 * GROUP_W and n_lat % tt == 0 and n_ctx % tt == 0
    nl, nc = n_lat // tt, n_ctx // tt
    n = nl + nc
    if step_major:
        out_spec = pl.BlockSpec((tt * HEAD, V7X_LANES), lambda j: (j, 0))
        out_shape = jax.ShapeDtypeStruct((t * HEAD, V7X_LANES), F32)
    else:
        out_spec = pl.BlockSpec((HEAD, tt, V7X_LANES), lambda j: (0, j, 0))
        out_shape = jax.ShapeDtypeStruct((HEAD, t, V7X_LANES), F32)
    out = pl.pallas_call(
        functools.partial(_to_scan_kernel, nb=nb, tt=tt, step_major=step_major),
        grid=(n,),
        in_specs=[pl.BlockSpec((nb, tt, GROUP_W), lambda j: (0, jnp.where(j < nc, nl + j, j - nc), 0)),
                  pl.BlockSpec((nb, tt, GROUP_W), lambda j: (0, n - 1 - j, 1))],
        out_specs=out_spec,
        out_shape=out_shape,
        scratch_shapes=[pltpu.VMEM((V7X_LANES * SCAN_ROW_PITCH, tt), F32)],
        compiler_params=_cparams(("parallel",), 2 * (nb * tt * c * 4 + HEAD * tt * V7X_LANES * 4)
                                 + 3 * V7X_LANES * SCAN_ROW_PITCH * tt * 4),
        name="to_scan_layout",
    )(x, x)
    return out.reshape(t, HEAD, V7X_LANES) if step_major else out


def _from_scan_kernel(yf_ref, yb_ref, o_ref, s_ref, *, nb, tt):
    rev = (tt - 1) - lax.broadcasted_iota(jnp.int32, (V7X_LANES, tt), 1)
    fwd_row = (lax.broadcasted_iota(jnp.int32, (V7X_LANES, tt), 0) & N_HEADS) == 0
    for v in range(HEAD):
        zf = yf_ref[pl.ds(v, tt, stride=HEAD), :].T
        zb = jnp.take_along_axis(yb_ref[pl.ds(v, tt, stride=HEAD), :].T, rev, axis=1)
        s_ref[pl.ds(v, V7X_LANES, stride=SCAN_ROW_PITCH), :] = jnp.where(fwd_row, zf, zb)
    groups = 2 * N_HEADS
    for b in range(nb):
        rows = [s_ref[pl.ds((b * groups + g) * SCAN_ROW_PITCH, HEAD), :] for g in range(groups)]
        o_ref[b] = jnp.concatenate(rows, axis=0).T


def _from_scan_layout(y, nb, n_lat, n_ctx):
    t = y.shape[0]
    tt = V7X_LANES
    nl, nc = n_lat // tt, n_ctx // tt
    n = nl + nc
    y2d = y.reshape(t * HEAD, V7X_LANES)
    blk = pl.BlockSpec((tt * HEAD, V7X_LANES), lambda j: (jnp.where(j < nl, j + nc, j - nl), 0))
    rblk = pl.BlockSpec((tt * HEAD, V7X_LANES), lambda j: (n - 1 - j, 0))
    return pl.pallas_call(
        functools.partial(_from_scan_kernel, nb=nb, tt=tt),
        grid=(n,),
        in_specs=[blk, rblk],
        out_specs=pl.BlockSpec((nb, tt, 2 * GROUP_W), lambda j: (0, j, 0)),
        out_shape=jax.ShapeDtypeStruct((nb, t, 2 * GROUP_W), F32),
        scratch_shapes=[pltpu.VMEM((V7X_LANES * SCAN_ROW_PITCH, tt), F32)],
        compiler_params=_cparams(("parallel",), 2 * (2 * tt * HEAD * V7X_LANES * 4 + nb * tt * 2 * GROUP_W * 4)
                                 + 3 * V7X_LANES * SCAN_ROW_PITCH * tt * 4),
        name="from_scan_layout",
    )(y2d, y2d)


SCAN_SLAB_PAD = 8


def _scan_kernel(r_ref, v_ref, kk_ref, lw_ref, b_ref, kd_ref, y_ref, h_ref, s_ref, g_ref, *, tb):
    @pl.when(pl.program_id(0) == 0)
    def _():
        h_ref[...] = jnp.zeros_like(h_ref)

    tile = (V7X_SUBLANES, V7X_LANES)
    R_G, KK_G, B_G, KD_G = range(4)

    sub = lax.broadcasted_iota(jnp.int32, tile, 0)

    def scale(k, carry):
        lw = lw_ref[k]
        parts = []
        total = jnp.zeros(tile, F32)
        for j in range(tb // V7X_SUBLANES):
            x = lw[j * V7X_SUBLANES:(j + 1) * V7X_SUBLANES, :]
            for s in (1, 2, 4):
                x = x + jnp.where(sub >= s, pltpu.roll(x, s, 0), 0.0)
            x = x + total
            total = jnp.broadcast_to(x[V7X_SUBLANES - 1:V7X_SUBLANES, :], tile)
            parts.append(x)
        cum = jnp.concatenate(parts, axis=0)
        g = jnp.exp(cum)
        g_inv = jnp.exp(-cum)
        s_ref[R_G, k, 0:tb] = r_ref[k] * g
        s_ref[KK_G, k, 0:tb] = kk_ref[k] * jnp.exp(cum - lw)
        s_ref[B_G, k, 0:tb] = b_ref[k] * g_inv
        s_ref[KD_G, k, 0:tb] = kd_ref[k] * g_inv
        g_ref[pl.ds(k, 1), :] = g[tb - 1:tb, :]
        return carry

    lax.fori_loop(0, HEAD, scale, 0)

    u0 = jnp.zeros((HEAD, V7X_LANES), F32)
    for k in range(HEAD):
        u0 = u0 + s_ref[KK_G, k, 0:1, :] * h_ref[k]

    def step(t, u):
        tn = jnp.minimum(t + 1, tb - 1)
        vv = v_ref[t]
        y = jnp.zeros((HEAD, V7X_LANES), F32)
        un = jnp.zeros((HEAD, V7X_LANES), F32)
        for k in range(HEAD):
            hn = h_ref[k] - s_ref[B_G, k, pl.ds(t, 1), :] * u + s_ref[KD_G, k, pl.ds(t, 1), :] * vv
            h_ref[k] = hn
            y = y + s_ref[R_G, k, pl.ds(t, 1), :] * hn
            un = un + s_ref[KK_G, k, pl.ds(tn, 1), :] * hn
        y_ref[t] = y
        return un

    lax.fori_loop(0, tb, step, u0)
    for k in range(HEAD):
        h_ref[k] = h_ref[k] * g_ref[k:k + 1, :]


def _rwkv_scan(r, v, kk, w, b, kd, tb):
    t_tot = r.shape[1]
    kspec = pl.BlockSpec((HEAD, tb, V7X_LANES), lambda i: (0, i, 0))
    sspec = pl.BlockSpec((tb, HEAD, V7X_LANES), lambda i: (i, 0, 0))
    blk = tb * HEAD * V7X_LANES * 4
    return pl.pallas_call(
        functools.partial(_scan_kernel, tb=tb),
        grid=(t_tot // tb,),
        in_specs=[kspec, sspec, kspec, kspec, kspec, kspec],
        out_specs=sspec,
        out_shape=jax.ShapeDtypeStruct((t_tot, HEAD, V7X_LANES), F32),
        scratch_shapes=[pltpu.VMEM((HEAD, HEAD, V7X_LANES), F32),
                        pltpu.VMEM((4, HEAD, tb + SCAN_SLAB_PAD, V7X_LANES), F32),
                        pltpu.VMEM((HEAD, V7X_LANES), F32)],
        compiler_params=_cparams(("arbitrary",), 14 * blk + 5 * blk + 3 * HEAD * HEAD * V7X_LANES * 4),
        name="rwkv_scan",
    )(r, v, kk, w, b, kd)


def _readout_kernel(y_ref, r_ref, v_ref, kd_ref, gate_ref, rk_ref, g_ref, b_ref, ones_ref, o_ref):
    ones_bd = ones_ref[...]
    inv = 1.0 / HEAD
    y = y_ref[0, :, 0:GROUP_W] + y_ref[0, :, GROUP_W:2 * GROUP_W]
    kd = kd_ref[0, :, 0:GROUP_W] + kd_ref[0, :, GROUP_W:2 * GROUP_W]
    mu = _seg_sum(y, ones_bd) * inv
    yc = y - mu
    var = _seg_sum(yc * yc, ones_bd) * inv
    yn = yc * lax.rsqrt(var + GN_EPS)
    bonus = _seg_sum(r_ref[0] * kd * rk_ref[...], ones_bd) * v_ref[0]
    o_ref[0] = (yn * g_ref[...] + b_ref[...] + bonus) * gate_ref[0]


def _rwkv_readout(y2, r, v, kd2, gate, r_k, gn_g, gn_b, ones_bd, tm, row0, n_rows):
    b = y2.shape[0]
    off = row0 // tm
    spec1 = pl.BlockSpec((1, tm, GROUP_W), lambda bi, i: (bi, i + off, 0))
    spec2 = pl.BlockSpec((1, tm, 2 * GROUP_W), lambda bi, i: (bi, i + off, 0))
    full = lambda shape: pl.BlockSpec(shape, lambda bi, i: (0,) * len(shape))
    return pl.pallas_call(
        _readout_kernel,
        grid=(b, n_rows // tm),
        in_specs=[spec2, spec1, spec1, spec2, spec1,
                  full(r_k.shape), full(gn_g.shape), full(gn_b.shape), full(ones_bd.shape)],
        out_specs=pl.BlockSpec((1, tm, GROUP_W), lambda bi, i: (bi, i, 0)),
        out_shape=jax.ShapeDtypeStruct((b, n_rows, GROUP_W), F32),
        compiler_params=_cparams(("parallel", "parallel"), 36 * tm * GROUP_W * 4),
        name="rwkv_readout",
    )(y2, r, v, kd2, gate, r_k, gn_g, gn_b, ones_bd)


def _pool_kernel(z_ref, w_ref, s_ref, o_ref, *, l):
    z = z_ref[0]
    row = lax.broadcasted_iota(jnp.int32, z.shape, 0)
    lane = lax.broadcasted_iota(jnp.int32, z.shape, 1)

    def shifted(x, o):
        y = pltpu.roll(x, (-o) % l, 0)
        return jnp.where((row + o >= 0) & (row + o < l), y, 0.0)

    def count(wd):
        lo = jnp.maximum(row - wd // 2, 0)
        hi = jnp.minimum(row + wd - wd // 2, l)
        return (hi - lo).astype(F32)

    s2 = shifted(z, -1) + z
    pooled = s2 / count(2)
    s4 = shifted(s2, -1) + shifted(s2, 1)
    pooled = jnp.where(lane >= 64, s4 / count(4), pooled)
    s8 = shifted(s4, -2) + shifted(s4, 2)
    pooled = jnp.where(lane >= 128, s8 / count(8), pooled)
    s16 = shifted(s8, -4) + shifted(s8, 4)
    pooled = jnp.where(lane >= 192, s16 / count(16), pooled)
    d = pooled - z
    o_ref[0] = jnp.dot(d.astype(BF16), w_ref[...], preferred_element_type=F32) * s_ref[...]


def _pool_mixer(z, w_bd, scale):
    b, l, c = z.shape
    spec = pl.BlockSpec((1, l, c), lambda bi: (bi, 0, 0))
    return pl.pallas_call(
        functools.partial(_pool_kernel, l=l),
        grid=(b,),
        in_specs=[spec, pl.BlockSpec(w_bd.shape, lambda bi: (0, 0)), pl.BlockSpec(scale.shape, lambda bi: (0, 0))],
        out_specs=spec,
        out_shape=jax.ShapeDtypeStruct((b, l, c), F32),
        compiler_params=_cparams(("parallel",), 20 * l * c * 4),
        name="pool_mixer",
    )(z, w_bd, scale)


def _gelu_tanh(x):
    return 0.5 * x * (1.0 + jnp.tanh(math.sqrt(2.0 / math.pi) * (x + 0.044715 * (x * x * x))))


def _gmlp_kernel(z_ref, g_ref, b_ref, ws_ref, bias_ref, ones_ref, o_ref, *, tm):
    ones_bd = ones_ref[...]
    inv = 1.0 / HEAD
    z = z_ref[0]
    u = _gelu_tanh(z[:, 0:GROUP_W])
    v = _gelu_tanh(z[:, GROUP_W:2 * GROUP_W])
    mu = _seg_sum(v, ones_bd) * inv
    vc = v - mu
    var = _seg_sum(vc * vc, ones_bd) * inv
    vn = (vc * lax.rsqrt(var + LN_EPS) * g_ref[...] + b_ref[...]).astype(BF16)
    lane = lax.broadcasted_iota(jnp.int32, (CHUNK, GROUP_W), 1)
    for c in range(tm // CHUNK):
        vchunk = vn[c * CHUNK:(c + 1) * CHUNK, :]
        sv = bias_ref[...]
        for g in range(4):
            part = jnp.dot(ws_ref[g], vchunk, preferred_element_type=F32)
            sv = sv + jnp.where((lane >= g * HEAD) & (lane < (g + 1) * HEAD), part, 0.0)
        o_ref[0, c * CHUNK:(c + 1) * CHUNK, :] = u[c * CHUNK:(c + 1) * CHUNK, :] * sv


def _gmlp_mixer(z, ln_g, ln_b, ws, bias_tile, ones_bd, tm):
    b, l, c2 = z.shape
    full = lambda shape: pl.BlockSpec(shape, lambda bi, i: (0,) * len(shape))
    return pl.pallas_call(
        functools.partial(_gmlp_kernel, tm=tm),
        grid=(b, l // tm),
        in_specs=[pl.BlockSpec((1, tm, c2), lambda bi, i: (bi, i, 0)), full(ln_g.shape), full(ln_b.shape),
                  full(ws.shape), full(bias_tile.shape), full(ones_bd.shape)],
        out_specs=pl.BlockSpec((1, tm, GROUP_W), lambda bi, i: (bi, i, 0)),
        out_shape=jax.ShapeDtypeStruct((b, l, GROUP_W), F32),
        compiler_params=_cparams(("parallel", "parallel"), 24 * tm * c2 * 4),
        name="gmlp_mixer",
    )(z, ln_g, ln_b, ws, bias_tile, ones_bd)


def _fnet_kernel(z_ref, cs_ref, cc_ref, sc_ref, w_ref, b_ref, o_ref, zcs_ref, *, l, scale):
    @pl.when(pl.program_id(1) == 0)
    def _():
        zb = z_ref[0].astype(BF16)
        zcs_ref[0:l, :] = jnp.dot(zb, cc_ref[...], preferred_element_type=F32).astype(BF16)
        zcs_ref[l:2 * l, :] = jnp.dot(zb, sc_ref[...], preferred_element_type=F32).astype(BF16)

    f = jnp.dot(cs_ref[...], zcs_ref[...], preferred_element_type=F32) * scale
    o_ref[0] = jnp.dot(f.astype(BF16), w_ref[...], preferred_element_type=F32) + b_ref[...]


def _fourier_mixer(z, cs, cc_bd, sc_bd, w_bd, bias, tm):
    b, l, c = z.shape
    full = lambda shape: pl.BlockSpec(shape, lambda bi, i: (0,) * len(shape))
    scale = 1.0 / math.sqrt(l * HEAD)
    vmem = 2 * (l * c * 4 + tm * 2 * l * 2 + tm * c * 4) + 2 * l * c * 2 + 6 * l * c * 4
    return pl.pallas_call(
        functools.partial(_fnet_kernel, l=l, scale=scale),
        grid=(b, l // tm),
        in_specs=[pl.BlockSpec((1, l, c), lambda bi, i: (bi, 0, 0)),
                  pl.BlockSpec((tm, 2 * l), lambda bi, i: (i, 0)),
                  full(cc_bd.shape), full(sc_bd.shape), full(w_bd.shape), full(bias.shape)],
        out_specs=pl.BlockSpec((1, tm, c), lambda bi, i: (bi, i, 0)),
        out_shape=jax.ShapeDtypeStruct((b, l, c), F32),
        scratch_shapes=[pltpu.VMEM((2 * l, c), BF16)],
        compiler_params=_cparams(("parallel", "arbitrary"), vmem),
        name="fourier_mixer",
    )(z, cs, cc_bd, sc_bd, w_bd, bias)


def _outproj_kernel(*refs, alpha, add_pos):
    if add_pos:
        a_ref, p_ref, g_ref, f_ref, x_ref, pos_ref, mod_ref, w_ref, lg_ref, lb_ref, o_ref = refs
        x = x_ref[0] + pos_ref[...]
    else:
        a_ref, p_ref, g_ref, f_ref, x_ref, mod_ref, w_ref, lg_ref, lb_ref, o_ref = refs
        x = x_ref[0]
    mix = jnp.dot(a_ref[0].astype(BF16), w_ref[0:GROUP_W, :], preferred_element_type=F32)
    for j, part in enumerate((p_ref, g_ref, f_ref), start=1):
        mix = mix + jnp.dot(part[0].astype(BF16), w_ref[j * GROUP_W:(j + 1) * GROUP_W, :],
                            preferred_element_type=F32)
    gate1 = mod_ref[0, 2:3, :]
    o_ref[0] = _layer_norm(alpha * x + gate1 * mix, lg_ref[...], lb_ref[...])


def _out_proj(parts, x, pos, mod, w_out, ln_g, ln_b, alpha, tm):
    b, l, d = x.shape
    add_pos = pos is not None
    pspec = pl.BlockSpec((1, tm, GROUP_W), lambda bi, i: (bi, i, 0))
    xspec = pl.BlockSpec((1, tm, d), lambda bi, i: (bi, i, 0))
    full = lambda shape: pl.BlockSpec(shape, lambda bi, i: (0,) * len(shape))
    in_specs = [pspec] * 4 + [xspec]
    args = list(parts) + [x]
    if add_pos:
        in_specs.append(pl.BlockSpec((tm, d), lambda bi, i: (i, 0)))
        args.append(pos)
    in_specs += [pl.BlockSpec((1, 6, d), lambda bi, i: (bi, 0, 0)), full(w_out.shape), full(ln_g.shape),
                 full(ln_b.shape)]
    args += [mod, w_out, ln_g, ln_b]
    vmem = 2 * (4 * tm * GROUP_W * 4 + 3 * tm * d * 4 + d * d * 2) + 6 * tm * d * 4
    return pl.pallas_call(
        functools.partial(_outproj_kernel, alpha=alpha, add_pos=add_pos),
        grid=(b, l // tm),
        in_specs=in_specs,
        out_specs=xspec,
        out_shape=jax.ShapeDtypeStruct((b, l, d), F32),
        compiler_params=_cparams(("parallel", "parallel"), vmem),
        name="out_proj_ln",
    )(*args)


def _ffn_kernel(x_ref, mod_ref, wg_ref, wu_ref, w2_ref, lg_ref, lb_ref, o_ref, *, alpha, f_chunks):
    x = x_ref[0]
    sh = mod_ref[0, 3:4, :]
    sc = mod_ref[0, 4:5, :]
    h = (x * (1.0 + sc) + sh).astype(BF16)
    acc = None
    for lo, hi in f_chunks:
        gate = jnp.dot(h, wg_ref[:, lo:hi], preferred_element_type=F32)
        up = jnp.dot(h, wu_ref[:, lo:hi], preferred_element_type=F32)
        act = (_silu(gate) * up).astype(BF16)
        part = jnp.dot(act, w2_ref[lo:hi, :], preferred_element_type=F32)
        acc = part if acc is None else acc + part
    gate2 = mod_ref[0, 5:6, :]
    o_ref[0] = _layer_norm(alpha * x + gate2 * acc, lg_ref[...], lb_ref[...])


def _ffn(x, mod, w1, w2, ln_g, ln_b, alpha, tm):
    b, l, d = x.shape
    f = w2.shape[0]
    mxu_n = 256
    half = (f // mxu_n + 1) // 2 * mxu_n
    f_chunks = ((0, half), (half, f))
    xspec = pl.BlockSpec((1, tm, d), lambda bi, i: (bi, i, 0))
    full = lambda shape: pl.BlockSpec(shape, lambda bi, i: (0,) * len(shape))
    once = pl.Buffered(1)
    vmem = 4 * tm * d * 4 + 3 * d * f * 2 + 3 * tm * d * 4 + 4 * tm * half * 4
    return pl.pallas_call(
        functools.partial(_ffn_kernel, alpha=alpha, f_chunks=f_chunks),
        grid=(b, l // tm),
        in_specs=[xspec,
                  pl.BlockSpec((1, 6, d), lambda bi, i: (bi, 0, 0)),
                  pl.BlockSpec((d, f), lambda bi, i: (0, 0), pipeline_mode=once),
                  pl.BlockSpec((d, f), lambda bi, i: (0, 1), pipeline_mode=once),
                  pl.BlockSpec((f, d), lambda bi, i: (0, 0), pipeline_mode=once),
                  full(ln_g.shape), full(ln_b.shape)],
        out_specs=xspec,
        out_shape=jax.ShapeDtypeStruct((b, l, d), F32),
        compiler_params=_cparams(("parallel", "parallel"), vmem),
        name="ffn_ln",
    )(x, mod, w1, w1, w2, ln_g, ln_b)


def _pos_embed(n_tok, dim):
    rows = n_tok // GRID_W
    row, col = jnp.meshgrid(jnp.arange(rows, dtype=F32), jnp.arange(GRID_W, dtype=F32), indexing='ij')
    quarter = dim // 4
    freqs = jnp.exp(-math.log(10000.0) * jnp.arange(quarter, dtype=F32) / quarter)

    def enc(p):
        ang = p.reshape(-1, 1) * freqs[None, :]
        return jnp.concatenate([jnp.sin(ang), jnp.cos(ang)], -1)

    return jnp.concatenate([enc(row), enc(col)], -1)


def _block_diag(blocks):
    g, n, m = blocks.shape
    eye = jnp.eye(g, dtype=blocks.dtype)
    return (eye[:, None, :, None] * blocks[:, :, None, :]).reshape(g * n, g * m)


def _dft_tables(n):
    def direct(rows, cols, period):
        ang = ((rows[:, None] * cols[None, :]) % period).astype(F32) * (2.0 * math.pi / period)
        return jnp.cos(ang), jnp.sin(ang)

    idx = jnp.arange(n, dtype=jnp.int32)
    if n <= 1024:
        return direct(idx, idx, n)
    lo = 64
    ca, sa = direct(jnp.arange(n // lo, dtype=jnp.int32), idx, n // lo)
    cb, sb = direct(jnp.arange(lo, dtype=jnp.int32), idx, n)
    ca, sa, cb, sb = ca[:, None, :], sa[:, None, :], cb[None], sb[None]
    return (ca * cb - sa * sb).reshape(n, n), (sa * cb + ca * sb).reshape(n, n)


def kernel(x, c, ctx, c_ctx, w_mod, b_mod, w_in, rkv_conv, decay_w0, decay_w2, iclr_a0, iclr_a2,
           gate_g2, k_k, k_a, r_k, gn_g, gn_b, pool_w, pool_scale, gmlp_ln_g, gmlp_ln_b, gmlp_ws,
           gmlp_bs, fnet_w, fnet_b, w_out, ln1_g, ln1_b, ln2_g, ln2_b, ffn_w1, ffn_w2):
    bsz, seq, d = x.shape
    ctx_len = ctx.shape[1]
    depth = w_in.shape[0]
    d_ff = ffn_w2.shape[1]
    alpha = (2 * depth) ** 0.25
    rwkv_cols = 3 * GROUP_W + LORA_W
    assert bsz * N_HEADS * 2 == V7X_LANES, "scan layout packs (direction, batch, head) on the 128 lanes"

    pos = _pos_embed(seq, d)
    ones_bd = _block_diag(jnp.ones((N_HEADS, HEAD, HEAD), F32)).astype(BF16)
    cc64, ss64 = _dft_tables(HEAD)
    cc_bd = _block_diag(jnp.tile(cc64[None], (4, 1, 1))).astype(BF16)
    sc_bd = _block_diag(jnp.tile(ss64[None], (4, 1, 1))).astype(BF16)

    def dft_rows(n):
        cn, sn = _dft_tables(n)
        return jnp.concatenate([cn, -sn], axis=1).astype(BF16)

    cs_x = dft_rows(seq)
    cs_c = dft_rows(ctx_len)

    pad_rows = (-(bsz + 1)) % V7X_SUBLANES
    c_all = jnp.concatenate([c, c_ctx[None], jnp.zeros((pad_rows, d), F32)], axis=0)

    tm_x = 512
    tm_c = ctx_len
    tb = 64

    xs, cs = x, ctx
    for l in range(depth):
        last = l == depth - 1
        first = l == 0
        m = _modulation(c_all, w_mod[l].astype(BF16), b_mod[l])
        mod_x = m[:bsz].reshape(bsz, 6, d)
        mod_c = jnp.broadcast_to(m[bsz].reshape(1, 6, d), (bsz, 6, d))

        w_in_l = w_in[l].astype(BF16)
        widths_full = (3 * GROUP_W, LORA_W, GROUP_W, 2 * GROUP_W, GROUP_W)
        px = _in_proj(xs, pos if first else None, mod_x, w_in_l, widths_full, tm_x)
        if last:
            pc = _in_proj(cs, None, mod_c, w_in_l[:, :rwkv_cols], widths_full[:2], tm_c)
        else:
            pc = _in_proj(cs, None, mod_c, w_in_l, widths_full, tm_c)

        wl = jnp.zeros((LORA_W, 5 * GROUP_W), F32)
        wl = wl.at[0:64, 0:256].set(decay_w2[l, 0]).at[64:128, 256:512].set(decay_w2[l, 1])
        wl = wl.at[128:192, 512:768].set(iclr_a2[l, 0]).at[192:256, 768:1024].set(iclr_a2[l, 1])
        wl = wl.at[256:384, 1024:1280].set(gate_g2[l]).astype(BF16)
        prep_args = (rkv_conv[l], wl, decay_w0[l], iclr_a0[l], k_k[l].reshape(1, -1), k_a[l].reshape(1, -1),
                     ones_bd)
        p_r, p_v, p_kk, p_gate, p_w2, p_b2, p_kd2 = _rwkv_prep(px[0], px[1], pc[0], pc[1], *prep_args)
        to_scan = functools.partial(_to_scan_layout, n_lat=seq, n_ctx=ctx_len)
        y_s = _rwkv_scan(to_scan(p_r), to_scan(p_v, step_major=True), to_scan(p_kk), to_scan(p_w2), to_scan(p_b2),
                         to_scan(p_kd2), tb)
        y2 = _from_scan_layout(y_s, bsz, seq, ctx_len)

        pool_bd = _block_diag(pool_w[l]).astype(BF16)
        fnet_bd = _block_diag(fnet_w[l]).astype(BF16)
        ws_b = gmlp_ws[l].astype(BF16)
        bias_tile = jnp.repeat(gmlp_bs[l].T, HEAD, axis=1)
        row = lambda a: a.reshape(1, -1)

        def mixers(parts, row0, n_rows, cs_tab, tm, tmf):
            a = _rwkv_readout(y2, p_r, p_v, p_kd2, p_gate, row(r_k[l]), row(gn_g[l]), row(gn_b[l]), ones_bd,
                              tm, row0, n_rows)
            p = _pool_mixer(parts[2], pool_bd, row(pool_scale[l]))
            g = _gmlp_mixer(parts[3], row(gmlp_ln_g[l]), row(gmlp_ln_b[l]), ws_b, bias_tile, ones_bd, tm)
            f = _fourier_mixer(parts[4], cs_tab, cc_bd, sc_bd, fnet_bd, row(fnet_b[l]), tmf)
            return a, p, g, f

        w_out_l = w_out[l].astype(BF16)
        w1_l = ffn_w1[l].astype(BF16)
        w2_l = ffn_w2[l].astype(BF16)
        parts_x = mixers(px, 0, seq, cs_x, tm_x, 256)
        xs = _out_proj(parts_x, xs, pos if first else None, mod_x, w_out_l, row(ln1_g[l]), row(ln1_b[l]), alpha, tm_x)
        xs = _ffn(xs, mod_x, w1_l, w2_l, row(ln2_g[l]), row(ln2_b[l]), alpha, tm_x)
        if not last:
            parts_c = mixers(pc, seq, ctx_len, cs_c, tm_c, tm_c)
            cs = _out_proj(parts_c, cs, None, mod_c, w_out_l, row(ln1_g[l]), row(ln1_b[l]), alpha, tm_c)
            cs = _ffn(cs, mod_c, w1_l, w2_l, row(ln2_g[l]), row(ln2_b[l]), alpha, tm_c)
    return xs
```

```python
import functools
import math

import jax
import jax.numpy as jnp
from jax import lax
from jax.experimental import pallas as pl
from jax.experimental.pallas import tpu as pltpu

F32 = jnp.float32
BF16 = jnp.bfloat16

V7X_LANES = 128
V7X_SUBLANES = 8
V7X_VMEM_BYTES = 64 * 1024 * 1024
V7X_VMEM_LIMIT_CAP = 56 * 1024 * 1024

GRID_W = 64
HEAD = 64
GROUP_W = 256
N_HEADS = GROUP_W // HEAD
LORA_W = 384
POOL_WINDOWS = (2, 4, 8, 16)
CHUNK = 128
LN_EPS = 1e-5
GN_EPS = 64e-5


def _cparams(sem, vmem_bytes):
    limit = int(min(max(vmem_bytes, 16 * 1024 * 1024), V7X_VMEM_LIMIT_CAP))
    return pltpu.CompilerParams(dimension_semantics=sem, vmem_limit_bytes=limit)


def _silu(x):
    return x * jax.nn.sigmoid(x)


def _seg_sum(x, ones_bd):
    hi = x.astype(BF16)
    lo = (x - hi.astype(F32)).astype(BF16)
    return (jnp.dot(hi, ones_bd, preferred_element_type=F32)
            + jnp.dot(lo, ones_bd, preferred_element_type=F32))


def _layer_norm(z, g, b):
    mu = jnp.mean(z, axis=-1, keepdims=True)
    zc = z - mu
    var = jnp.mean(zc * zc, axis=-1, keepdims=True)
    return zc * lax.rsqrt(var + LN_EPS) * g + b


def _mod_kernel(c_ref, w_ref, b_ref, o_ref):
    s = _silu(c_ref[...])
    o_ref[...] = jnp.dot(s.astype(BF16), w_ref[...], preferred_element_type=F32) + b_ref[...]


def _modulation(c_all, w_mod, b_mod):
    rows, d = c_all.shape
    n = w_mod.shape[1]
    tn = 1024
    return pl.pallas_call(
        _mod_kernel,
        grid=(n // tn,),
        in_specs=[pl.BlockSpec((rows, d), lambda j: (0, 0)),
                  pl.BlockSpec((d, tn), lambda j: (0, j)),
                  pl.BlockSpec((1, tn), lambda j: (0, j))],
        out_specs=pl.BlockSpec((rows, tn), lambda j: (0, j)),
        out_shape=jax.ShapeDtypeStruct((rows, n), F32),
        compiler_params=_cparams(("parallel",), 4 * d * tn * 2),
        name="modulation",
    )(c_all, w_mod, b_mod.reshape(1, n))


def _inproj_kernel(*refs, widths, add_pos):
    if add_pos:
        x_ref, pos_ref, mod_ref, w_ref = refs[:4]
        outs = refs[4:]
        x = x_ref[0] + pos_ref[...]
    else:
        x_ref, mod_ref, w_ref = refs[:3]
        outs = refs[3:]
        x = x_ref[0]
    sh = mod_ref[0, 0:1, :]
    sc = mod_ref[0, 1:2, :]
    h = (x * (1.0 + sc) + sh).astype(BF16)
    off = 0
    for o_ref, wd in zip(outs, widths):
        o_ref[0] = jnp.dot(h, w_ref[:, off:off + wd], preferred_element_type=F32)
        off += wd


def _in_proj(x, pos, mod, w, widths, tm):
    b, l, d = x.shape
    n = w.shape[1]
    assert sum(widths) == n
    add_pos = pos is not None
    in_specs = [pl.BlockSpec((1, tm, d), lambda bi, i: (bi, i, 0))]
    args = [x]
    if add_pos:
        in_specs.append(pl.BlockSpec((tm, d), lambda bi, i: (i, 0)))
        args.append(pos)
    in_specs += [pl.BlockSpec((1, 6, d), lambda bi, i: (bi, 0, 0)),
                 pl.BlockSpec((d, n), lambda bi, i: (0, 0))]
    args += [mod, w]
    vmem = 2 * (2 * tm * d * 4 + d * n * 2 + tm * n * 4) + tm * d * 8
    return pl.pallas_call(
        functools.partial(_inproj_kernel, widths=widths, add_pos=add_pos),
        grid=(b, l // tm),
        in_specs=in_specs,
        out_specs=[pl.BlockSpec((1, tm, wd), lambda bi, i: (bi, i, 0)) for wd in widths],
        out_shape=[jax.ShapeDtypeStruct((b, l, wd), F32) for wd in widths],
        compiler_params=_cparams(("parallel", "parallel"), vmem),
        name="in_proj",
    )(*args)


def _prep_kernel(zx_ref, zp_ref, zn_ref, zc_ref, lox_ref, loc_ref, conv_ref, wl_ref, w0_ref, a0_ref, kk_ref,
                 ka_ref, ones_ref, r_o, v_o, kk_o, gate_o, w_o, b_o, kd_o, *, tm, nx):
    i = pl.program_id(1)
    is_ctx = i >= nx
    z = jnp.where(is_ctx, zc_ref[0], zx_ref[0])
    lo = jnp.where(is_ctx, loc_ref[0], lox_ref[0])
    row = lax.broadcasted_iota(jnp.int32, z.shape, 0)
    has_prev = jnp.logical_and(i > 0, i < nx).astype(F32)
    has_next = (i < nx - 1).astype(F32)
    prev_row = zp_ref[0, V7X_SUBLANES - 1:V7X_SUBLANES, :] * has_prev
    next_row = zn_ref[0, 0:1, :] * has_next
    zm1 = jnp.where(row == 0, prev_row, pltpu.roll(z, 1, 0))
    zp1 = jnp.where(row == tm - 1, next_row, pltpu.roll(z, tm - 1, 0))
    c = zm1 * conv_ref[0:1, :] + z * conv_ref[1:2, :] + zp1 * conv_ref[2:3, :]
    r = c[:, 0:GROUP_W]
    k = c[:, GROUP_W:2 * GROUP_W]
    v = c[:, 2 * GROUP_W:3 * GROUP_W]

    col = lax.broadcasted_iota(jnp.int32, lo.shape, 1)
    act = jnp.where(col < 128, jnp.tanh(lo), jnp.where(col < 256, lo, jax.nn.sigmoid(lo)))
    pre = jnp.dot(act.astype(BF16), wl_ref[...], preferred_element_type=F32)

    ones_bd = ones_ref[...]
    kk = k * kk_ref[...]
    nrm = jnp.sqrt(_seg_sum(kk * kk, ones_bd))
    kk = kk / jnp.maximum(nrm, 1e-12)
    r_o[0] = r
    v_o[0] = v
    kk_o[0] = kk
    gate_o[0] = pre[:, 4 * GROUP_W:5 * GROUP_W]
    ka = ka_ref[...]
    for d in range(2):
        cols = slice(d * GROUP_W, (d + 1) * GROUP_W)
        xw = w0_ref[d:d + 1, :] + pre[:, cols]
        w_o[0, :, cols] = -math.exp(-0.5) * jax.nn.sigmoid(xw)
        a = jax.nn.sigmoid(a0_ref[d:d + 1, :] + pre[:, (2 + d) * GROUP_W:(3 + d) * GROUP_W])
        kd_o[0, :, cols] = k * (1.0 + (a - 1.0) * ka)
        b_o[0, :, cols] = kk * a


def _rwkv_prep(zx_rkv, zx_lora, zc_rkv, zc_lora, conv, wl, w0, a0, k_k, k_a, ones_bd):
    b, l, c3 = zx_rkv.shape
    tm = zc_rkv.shape[1]
    nx = l // tm
    nblk8 = l // V7X_SUBLANES
    r8 = tm // V7X_SUBLANES
    full = lambda shape: pl.BlockSpec(shape, lambda bi, i: (0,) * len(shape))
    xi = lambda i: jnp.minimum(i, nx - 1)
    in_specs = [
        pl.BlockSpec((1, tm, c3), lambda bi, i: (bi, xi(i), 0)),
        pl.BlockSpec((1, V7X_SUBLANES, c3), lambda bi, i: (bi, jnp.clip(i * r8 - 1, 0, nblk8 - 1), 0)),
        pl.BlockSpec((1, V7X_SUBLANES, c3), lambda bi, i: (bi, jnp.minimum((i + 1) * r8, nblk8 - 1), 0)),
        pl.BlockSpec((1, tm, c3), lambda bi, i: (bi, 0, 0)),
        pl.BlockSpec((1, tm, LORA_W), lambda bi, i: (bi, xi(i), 0)),
        pl.BlockSpec((1, tm, LORA_W), lambda bi, i: (bi, 0, 0)),
        full(conv.shape), full(wl.shape), full(w0.shape), full(a0.shape),
        full(k_k.shape), full(k_a.shape), full(ones_bd.shape),
    ]
    spec1 = pl.BlockSpec((1, tm, GROUP_W), lambda bi, i: (bi, i, 0))
    spec2 = pl.BlockSpec((1, tm, 2 * GROUP_W), lambda bi, i: (bi, i, 0))
    shp1 = jax.ShapeDtypeStruct((b, l + tm, GROUP_W), F32)
    shp2 = jax.ShapeDtypeStruct((b, l + tm, 2 * GROUP_W), F32)
    vmem = 2 * (2 * tm * (c3 + LORA_W) * 4 + 13 * tm * GROUP_W * 4) + 14 * tm * c3 * 4
    return pl.pallas_call(
        functools.partial(_prep_kernel, tm=tm, nx=nx),
        grid=(b, nx + 1),
        in_specs=in_specs,
        out_specs=[spec1] * 4 + [spec2] * 3,
        out_shape=[shp1] * 4 + [shp2] * 3,
        compiler_params=_cparams(("parallel", "arbitrary"), vmem),
        name="rwkv_prep",
    )(zx_rkv, zx_rkv, zx_rkv, zc_rkv, zx_lora, zc_lora, conv, wl, w0, a0, k_k, k_a, ones_bd)


SCAN_ROW_PITCH = 72


def _to_scan_kernel(xf_ref, xb_ref, o_ref, y_ref, *, nb, tt, step_major):
    rev = (tt - 1) - lax.broadcasted_iota(jnp.int32, (GROUP_W, tt), 1)
    for b in range(nb):
        xf = xf_ref[b].T
        xb = jnp.take_along_axis(xb_ref[b].T, rev, axis=1)
        for h in range(N_HEADS):
            y_ref[pl.ds((b * 2 * N_HEADS + h) * SCAN_ROW_PITCH, HEAD), :] = xf[h * HEAD:(h + 1) * HEAD, :]
            y_ref[pl.ds((b * 2 * N_HEADS + N_HEADS + h) * SCAN_ROW_PITCH, HEAD), :] = xb[h * HEAD:(h + 1) * HEAD, :]
    for k in range(HEAD):
        z = y_ref[pl.ds(k, nb * 2 * N_HEADS, stride=SCAN_ROW_PITCH), :]
        if step_major:
            o_ref[pl.ds(k, tt, stride=HEAD), :] = z.T
        else:
            o_ref[k] = z.T


def _to_scan_layout(x, n_lat, n_ctx, step_major=False):
    nb, t, c = x.shape
    tt = V7X_LANES
    assert nb * 2 * N_HEADS == V7X_LANES and c in (GROUP_W, 2 * GROUP_W) and n_lat % tt == 0 and n_ctx % tt == 0
    nl, nc = n_lat // tt, n_ctx // tt
    n = nl + nc
    bwd_cols = c // GROUP_W - 1
    if step_major:
        out_spec = pl.BlockSpec((tt * HEAD, V7X_LANES), lambda j: (j, 0))
        out_shape = jax.ShapeDtypeStruct((t * HEAD, V7X_LANES), F32)
    else:
        out_spec = pl.BlockSpec((HEAD, tt, V7X_LANES), lambda j: (0, j, 0))
        out_shape = jax.ShapeDtypeStruct((HEAD, t, V7X_LANES), F32)
    out = pl.pallas_call(
        functools.partial(_to_scan_kernel, nb=nb, tt=tt, step_major=step_major),
        grid=(n,),
        in_specs=[pl.BlockSpec((nb, tt, GROUP_W), lambda j: (0, jnp.where(j < nc, nl + j, j - nc), 0)),
                  pl.BlockSpec((nb, tt, GROUP_W), lambda j: (0, n - 1 - j, bwd_cols))],
        out_specs=out_spec,
        out_shape=out_shape,
        scratch_shapes=[pltpu.VMEM((V7X_LANES * SCAN_ROW_PITCH, tt), F32)],
        compiler_params=_cparams(("parallel",), 2 * (nb * tt * 2 * GROUP_W * 4 + HEAD * tt * V7X_LANES * 4)
                                 + 3 * V7X_LANES * SCAN_ROW_PITCH * tt * 4),
        name="to_scan_layout",
    )(x, x)
    return out.reshape(t, HEAD, V7X_LANES) if step_major else out


def _from_scan_kernel(yf_ref, yb_ref, o_ref, s_ref, *, nb, tt):
    rev = (tt - 1) - lax.broadcasted_iota(jnp.int32, (V7X_LANES, tt), 1)
    fwd_row = (lax.broadcasted_iota(jnp.int32, (V7X_LANES, tt), 0) & N_HEADS) == 0
    for v in range(HEAD):
        zf = yf_ref[pl.ds(v, tt, stride=HEAD), :].T
        zb = jnp.take_along_axis(yb_ref[pl.ds(v, tt, stride=HEAD), :].T, rev, axis=1)
        s_ref[pl.ds(v, V7X_LANES, stride=SCAN_ROW_PITCH), :] = jnp.where(fwd_row, zf, zb)
    groups = 2 * N_HEADS
    for b in range(nb):
        rows = [s_ref[pl.ds((b * groups + g) * SCAN_ROW_PITCH, HEAD), :] for g in range(groups)]
        o_ref[b] = jnp.concatenate(rows, axis=0).T


def _from_scan_layout(y, nb, n_lat, n_ctx):
    t = y.shape[0]
    tt = V7X_LANES
    nl, nc = n_lat // tt, n_ctx // tt
    n = nl + nc
    y2d = y.reshape(t * HEAD, V7X_LANES)
    blk = pl.BlockSpec((tt * HEAD, V7X_LANES), lambda j: (jnp.where(j < nl, j + nc, j - nl), 0))
    rblk = pl.BlockSpec((tt * HEAD, V7X_LANES), lambda j: (n - 1 - j, 0))
    return pl.pallas_call(
        functools.partial(_from_scan_kernel, nb=nb, tt=tt),
        grid=(n,),
        in_specs=[blk, rblk],
        out_specs=pl.BlockSpec((nb, tt, 2 * GROUP_W), lambda j: (0, j, 0)),
        out_shape=jax.ShapeDtypeStruct((nb, t, 2 * GROUP_W), F32),
        scratch_shapes=[pltpu.VMEM((V7X_LANES * SCAN_ROW_PITCH, tt), F32)],
        compiler_params=_cparams(("parallel",), 2 * (2 * tt * HEAD * V7X_LANES * 4 + nb * tt * 2 * GROUP_W * 4)
                                 + 3 * V7X_LANES * SCAN_ROW_PITCH * tt * 4),
        name="from_scan_layout",
    )(y2d, y2d)


SCAN_SLAB_PAD = 8


def _scan_kernel(r_ref, v_ref, kk_ref, lw_ref, b_ref, kd_ref, y_ref, h_ref, s_ref, g_ref, *, tb):
    @pl.when(pl.program_id(0) == 0)
    def _():
        h_ref[...] = jnp.zeros_like(h_ref)

    tile = (V7X_SUBLANES, V7X_LANES)
    R_G, KK_G, B_G, KD_G = range(4)

    sub = lax.broadcasted_iota(jnp.int32, tile, 0)

    def scale(k, carry):
        lw = lw_ref[k]
        parts = []
        total = jnp.zeros(tile, F32)
        for j in range(tb // V7X_SUBLANES):
            x = lw[j * V7X_SUBLANES:(j + 1) * V7X_SUBLANES, :]
            for s in (1, 2, 4):
                x = x + jnp.where(sub >= s, pltpu.roll(x, s, 0), 0.0)
            x = x + total
            total = jnp.broadcast_to(x[V7X_SUBLANES - 1:V7X_SUBLANES, :], tile)
            parts.append(x)
        cum = jnp.concatenate(parts, axis=0)
        g = jnp.exp(cum)
        g_inv = jnp.exp(-cum)
        s_ref[R_G, k, 0:tb] = r_ref[k] * g
        s_ref[KK_G, k, 0:tb] = kk_ref[k] * jnp.exp(cum - lw)
        s_ref[B_G, k, 0:tb] = b_ref[k] * g_inv
        s_ref[KD_G, k, 0:tb] = kd_ref[k] * g_inv
        g_ref[pl.ds(k, 1), :] = g[tb - 1:tb, :]
        return carry

    lax.fori_loop(0, HEAD, scale, 0)

    u0 = jnp.zeros((HEAD, V7X_LANES), F32)
    for k in range(HEAD):
        u0 = u0 + s_ref[KK_G, k, 0:1, :] * h_ref[k]

    def step(t, u):
        tn = jnp.minimum(t + 1, tb - 1)
        vv = v_ref[t]
        y = jnp.zeros((HEAD, V7X_LANES), F32)
        un = jnp.zeros((HEAD, V7X_LANES), F32)
        for k in range(HEAD):
            hn = h_ref[k] - s_ref[B_G, k, pl.ds(t, 1), :] * u + s_ref[KD_G, k, pl.ds(t, 1), :] * vv
            h_ref[k] = hn
            y = y + s_ref[R_G, k, pl.ds(t, 1), :] * hn
            un = un + s_ref[KK_G, k, pl.ds(tn, 1), :] * hn
        y_ref[t] = y
        return un

    lax.fori_loop(0, tb, step, u0)
    for k in range(HEAD):
        h_ref[k] = h_ref[k] * g_ref[k:k + 1, :]


def _rwkv_scan(r, v, kk, w, b, kd, tb):
    t_tot = r.shape[1]
    kspec = pl.BlockSpec((HEAD, tb, V7X_LANES), lambda i: (0, i, 0))
    sspec = pl.BlockSpec((tb, HEAD, V7X_LANES), lambda i: (i, 0, 0))
    blk = tb * HEAD * V7X_LANES * 4
    return pl.pallas_call(
        functools.partial(_scan_kernel, tb=tb),
        grid=(t_tot // tb,),
        in_specs=[kspec, sspec, kspec, kspec, kspec, kspec],
        out_specs=sspec,
        out_shape=jax.ShapeDtypeStruct((t_tot, HEAD, V7X_LANES), F32),
        scratch_shapes=[pltpu.VMEM((HEAD, HEAD, V7X_LANES), F32),
                        pltpu.VMEM((4, HEAD, tb + SCAN_SLAB_PAD, V7X_LANES), F32),
                        pltpu.VMEM((HEAD, V7X_LANES), F32)],
        compiler_params=_cparams(("arbitrary",), 14 * blk + 5 * blk + 3 * HEAD * HEAD * V7X_LANES * 4),
        name="rwkv_scan",
    )(r, v, kk, w, b, kd)


def _readout(y2, r, v, kd2, gate, r_k, gn_g, gn_b, ones_bd):
    inv = 1.0 / HEAD
    y = y2[:, 0:GROUP_W] + y2[:, GROUP_W:2 * GROUP_W]
    kd = kd2[:, 0:GROUP_W] + kd2[:, GROUP_W:2 * GROUP_W]
    mu = _seg_sum(y, ones_bd) * inv
    yc = y - mu
    var = _seg_sum(yc * yc, ones_bd) * inv
    yn = yc * lax.rsqrt(var + GN_EPS)
    bonus = _seg_sum(r * kd * r_k, ones_bd) * v
    return (yn * gn_g + gn_b + bonus) * gate


def _pool_kernel(z_ref, w_ref, s_ref, o_ref, *, l):
    z = z_ref[0]
    row = lax.broadcasted_iota(jnp.int32, z.shape, 0)
    lane = lax.broadcasted_iota(jnp.int32, z.shape, 1)

    def shifted(x, o):
        y = pltpu.roll(x, (-o) % l, 0)
        return jnp.where((row + o >= 0) & (row + o < l), y, 0.0)

    def count(wd):
        lo = jnp.maximum(row - wd // 2, 0)
        hi = jnp.minimum(row + wd - wd // 2, l)
        return (hi - lo).astype(F32)

    s2 = shifted(z, -1) + z
    pooled = s2 / count(2)
    s4 = shifted(s2, -1) + shifted(s2, 1)
    pooled = jnp.where(lane >= 64, s4 / count(4), pooled)
    s8 = shifted(s4, -2) + shifted(s4, 2)
    pooled = jnp.where(lane >= 128, s8 / count(8), pooled)
    s16 = shifted(s8, -4) + shifted(s8, 4)
    pooled = jnp.where(lane >= 192, s16 / count(16), pooled)
    d = pooled - z
    o_ref[0] = jnp.dot(d.astype(BF16), w_ref[...], preferred_element_type=F32) * s_ref[...]


def _pool_mixer(z, w_bd, scale):
    b, l, c = z.shape
    spec = pl.BlockSpec((1, l, c), lambda bi: (bi, 0, 0))
    return pl.pallas_call(
        functools.partial(_pool_kernel, l=l),
        grid=(b,),
        in_specs=[spec, pl.BlockSpec(w_bd.shape, lambda bi: (0, 0)), pl.BlockSpec(scale.shape, lambda bi: (0, 0))],
        out_specs=spec,
        out_shape=jax.ShapeDtypeStruct((b, l, c), F32),
        compiler_params=_cparams(("parallel",), 20 * l * c * 4),
        name="pool_mixer",
    )(z, w_bd, scale)


def _gelu_tanh(x):
    return 0.5 * x * (1.0 + jnp.tanh(math.sqrt(2.0 / math.pi) * (x + 0.044715 * (x * x * x))))


def _gmlp(z, ln_g, ln_b, ws_ref, bias, ones_bd):
    inv = 1.0 / HEAD
    tm = z.shape[0]
    u = _gelu_tanh(z[:, 0:GROUP_W])
    v = _gelu_tanh(z[:, GROUP_W:2 * GROUP_W])
    mu = _seg_sum(v, ones_bd) * inv
    vc = v - mu
    var = _seg_sum(vc * vc, ones_bd) * inv
    vn = (vc * lax.rsqrt(var + LN_EPS) * ln_g + ln_b).astype(BF16)
    lane = lax.broadcasted_iota(jnp.int32, (CHUNK, GROUP_W), 1)
    out = []
    for c in range(tm // CHUNK):
        vchunk = vn[c * CHUNK:(c + 1) * CHUNK, :]
        sv = bias
        for g in range(4):
            part = jnp.dot(ws_ref[g], vchunk, preferred_element_type=F32)
            sv = sv + jnp.where((lane >= g * HEAD) & (lane < (g + 1) * HEAD), part, 0.0)
        out.append(u[c * CHUNK:(c + 1) * CHUNK, :] * sv)
    return jnp.concatenate(out, axis=0)


def _fnet_kernel(z_ref, cs_ref, cc_ref, sc_ref, w_ref, b_ref, o_ref, zcs_ref, *, l, scale):
    @pl.when(pl.program_id(1) == 0)
    def _():
        zb = z_ref[0].astype(BF16)
        zcs_ref[0:l, :] = jnp.dot(zb, cc_ref[...], preferred_element_type=F32).astype(BF16)
        zcs_ref[l:2 * l, :] = jnp.dot(zb, sc_ref[...], preferred_element_type=F32).astype(BF16)

    f = jnp.dot(cs_ref[...], zcs_ref[...], preferred_element_type=F32) * scale
    o_ref[0] = jnp.dot(f.astype(BF16), w_ref[...], preferred_element_type=F32) + b_ref[...]


def _fourier_mixer(z, cs, cc_bd, sc_bd, w_bd, bias, tm):
    b, l, c = z.shape
    full = lambda shape: pl.BlockSpec(shape, lambda bi, i: (0,) * len(shape))
    scale = 1.0 / math.sqrt(l * HEAD)
    vmem = 2 * (l * c * 4 + tm * 2 * l * 2 + tm * c * 4) + 2 * l * c * 2 + 6 * l * c * 4
    return pl.pallas_call(
        functools.partial(_fnet_kernel, l=l, scale=scale),
        grid=(b, l // tm),
        in_specs=[pl.BlockSpec((1, l, c), lambda bi, i: (bi, 0, 0)),
                  pl.BlockSpec((tm, 2 * l), lambda bi, i: (i, 0)),
                  full(cc_bd.shape), full(sc_bd.shape), full(w_bd.shape), full(bias.shape)],
        out_specs=pl.BlockSpec((1, tm, c), lambda bi, i: (bi, i, 0)),
        out_shape=jax.ShapeDtypeStruct((b, l, c), F32),
        scratch_shapes=[pltpu.VMEM((2 * l, c), BF16)],
        compiler_params=_cparams(("parallel", "arbitrary"), vmem),
        name="fourier_mixer",
    )(z, cs, cc_bd, sc_bd, w_bd, bias)


def _outproj_kernel(*refs, alpha, add_pos):
    (y_ref, r_ref, v_ref, kd_ref, gate_ref, zg_ref, p_ref, f_ref, x_ref), refs = refs[:9], refs[9:]
    if add_pos:
        pos_ref, refs = refs[0], refs[1:]
        x = x_ref[0] + pos_ref[...]
    else:
        x = x_ref[0]
    (mod_ref, rk_ref, gg_ref, gb_ref, ones_ref, lng_ref, lnb_ref, ws_ref, bias_ref, w_ref, lg_ref, lb_ref,
     o_ref) = refs
    ones_bd = ones_ref[...]
    a = _readout(y_ref[0], r_ref[0], v_ref[0], kd_ref[0], gate_ref[0], rk_ref[...], gg_ref[...], gb_ref[...], ones_bd)
    g = _gmlp(zg_ref[0], lng_ref[...], lnb_ref[...], ws_ref, bias_ref[...], ones_bd)
    mix = None
    for j, part in enumerate((a, p_ref[0], g, f_ref[0])):
        prod = jnp.dot(part.astype(BF16), w_ref[j * GROUP_W:(j + 1) * GROUP_W, :], preferred_element_type=F32)
        mix = prod if mix is None else mix + prod
    gate1 = mod_ref[0, 2:3, :]
    o_ref[0] = _layer_norm(alpha * x + gate1 * mix, lg_ref[...], lb_ref[...])


def _mix_out_proj(y2, r2, v2, kd2, gate, row0, z_gmlp, pooled, fourier, x, pos, mod, params, alpha, tm):
    b, l, d = x.shape
    add_pos = pos is not None
    off = row0 // tm
    c1 = pl.BlockSpec((1, tm, GROUP_W), lambda bi, i: (bi, i + off, 0))
    c2 = pl.BlockSpec((1, tm, 2 * GROUP_W), lambda bi, i: (bi, i + off, 0))
    s1 = pl.BlockSpec((1, tm, GROUP_W), lambda bi, i: (bi, i, 0))
    s2 = pl.BlockSpec((1, tm, 2 * GROUP_W), lambda bi, i: (bi, i, 0))
    xspec = pl.BlockSpec((1, tm, d), lambda bi, i: (bi, i, 0))
    full = lambda shape: pl.BlockSpec(shape, lambda bi, i: (0,) * len(shape))
    in_specs = [c2, c1, c1, c2, c1, s2, s1, s1, xspec]
    args = [y2, r2, v2, kd2, gate, z_gmlp, pooled, fourier, x]
    if add_pos:
        in_specs.append(pl.BlockSpec((tm, d), lambda bi, i: (i, 0)))
        args.append(pos)
    in_specs += [pl.BlockSpec((1, 6, d), lambda bi, i: (bi, 0, 0))] + [full(p.shape) for p in params]
    args += [mod, *params]
    vmem = 2 * (10 * tm * GROUP_W * 4 + 3 * tm * d * 4 + d * d * 2) + 6 * tm * d * 4 + 24 * tm * GROUP_W * 4
    return pl.pallas_call(
        functools.partial(_outproj_kernel, alpha=alpha, add_pos=add_pos),
        grid=(b, l // tm),
        in_specs=in_specs,
        out_specs=xspec,
        out_shape=jax.ShapeDtypeStruct((b, l, d), F32),
        compiler_params=_cparams(("parallel", "parallel"), vmem),
        name="mix_out_proj_ln",
    )(*args)


def _ffn_kernel(x_ref, mod_ref, wg_ref, wu_ref, w2_ref, lg_ref, lb_ref, o_ref, *, alpha, f_chunks):
    x = x_ref[0]
    sh = mod_ref[0, 3:4, :]
    sc = mod_ref[0, 4:5, :]
    h = (x * (1.0 + sc) + sh).astype(BF16)
    acc = None
    for lo, hi in f_chunks:
        gate = jnp.dot(h, wg_ref[:, lo:hi], preferred_element_type=F32)
        up = jnp.dot(h, wu_ref[:, lo:hi], preferred_element_type=F32)
        act = (_silu(gate) * up).astype(BF16)
        part = jnp.dot(act, w2_ref[lo:hi, :], preferred_element_type=F32)
        acc = part if acc is None else acc + part
    gate2 = mod_ref[0, 5:6, :]
    o_ref[0] = _layer_norm(alpha * x + gate2 * acc, lg_ref[...], lb_ref[...])


def _ffn(x, mod, w1, w2, ln_g, ln_b, alpha, tm):
    b, l, d = x.shape
    f = w2.shape[0]
    mxu_n = 256
    half = (f // mxu_n + 1) // 2 * mxu_n
    f_chunks = ((0, half), (half, f))
    xspec = pl.BlockSpec((1, tm, d), lambda bi, i: (bi, i, 0))
    full = lambda shape: pl.BlockSpec(shape, lambda bi, i: (0,) * len(shape))
    once = pl.Buffered(1)
    vmem = 4 * tm * d * 4 + 3 * d * f * 2 + 3 * tm * d * 4 + 4 * tm * half * 4
    return pl.pallas_call(
        functools.partial(_ffn_kernel, alpha=alpha, f_chunks=f_chunks),
        grid=(b, l // tm),
        in_specs=[xspec,
                  pl.BlockSpec((1, 6, d), lambda bi, i: (bi, 0, 0)),
                  pl.BlockSpec((d, f), lambda bi, i: (0, 0), pipeline_mode=once),
                  pl.BlockSpec((d, f), lambda bi, i: (0, 1), pipeline_mode=once),
                  pl.BlockSpec((f, d), lambda bi, i: (0, 0), pipeline_mode=once),
                  full(ln_g.shape), full(ln_b.shape)],
        out_specs=xspec,
        out_shape=jax.ShapeDtypeStruct((b, l, d), F32),
        compiler_params=_cparams(("parallel", "parallel"), vmem),
        name="ffn_ln",
    )(x, mod, w1, w1, w2, ln_g, ln_b)


def _pos_embed(n_tok, dim):
    rows = n_tok // GRID_W
    row, col = jnp.meshgrid(jnp.arange(rows, dtype=F32), jnp.arange(GRID_W, dtype=F32), indexing='ij')
    quarter = dim // 4
    freqs = jnp.exp(-math.log(10000.0) * jnp.arange(quarter, dtype=F32) / quarter)

    def enc(p):
        ang = p.reshape(-1, 1) * freqs[None, :]
        return jnp.concatenate([jnp.sin(ang), jnp.cos(ang)], -1)

    return jnp.concatenate([enc(row), enc(col)], -1)


def _block_diag(blocks):
    g, n, m = blocks.shape
    eye = jnp.eye(g, dtype=blocks.dtype)
    return (eye[:, None, :, None] * blocks[:, :, None, :]).reshape(g * n, g * m)


def _dft_tables(n):
    def direct(rows, cols, period):
        ang = ((rows[:, None] * cols[None, :]) % period).astype(F32) * (2.0 * math.pi / period)
        return jnp.cos(ang), jnp.sin(ang)

    idx = jnp.arange(n, dtype=jnp.int32)
    if n <= 1024:
        return direct(idx, idx, n)
    lo = 64
    ca, sa = direct(jnp.arange(n // lo, dtype=jnp.int32), idx, n // lo)
    cb, sb = direct(jnp.arange(lo, dtype=jnp.int32), idx, n)
    ca, sa, cb, sb = ca[:, None, :], sa[:, None, :], cb[None], sb[None]
    return (ca * cb - sa * sb).reshape(n, n), (sa * cb + ca * sb).reshape(n, n)


def kernel(x, c, ctx, c_ctx, w_mod, b_mod, w_in, rkv_conv, decay_w0, decay_w2, iclr_a0, iclr_a2,
           gate_g2, k_k, k_a, r_k, gn_g, gn_b, pool_w, pool_scale, gmlp_ln_g, gmlp_ln_b, gmlp_ws,
           gmlp_bs, fnet_w, fnet_b, w_out, ln1_g, ln1_b, ln2_g, ln2_b, ffn_w1, ffn_w2):
    bsz, seq, d = x.shape
    ctx_len = ctx.shape[1]
    depth = w_in.shape[0]
    d_ff = ffn_w2.shape[1]
    alpha = (2 * depth) ** 0.25
    rwkv_cols = 3 * GROUP_W + LORA_W
    assert bsz * N_HEADS * 2 == V7X_LANES, "scan layout packs (direction, batch, head) on the 128 lanes"

    pos = _pos_embed(seq, d)
    ones_bd = _block_diag(jnp.ones((N_HEADS, HEAD, HEAD), F32)).astype(BF16)
    cc64, ss64 = _dft_tables(HEAD)
    cc_bd = _block_diag(jnp.tile(cc64[None], (4, 1, 1))).astype(BF16)
    sc_bd = _block_diag(jnp.tile(ss64[None], (4, 1, 1))).astype(BF16)

    def dft_rows(n):
        cn, sn = _dft_tables(n)
        return jnp.concatenate([cn, -sn], axis=1).astype(BF16)

    cs_x = dft_rows(seq)
    cs_c = dft_rows(ctx_len)

    pad_rows = (-(bsz + 1)) % V7X_SUBLANES
    c_all = jnp.concatenate([c, c_ctx[None], jnp.zeros((pad_rows, d), F32)], axis=0)

    tm_x = 512
    tm_c = ctx_len
    tb = 64

    xs, cs = x, ctx
    for l in range(depth):
        last = l == depth - 1
        first = l == 0
        m = _modulation(c_all, w_mod[l].astype(BF16), b_mod[l])
        mod_x = m[:bsz].reshape(bsz, 6, d)
        mod_c = jnp.broadcast_to(m[bsz].reshape(1, 6, d), (bsz, 6, d))

        w_in_l = w_in[l].astype(BF16)
        widths_full = (3 * GROUP_W, LORA_W, GROUP_W, 2 * GROUP_W, GROUP_W)
        px = _in_proj(xs, pos if first else None, mod_x, w_in_l, widths_full, tm_x)
        if last:
            pc = _in_proj(cs, None, mod_c, w_in_l[:, :rwkv_cols], widths_full[:2], tm_c)
        else:
            pc = _in_proj(cs, None, mod_c, w_in_l, widths_full, tm_c)

        wl = jnp.zeros((LORA_W, 5 * GROUP_W), F32)
        wl = wl.at[0:64, 0:256].set(decay_w2[l, 0]).at[64:128, 256:512].set(decay_w2[l, 1])
        wl = wl.at[128:192, 512:768].set(iclr_a2[l, 0]).at[192:256, 768:1024].set(iclr_a2[l, 1])
        wl = wl.at[256:384, 1024:1280].set(gate_g2[l]).astype(BF16)
        prep_args = (rkv_conv[l], wl, decay_w0[l], iclr_a0[l], k_k[l].reshape(1, -1), k_a[l].reshape(1, -1),
                     ones_bd)
        p_r, p_v, p_kk, p_gate, p_w2, p_b2, p_kd2 = _rwkv_prep(px[0], px[1], pc[0], pc[1], *prep_args)
        to_scan = functools.partial(_to_scan_layout, n_lat=seq, n_ctx=ctx_len)
        y_s = _rwkv_scan(to_scan(p_r), to_scan(p_v, step_major=True), to_scan(p_kk), to_scan(p_w2), to_scan(p_b2),
                         to_scan(p_kd2), tb)
        y2 = _from_scan_layout(y_s, bsz, seq, ctx_len)

        pool_bd = _block_diag(pool_w[l]).astype(BF16)
        fnet_bd = _block_diag(fnet_w[l]).astype(BF16)
        ws_b = gmlp_ws[l].astype(BF16)
        bias_tile = jnp.repeat(gmlp_bs[l].T, HEAD, axis=1)
        row = lambda a: a.reshape(1, -1)

        mix_params = (row(r_k[l]), row(gn_g[l]), row(gn_b[l]), ones_bd, row(gmlp_ln_g[l]), row(gmlp_ln_b[l]), ws_b,
                      bias_tile, w_out[l].astype(BF16), row(ln1_g[l]), row(ln1_b[l]))

        def mix_sublayer(parts, row0, stream, pos_tab, mod, cs_tab, tm, tmf):
            pooled = _pool_mixer(parts[2], pool_bd, row(pool_scale[l]))
            fourier = _fourier_mixer(parts[4], cs_tab, cc_bd, sc_bd, fnet_bd, row(fnet_b[l]), tmf)
            return _mix_out_proj(y2, p_r, p_v, p_kd2, p_gate, row0, parts[3], pooled, fourier, stream, pos_tab, mod,
                                 mix_params, alpha, tm)

        w1_l = ffn_w1[l].astype(BF16)
        w2_l = ffn_w2[l].astype(BF16)
        xs = mix_sublayer(px, 0, xs, pos if first else None, mod_x, cs_x, tm_x, 256)
        xs = _ffn(xs, mod_x, w1_l, w2_l, row(ln2_g[l]), row(ln2_b[l]), alpha, tm_x)
        if not last:
            cs = mix_sublayer(pc, seq, cs, None, mod_c, cs_c, tm_c, tm_c)
            cs = _ffn(cs, mod_c, w1_l, w2_l, row(ln2_g[l]), row(ln2_b[l]), alpha, tm_c)
    return xs
```

```python
import functools
import math

import jax
import jax.numpy as jnp
from jax import lax
from jax.experimental import pallas as pl
from jax.experimental.pallas import tpu as pltpu

F32 = jnp.float32
BF16 = jnp.bfloat16

V7X_LANES = 128
V7X_SUBLANES = 8
V7X_VMEM_BYTES = 64 * 1024 * 1024
V7X_VMEM_LIMIT_CAP = 56 * 1024 * 1024

GRID_W = 64
HEAD = 64
GROUP_W = 256
N_HEADS = GROUP_W // HEAD
LORA_W = 384
POOL_WINDOWS = (2, 4, 8, 16)
CHUNK = 128
LN_EPS = 1e-5
GN_EPS = 64e-5


def _cparams(sem, vmem_bytes):
    limit = int(min(max(vmem_bytes, 16 * 1024 * 1024), V7X_VMEM_LIMIT_CAP))
    return pltpu.CompilerParams(dimension_semantics=sem, vmem_limit_bytes=limit)


def _silu(x):
    return x * jax.nn.sigmoid(x)


def _seg_sum(x, ones_bd):
    hi = x.astype(BF16)
    lo = (x - hi.astype(F32)).astype(BF16)
    return (jnp.dot(hi, ones_bd, preferred_element_type=F32)
            + jnp.dot(lo, ones_bd, preferred_element_type=F32))


def _layer_norm(z, g, b):
    mu = jnp.mean(z, axis=-1, keepdims=True)
    zc = z - mu
    var = jnp.mean(zc * zc, axis=-1, keepdims=True)
    return zc * lax.rsqrt(var + LN_EPS) * g + b


def _mod_kernel(c_ref, w_ref, b_ref, o_ref):
    s = _silu(c_ref[...])
    o_ref[...] = jnp.dot(s.astype(BF16), w_ref[...], preferred_element_type=F32) + b_ref[...]


def _modulation(c_all, w_mod, b_mod):
    rows, d = c_all.shape
    n = w_mod.shape[1]
    tn = 1024
    return pl.pallas_call(
        _mod_kernel,
        grid=(n // tn,),
        in_specs=[pl.BlockSpec((rows, d), lambda j: (0, 0)),
                  pl.BlockSpec((d, tn), lambda j: (0, j)),
                  pl.BlockSpec((1, tn), lambda j: (0, j))],
        out_specs=pl.BlockSpec((rows, tn), lambda j: (0, j)),
        out_shape=jax.ShapeDtypeStruct((rows, n), F32),
        compiler_params=_cparams(("parallel",), 4 * d * tn * 2),
        name="modulation",
    )(c_all, w_mod, b_mod.reshape(1, n))


def _inproj_kernel(*refs, widths, add_pos):
    if add_pos:
        x_ref, pos_ref, mod_ref, w_ref = refs[:4]
        outs = refs[4:]
        x = x_ref[0] + pos_ref[...]
    else:
        x_ref, mod_ref, w_ref = refs[:3]
        outs = refs[3:]
        x = x_ref[0]
    sh = mod_ref[0, 0:1, :]
    sc = mod_ref[0, 1:2, :]
    h = (x * (1.0 + sc) + sh).astype(BF16)
    off = 0
    for o_ref, wd in zip(outs, widths):
        o_ref[0] = jnp.dot(h, w_ref[:, off:off + wd], preferred_element_type=F32)
        off += wd


def _in_proj(x, pos, mod, w, widths, tm):
    b, l, d = x.shape
    n = w.shape[1]
    assert sum(widths) == n
    add_pos = pos is not None
    in_specs = [pl.BlockSpec((1, tm, d), lambda bi, i: (bi, i, 0))]
    args = [x]
    if add_pos:
        in_specs.append(pl.BlockSpec((tm, d), lambda bi, i: (i, 0)))
        args.append(pos)
    in_specs += [pl.BlockSpec((1, 6, d), lambda bi, i: (bi, 0, 0)),
                 pl.BlockSpec((d, n), lambda bi, i: (0, 0))]
    args += [mod, w]
    vmem = 2 * (2 * tm * d * 4 + d * n * 2 + tm * n * 4) + tm * d * 8
    return pl.pallas_call(
        functools.partial(_inproj_kernel, widths=widths, add_pos=add_pos),
        grid=(b, l // tm),
        in_specs=in_specs,
        out_specs=[pl.BlockSpec((1, tm, wd), lambda bi, i: (bi, i, 0)) for wd in widths],
        out_shape=[jax.ShapeDtypeStruct((b, l, wd), F32) for wd in widths],
        compiler_params=_cparams(("parallel", "parallel"), vmem),
        name="in_proj",
    )(*args)


def _prep_kernel(zx_ref, zp_ref, zn_ref, zc_ref, lox_ref, loc_ref, conv_ref, wl_ref, w0_ref, a0_ref, kk_ref,
                 ka_ref, ones_ref, r_o, v_o, kk_o, gate_o, w_o, b_o, kd_o, *, tm, nx):
    i = pl.program_id(1)
    is_ctx = i >= nx
    z = jnp.where(is_ctx, zc_ref[0], zx_ref[0])
    lo = jnp.where(is_ctx, loc_ref[0], lox_ref[0])
    row = lax.broadcasted_iota(jnp.int32, z.shape, 0)
    has_prev = jnp.logical_and(i > 0, i < nx).astype(F32)
    has_next = (i < nx - 1).astype(F32)
    prev_row = zp_ref[0, V7X_SUBLANES - 1:V7X_SUBLANES, :] * has_prev
    next_row = zn_ref[0, 0:1, :] * has_next
    zm1 = jnp.where(row == 0, prev_row, pltpu.roll(z, 1, 0))
    zp1 = jnp.where(row == tm - 1, next_row, pltpu.roll(z, tm - 1, 0))
    c = zm1 * conv_ref[0:1, :] + z * conv_ref[1:2, :] + zp1 * conv_ref[2:3, :]
    r = c[:, 0:GROUP_W]
    k = c[:, GROUP_W:2 * GROUP_W]
    v = c[:, 2 * GROUP_W:3 * GROUP_W]

    col = lax.broadcasted_iota(jnp.int32, lo.shape, 1)
    act = jnp.where(col < 128, jnp.tanh(lo), jnp.where(col < 256, lo, jax.nn.sigmoid(lo)))
    pre = jnp.dot(act.astype(BF16), wl_ref[...], preferred_element_type=F32)

    ones_bd = ones_ref[...]
    kk = k * kk_ref[...]
    nrm = jnp.sqrt(_seg_sum(kk * kk, ones_bd))
    kk = kk / jnp.maximum(nrm, 1e-12)
    r_o[0] = r
    v_o[0] = v
    kk_o[0] = kk
    gate_o[0] = pre[:, 4 * GROUP_W:5 * GROUP_W]
    ka = ka_ref[...]
    for d in range(2):
        cols = slice(d * GROUP_W, (d + 1) * GROUP_W)
        xw = w0_ref[d:d + 1, :] + pre[:, cols]
        w_o[0, :, cols] = -math.exp(-0.5) * jax.nn.sigmoid(xw)
        a = jax.nn.sigmoid(a0_ref[d:d + 1, :] + pre[:, (2 + d) * GROUP_W:(3 + d) * GROUP_W])
        kd_o[0, :, cols] = k * (1.0 + (a - 1.0) * ka)
        b_o[0, :, cols] = kk * a


def _rwkv_prep(zx_rkv, zx_lora, zc_rkv, zc_lora, conv, wl, w0, a0, k_k, k_a, ones_bd):
    b, l, c3 = zx_rkv.shape
    tm = zc_rkv.shape[1]
    nx = l // tm
    nblk8 = l // V7X_SUBLANES
    r8 = tm // V7X_SUBLANES
    full = lambda shape: pl.BlockSpec(shape, lambda bi, i: (0,) * len(shape))
    xi = lambda i: jnp.minimum(i, nx - 1)
    in_specs = [
        pl.BlockSpec((1, tm, c3), lambda bi, i: (bi, xi(i), 0)),
        pl.BlockSpec((1, V7X_SUBLANES, c3), lambda bi, i: (bi, jnp.clip(i * r8 - 1, 0, nblk8 - 1), 0)),
        pl.BlockSpec((1, V7X_SUBLANES, c3), lambda bi, i: (bi, jnp.minimum((i + 1) * r8, nblk8 - 1), 0)),
        pl.BlockSpec((1, tm, c3), lambda bi, i: (bi, 0, 0)),
        pl.BlockSpec((1, tm, LORA_W), lambda bi, i: (bi, xi(i), 0)),
        pl.BlockSpec((1, tm, LORA_W), lambda bi, i: (bi, 0, 0)),
        full(conv.shape), full(wl.shape), full(w0.shape), full(a0.shape),
        full(k_k.shape), full(k_a.shape), full(ones_bd.shape),
    ]
    spec1 = pl.BlockSpec((1, tm, GROUP_W), lambda bi, i: (bi, i, 0))
    spec2 = pl.BlockSpec((1, tm, 2 * GROUP_W), lambda bi, i: (bi, i, 0))
    shp1 = jax.ShapeDtypeStruct((b, l + tm, GROUP_W), F32)
    shp2 = jax.ShapeDtypeStruct((b, l + tm, 2 * GROUP_W), F32)
    vmem = 2 * (2 * tm * (c3 + LORA_W) * 4 + 13 * tm * GROUP_W * 4) + 14 * tm * c3 * 4
    return pl.pallas_call(
        functools.partial(_prep_kernel, tm=tm, nx=nx),
        grid=(b, nx + 1),
        in_specs=in_specs,
        out_specs=[spec1] * 4 + [spec2] * 3,
        out_shape=[shp1] * 4 + [shp2] * 3,
        compiler_params=_cparams(("parallel", "arbitrary"), vmem),
        name="rwkv_prep",
    )(zx_rkv, zx_rkv, zx_rkv, zc_rkv, zx_lora, zc_lora, conv, wl, w0, a0, k_k, k_a, ones_bd)


SCAN_ROW_PITCH = 72


def _to_scan_kernel(xf_ref, xb_ref, o_ref, y_ref, *, nb, tt, step_major):
    rev = (tt - 1) - lax.broadcasted_iota(jnp.int32, (GROUP_W, tt), 1)
    for b in range(nb):
        xf = xf_ref[b].T
        xb = jnp.take_along_axis(xb_ref[b].T, rev, axis=1)
        for h in range(N_HEADS):
            y_ref[pl.ds((b * 2 * N_HEADS + h) * SCAN_ROW_PITCH, HEAD), :] = xf[h * HEAD:(h + 1) * HEAD, :]
            y_ref[pl.ds((b * 2 * N_HEADS + N_HEADS + h) * SCAN_ROW_PITCH, HEAD), :] = xb[h * HEAD:(h + 1) * HEAD, :]
    for k in range(HEAD):
        z = y_ref[pl.ds(k, nb * 2 * N_HEADS, stride=SCAN_ROW_PITCH), :]
        if step_major:
            o_ref[pl.ds(k, tt, stride=HEAD), :] = z.T
        else:
            o_ref[k] = z.T


def _to_scan_layout(x, n_lat, n_ctx, step_major=False):
    nb, t, c = x.shape
    tt = V7X_LANES
    assert nb * 2 * N_HEADS == V7X_LANES and c in (GROUP_W, 2 * GROUP_W) and n_lat % tt == 0 and n_ctx % tt == 0
    nl, nc = n_lat // tt, n_ctx // tt
    n = nl + nc
    bwd_cols = c // GROUP_W - 1
    if step_major:
        out_spec = pl.BlockSpec((tt * HEAD, V7X_LANES), lambda j: (j, 0))
        out_shape = jax.ShapeDtypeStruct((t * HEAD, V7X_LANES), F32)
    else:
        out_spec = pl.BlockSpec((HEAD, tt, V7X_LANES), lambda j: (0, j, 0))
        out_shape = jax.ShapeDtypeStruct((HEAD, t, V7X_LANES), F32)
    out = pl.pallas_call(
        functools.partial(_to_scan_kernel, nb=nb, tt=tt, step_major=step_major),
        grid=(n,),
        in_specs=[pl.BlockSpec((nb, tt, GROUP_W), lambda j: (0, jnp.where(j < nc, nl + j, j - nc), 0)),
                  pl.BlockSpec((nb, tt, GROUP_W), lambda j: (0, n - 1 - j, bwd_cols))],
        out_specs=out_spec,
        out_shape=out_shape,
        scratch_shapes=[pltpu.VMEM((V7X_LANES * SCAN_ROW_PITCH, tt), F32)],
        compiler_params=_cparams(("parallel",), 2 * (nb * tt * 2 * GROUP_W * 4 + HEAD * tt * V7X_LANES * 4)
                                 + 3 * V7X_LANES * SCAN_ROW_PITCH * tt * 4),
        name="to_scan_layout",
    )(x, x)
    return out.reshape(t, HEAD, V7X_LANES) if step_major else out


def _from_scan_kernel(yf_ref, yb_ref, o_ref, s_ref, *, nb, tt):
    rev = (tt - 1) - lax.broadcasted_iota(jnp.int32, (V7X_LANES, tt), 1)
    fwd_row = (lax.broadcasted_iota(jnp.int32, (V7X_LANES, tt), 0) & N_HEADS) == 0
    for v in range(HEAD):
        zf = yf_ref[pl.ds(v, tt, stride=SCAN_ROW_PITCH), :].T
        zb = jnp.take_along_axis(yb_ref[pl.ds(v, tt, stride=SCAN_ROW_PITCH), :].T, rev, axis=1)
        s_ref[pl.ds(v, V7X_LANES, stride=SCAN_ROW_PITCH), :] = jnp.where(fwd_row, zf, zb)
    groups = 2 * N_HEADS
    for b in range(nb):
        rows = [s_ref[pl.ds((b * groups + g) * SCAN_ROW_PITCH, HEAD), :] for g in range(groups)]
        o_ref[b] = jnp.concatenate(rows, axis=0).T


def _from_scan_layout(y, nb, n_lat, n_ctx):
    t, pitch, _ = y.shape
    assert pitch == SCAN_ROW_PITCH
    tt = V7X_LANES
    nl, nc = n_lat // tt, n_ctx // tt
    n = nl + nc
    y2d = y.reshape(t * pitch, V7X_LANES)
    blk = pl.BlockSpec((tt * pitch, V7X_LANES), lambda j: (jnp.where(j < nl, j + nc, j - nl), 0))
    rblk = pl.BlockSpec((tt * pitch, V7X_LANES), lambda j: (n - 1 - j, 0))
    return pl.pallas_call(
        functools.partial(_from_scan_kernel, nb=nb, tt=tt),
        grid=(n,),
        in_specs=[blk, rblk],
        out_specs=pl.BlockSpec((nb, tt, 2 * GROUP_W), lambda j: (0, j, 0)),
        out_shape=jax.ShapeDtypeStruct((nb, t, 2 * GROUP_W), F32),
        scratch_shapes=[pltpu.VMEM((V7X_LANES * SCAN_ROW_PITCH, tt), F32)],
        compiler_params=_cparams(("parallel",), 2 * (2 * tt * pitch * V7X_LANES * 4 + nb * tt * 2 * GROUP_W * 4)
                                 + 3 * V7X_LANES * SCAN_ROW_PITCH * tt * 4),
        name="from_scan_layout",
    )(y2d, y2d)


SCAN_SLAB_PAD = 8


def _scan_kernel(r_ref, v_ref, kk_ref, lw_ref, b_ref, kd_ref, y_ref, h_ref, s_ref, g_ref, *, tb):
    @pl.when(pl.program_id(0) == 0)
    def _():
        h_ref[...] = jnp.zeros_like(h_ref)

    tile = (V7X_SUBLANES, V7X_LANES)
    R_G, KK_G, B_G, KD_G = range(4)

    sub = lax.broadcasted_iota(jnp.int32, tile, 0)

    def scale(k, carry):
        lw = lw_ref[k]
        parts = []
        total = jnp.zeros(tile, F32)
        for j in range(tb // V7X_SUBLANES):
            x = lw[j * V7X_SUBLANES:(j + 1) * V7X_SUBLANES, :]
            for s in (1, 2, 4):
                x = x + jnp.where(sub >= s, pltpu.roll(x, s, 0), 0.0)
            x = x + total
            total = jnp.broadcast_to(x[V7X_SUBLANES - 1:V7X_SUBLANES, :], tile)
            parts.append(x)
        cum = jnp.concatenate(parts, axis=0)
        g = jnp.exp(cum)
        g_inv = jnp.exp(-cum)
        s_ref[R_G, k, 0:tb] = r_ref[k] * g
        s_ref[KK_G, k, 0:tb] = kk_ref[k] * jnp.exp(cum - lw)
        s_ref[B_G, k, 0:tb] = b_ref[k] * g_inv
        s_ref[KD_G, k, 0:tb] = kd_ref[k] * g_inv
        g_ref[pl.ds(k, 1), :] = g[tb - 1:tb, :]
        return carry

    lax.fori_loop(0, HEAD, scale, 0)

    u0 = jnp.zeros((HEAD, V7X_LANES), F32)
    for k in range(HEAD):
        u0 = u0 + s_ref[KK_G, k, 0:1, :] * h_ref[k]

    def step(t, u):
        tn = jnp.minimum(t + 1, tb - 1)
        vv = v_ref[t]
        y = jnp.zeros((HEAD, V7X_LANES), F32)
        un = jnp.zeros((HEAD, V7X_LANES), F32)
        for k in range(HEAD):
            hn = h_ref[k] - s_ref[B_G, k, pl.ds(t, 1), :] * u + s_ref[KD_G, k, pl.ds(t, 1), :] * vv
            h_ref[k] = hn
            y = y + s_ref[R_G, k, pl.ds(t, 1), :] * hn
            un = un + s_ref[KK_G, k, pl.ds(tn, 1), :] * hn
        y_ref[t, 0:HEAD] = y
        return un

    y_ref[:, HEAD:, :] = jnp.zeros((tb, SCAN_ROW_PITCH - HEAD, V7X_LANES), F32)
    lax.fori_loop(0, tb, step, u0)
    for k in range(HEAD):
        h_ref[k] = h_ref[k] * g_ref[k:k + 1, :]


def _rwkv_scan(r, v, kk, w, b, kd, tb):
    t_tot = r.shape[1]
    kspec = pl.BlockSpec((HEAD, tb, V7X_LANES), lambda i: (0, i, 0))
    sspec = pl.BlockSpec((tb, HEAD, V7X_LANES), lambda i: (i, 0, 0))
    blk = tb * HEAD * V7X_LANES * 4
    return pl.pallas_call(
        functools.partial(_scan_kernel, tb=tb),
        grid=(t_tot // tb,),
        in_specs=[kspec, sspec, kspec, kspec, kspec, kspec],
        out_specs=pl.BlockSpec((tb, SCAN_ROW_PITCH, V7X_LANES), lambda i: (i, 0, 0)),
        out_shape=jax.ShapeDtypeStruct((t_tot, SCAN_ROW_PITCH, V7X_LANES), F32),
        scratch_shapes=[pltpu.VMEM((HEAD, HEAD, V7X_LANES), F32),
                        pltpu.VMEM((4, HEAD, tb + SCAN_SLAB_PAD, V7X_LANES), F32),
                        pltpu.VMEM((HEAD, V7X_LANES), F32)],
        compiler_params=_cparams(("arbitrary",), 14 * blk + 5 * blk + 3 * HEAD * HEAD * V7X_LANES * 4),
        name="rwkv_scan",
    )(r, v, kk, w, b, kd)


def _readout(y2, r, v, kd2, gate, r_k, gn_g, gn_b, ones_bd):
    inv = 1.0 / HEAD
    y = y2[:, 0:GROUP_W] + y2[:, GROUP_W:2 * GROUP_W]
    kd = kd2[:, 0:GROUP_W] + kd2[:, GROUP_W:2 * GROUP_W]
    mu = _seg_sum(y, ones_bd) * inv
    yc = y - mu
    var = _seg_sum(yc * yc, ones_bd) * inv
    yn = yc * lax.rsqrt(var + GN_EPS)
    bonus = _seg_sum(r * kd * r_k, ones_bd) * v
    return (yn * gn_g + gn_b + bonus) * gate


def _pool_kernel(z_ref, w_ref, s_ref, o_ref, *, l):
    z = z_ref[0]
    row = lax.broadcasted_iota(jnp.int32, z.shape, 0)
    lane = lax.broadcasted_iota(jnp.int32, z.shape, 1)

    def shifted(x, o):
        y = pltpu.roll(x, (-o) % l, 0)
        return jnp.where((row + o >= 0) & (row + o < l), y, 0.0)

    def count(wd):
        lo = jnp.maximum(row - wd // 2, 0)
        hi = jnp.minimum(row + wd - wd // 2, l)
        return (hi - lo).astype(F32)

    s2 = shifted(z, -1) + z
    pooled = s2 / count(2)
    s4 = shifted(s2, -1) + shifted(s2, 1)
    pooled = jnp.where(lane >= 64, s4 / count(4), pooled)
    s8 = shifted(s4, -2) + shifted(s4, 2)
    pooled = jnp.where(lane >= 128, s8 / count(8), pooled)
    s16 = shifted(s8, -4) + shifted(s8, 4)
    pooled = jnp.where(lane >= 192, s16 / count(16), pooled)
    d = pooled - z
    o_ref[0] = jnp.dot(d.astype(BF16), w_ref[...], preferred_element_type=F32) * s_ref[...]


def _pool_mixer(z, w_bd, scale):
    b, l, c = z.shape
    spec = pl.BlockSpec((1, l, c), lambda bi: (bi, 0, 0))
    return pl.pallas_call(
        functools.partial(_pool_kernel, l=l),
        grid=(b,),
        in_specs=[spec, pl.BlockSpec(w_bd.shape, lambda bi: (0, 0)), pl.BlockSpec(scale.shape, lambda bi: (0, 0))],
        out_specs=spec,
        out_shape=jax.ShapeDtypeStruct((b, l, c), F32),
        compiler_params=_cparams(("parallel",), 20 * l * c * 4),
        name="pool_mixer",
    )(z, w_bd, scale)


def _gelu_tanh(x):
    return 0.5 * x * (1.0 + jnp.tanh(math.sqrt(2.0 / math.pi) * (x + 0.044715 * (x * x * x))))


def _gmlp(z, ln_g, ln_b, ws_ref, bias, ones_bd):
    inv = 1.0 / HEAD
    tm = z.shape[0]
    u = _gelu_tanh(z[:, 0:GROUP_W])
    v = _gelu_tanh(z[:, GROUP_W:2 * GROUP_W])
    mu = _seg_sum(v, ones_bd) * inv
    vc = v - mu
    var = _seg_sum(vc * vc, ones_bd) * inv
    vn = (vc * lax.rsqrt(var + LN_EPS) * ln_g + ln_b).astype(BF16)
    lane = lax.broadcasted_iota(jnp.int32, (CHUNK, GROUP_W), 1)
    out = []
    for c in range(tm // CHUNK):
        vchunk = vn[c * CHUNK:(c + 1) * CHUNK, :]
        sv = bias
        for g in range(4):
            part = jnp.dot(ws_ref[g], vchunk, preferred_element_type=F32)
            sv = sv + jnp.where((lane >= g * HEAD) & (lane < (g + 1) * HEAD), part, 0.0)
        out.append(u[c * CHUNK:(c + 1) * CHUNK, :] * sv)
    return jnp.concatenate(out, axis=0)


def _fnet_kernel(z_ref, cs_ref, cc_ref, sc_ref, w_ref, b_ref, o_ref, zcs_ref, *, l, scale, nbg):
    c = z_ref.shape[2]

    @pl.when(pl.program_id(1) == 0)
    def _():
        for j in range(nbg):
            zb = z_ref[j].astype(BF16)
            zcs_ref[0:l, j * c:(j + 1) * c] = jnp.dot(zb, cc_ref[...], preferred_element_type=F32).astype(BF16)
            zcs_ref[l:2 * l, j * c:(j + 1) * c] = jnp.dot(zb, sc_ref[...], preferred_element_type=F32).astype(BF16)

    f = jnp.dot(cs_ref[...], zcs_ref[...], preferred_element_type=F32) * scale
    for j in range(nbg):
        o_ref[j] = jnp.dot(f[:, j * c:(j + 1) * c].astype(BF16), w_ref[...], preferred_element_type=F32) + b_ref[...]


def _fourier_mixer(z, cs, cc_bd, sc_bd, w_bd, bias, tm, nbg=4):
    b, l, c = z.shape
    assert b % nbg == 0
    full = lambda shape: pl.BlockSpec(shape, lambda bi, i: (0,) * len(shape))
    scale = 1.0 / math.sqrt(l * HEAD)
    vmem = 2 * (nbg * l * c * 4 + tm * 2 * l * 2 + nbg * tm * c * 4) + 2 * l * nbg * c * 2 + 4 * l * c * 4
    return pl.pallas_call(
        functools.partial(_fnet_kernel, l=l, scale=scale, nbg=nbg),
        grid=(b // nbg, l // tm),
        in_specs=[pl.BlockSpec((nbg, l, c), lambda bi, i: (bi, 0, 0)),
                  pl.BlockSpec((tm, 2 * l), lambda bi, i: (i, 0)),
                  full(cc_bd.shape), full(sc_bd.shape), full(w_bd.shape), full(bias.shape)],
        out_specs=pl.BlockSpec((nbg, tm, c), lambda bi, i: (bi, i, 0)),
        out_shape=jax.ShapeDtypeStruct((b, l, c), F32),
        scratch_shapes=[pltpu.VMEM((2 * l, nbg * c), BF16)],
        compiler_params=_cparams(("parallel", "arbitrary"), vmem),
        name="fourier_mixer",
    )(z, cs, cc_bd, sc_bd, w_bd, bias)


def _outproj_kernel(*refs, alpha, add_pos):
    (y_ref, r_ref, v_ref, kd_ref, gate_ref, zg_ref, p_ref, f_ref, x_ref), refs = refs[:9], refs[9:]
    if add_pos:
        pos_ref, refs = refs[0], refs[1:]
        x = x_ref[0] + pos_ref[...]
    else:
        x = x_ref[0]
    (mod_ref, rk_ref, gg_ref, gb_ref, ones_ref, lng_ref, lnb_ref, ws_ref, bias_ref, w_ref, lg_ref, lb_ref,
     o_ref) = refs
    ones_bd = ones_ref[...]
    a = _readout(y_ref[0], r_ref[0], v_ref[0], kd_ref[0], gate_ref[0], rk_ref[...], gg_ref[...], gb_ref[...], ones_bd)
    g = _gmlp(zg_ref[0], lng_ref[...], lnb_ref[...], ws_ref, bias_ref[...], ones_bd)
    mix = None
    for j, part in enumerate((a, p_ref[0], g, f_ref[0])):
        prod = jnp.dot(part.astype(BF16), w_ref[j * GROUP_W:(j + 1) * GROUP_W, :], preferred_element_type=F32)
        mix = prod if mix is None else mix + prod
    gate1 = mod_ref[0, 2:3, :]
    o_ref[0] = _layer_norm(alpha * x + gate1 * mix, lg_ref[...], lb_ref[...])


def _mix_out_proj(y2, r2, v2, kd2, gate, row0, z_gmlp, pooled, fourier, x, pos, mod, params, alpha, tm):
    b, l, d = x.shape
    add_pos = pos is not None
    off = row0 // tm
    c1 = pl.BlockSpec((1, tm, GROUP_W), lambda bi, i: (bi, i + off, 0))
    c2 = pl.BlockSpec((1, tm, 2 * GROUP_W), lambda bi, i: (bi, i + off, 0))
    s1 = pl.BlockSpec((1, tm, GROUP_W), lambda bi, i: (bi, i, 0))
    s2 = pl.BlockSpec((1, tm, 2 * GROUP_W), lambda bi, i: (bi, i, 0))
    xspec = pl.BlockSpec((1, tm, d), lambda bi, i: (bi, i, 0))
    full = lambda shape: pl.BlockSpec(shape, lambda bi, i: (0,) * len(shape))
    in_specs = [c2, c1, c1, c2, c1, s2, s1, s1, xspec]
    args = [y2, r2, v2, kd2, gate, z_gmlp, pooled, fourier, x]
    if add_pos:
        in_specs.append(pl.BlockSpec((tm, d), lambda bi, i: (i, 0)))
        args.append(pos)
    in_specs += [pl.BlockSpec((1, 6, d), lambda bi, i: (bi, 0, 0))] + [full(p.shape) for p in params]
    args += [mod, *params]
    vmem = 2 * (10 * tm * GROUP_W * 4 + 3 * tm * d * 4 + d * d * 2) + 6 * tm * d * 4 + 24 * tm * GROUP_W * 4
    return pl.pallas_call(
        functools.partial(_outproj_kernel, alpha=alpha, add_pos=add_pos),
        grid=(b, l // tm),
        in_specs=in_specs,
        out_specs=xspec,
        out_shape=jax.ShapeDtypeStruct((b, l, d), F32),
        compiler_params=_cparams(("parallel", "parallel"), vmem),
        name="mix_out_proj_ln",
    )(*args)


def _ffn_kernel(x_ref, mod_ref, wg_ref, wu_ref, w2_ref, lg_ref, lb_ref, o_ref, *, alpha, f_chunks):
    x = x_ref[0]
    sh = mod_ref[0, 3:4, :]
    sc = mod_ref[0, 4:5, :]
    h = (x * (1.0 + sc) + sh).astype(BF16)
    acc = None
    for lo, hi in f_chunks:
        gate = jnp.dot(h, wg_ref[:, lo:hi], preferred_element_type=F32)
        up = jnp.dot(h, wu_ref[:, lo:hi], preferred_element_type=F32)
        act = (_silu(gate) * up).astype(BF16)
        part = jnp.dot(act, w2_ref[lo:hi, :], preferred_element_type=F32)
        acc = part if acc is None else acc + part
    gate2 = mod_ref[0, 5:6, :]
    o_ref[0] = _layer_norm(alpha * x + gate2 * acc, lg_ref[...], lb_ref[...])


def _ffn(x, mod, w1, w2, ln_g, ln_b, alpha, tm):
    b, l, d = x.shape
    f = w2.shape[0]
    mxu_n = 256
    half = (f // mxu_n + 1) // 2 * mxu_n
    f_chunks = ((0, half), (half, f))
    xspec = pl.BlockSpec((1, tm, d), lambda bi, i: (bi, i, 0))
    full = lambda shape: pl.BlockSpec(shape, lambda bi, i: (0,) * len(shape))
    once = pl.Buffered(1)
    vmem = 4 * tm * d * 4 + 3 * d * f * 2 + 3 * tm * d * 4 + 4 * tm * half * 4
    return pl.pallas_call(
        functools.partial(_ffn_kernel, alpha=alpha, f_chunks=f_chunks),
        grid=(b, l // tm),
        in_specs=[xspec,
                  pl.BlockSpec((1, 6, d), lambda bi, i: (bi, 0, 0)),
                  pl.BlockSpec((d, f), lambda bi, i: (0, 0), pipeline_mode=once),
                  pl.BlockSpec((d, f), lambda bi, i: (0, 1), pipeline_mode=once),
                  pl.BlockSpec((f, d), lambda bi, i: (0, 0), pipeline_mode=once),
                  full(ln_g.shape), full(ln_b.shape)],
        out_specs=xspec,
        out_shape=jax.ShapeDtypeStruct((b, l, d), F32),
        compiler_params=_cparams(("parallel", "parallel"), vmem),
        name="ffn_ln",
    )(x, mod, w1, w1, w2, ln_g, ln_b)


def _pos_embed(n_tok, dim):
    rows = n_tok // GRID_W
    row, col = jnp.meshgrid(jnp.arange(rows, dtype=F32), jnp.arange(GRID_W, dtype=F32), indexing='ij')
    quarter = dim // 4
    freqs = jnp.exp(-math.log(10000.0) * jnp.arange(quarter, dtype=F32) / quarter)

    def enc(p):
        ang = p.reshape(-1, 1) * freqs[None, :]
        return jnp.concatenate([jnp.sin(ang), jnp.cos(ang)], -1)

    return jnp.concatenate([enc(row), enc(col)], -1)


def _block_diag(blocks):
    g, n, m = blocks.shape
    eye = jnp.eye(g, dtype=blocks.dtype)
    return (eye[:, None, :, None] * blocks[:, :, None, :]).reshape(g * n, g * m)


def _dft_tables(n):
    def direct(rows, cols, period):
        ang = ((rows[:, None] * cols[None, :]) % period).astype(F32) * (2.0 * math.pi / period)
        return jnp.cos(ang), jnp.sin(ang)

    idx = jnp.arange(n, dtype=jnp.int32)
    if n <= 1024:
        return direct(idx, idx, n)
    lo = 64
    ca, sa = direct(jnp.arange(n // lo, dtype=jnp.int32), idx, n // lo)
    cb, sb = direct(jnp.arange(lo, dtype=jnp.int32), idx, n)
    ca, sa, cb, sb = ca[:, None, :], sa[:, None, :], cb[None], sb[None]
    return (ca * cb - sa * sb).reshape(n, n), (sa * cb + ca * sb).reshape(n, n)


def kernel(x, c, ctx, c_ctx, w_mod, b_mod, w_in, rkv_conv, decay_w0, decay_w2, iclr_a0, iclr_a2,
           gate_g2, k_k, k_a, r_k, gn_g, gn_b, pool_w, pool_scale, gmlp_ln_g, gmlp_ln_b, gmlp_ws,
           gmlp_bs, fnet_w, fnet_b, w_out, ln1_g, ln1_b, ln2_g, ln2_b, ffn_w1, ffn_w2):
    bsz, seq, d = x.shape
    ctx_len = ctx.shape[1]
    depth = w_in.shape[0]
    d_ff = ffn_w2.shape[1]
    alpha = (2 * depth) ** 0.25
    rwkv_cols = 3 * GROUP_W + LORA_W
    assert bsz * N_HEADS * 2 == V7X_LANES, "scan layout packs (direction, batch, head) on the 128 lanes"

    pos = _pos_embed(seq, d)
    ones_bd = _block_diag(jnp.ones((N_HEADS, HEAD, HEAD), F32)).astype(BF16)
    cc64, ss64 = _dft_tables(HEAD)
    cc_bd = _block_diag(jnp.tile(cc64[None], (4, 1, 1))).astype(BF16)
    sc_bd = _block_diag(jnp.tile(ss64[None], (4, 1, 1))).astype(BF16)

    def dft_rows(n):
        cn, sn = _dft_tables(n)
        return jnp.concatenate([cn, -sn], axis=1).astype(BF16)

    cs_x = dft_rows(seq)
    cs_c = dft_rows(ctx_len)

    pad_rows = (-(bsz + 1)) % V7X_SUBLANES
    c_all = jnp.concatenate([c, c_ctx[None], jnp.zeros((pad_rows, d), F32)], axis=0)

    tm_x = 512
    tm_c = ctx_len
    tb = 64

    xs, cs = x, ctx
    for l in range(depth):
        last = l == depth - 1
        first = l == 0
        m = _modulation(c_all, w_mod[l].astype(BF16), b_mod[l])
        mod_x = m[:bsz].reshape(bsz, 6, d)
        mod_c = jnp.broadcast_to(m[bsz].reshape(1, 6, d), (bsz, 6, d))

        w_in_l = w_in[l].astype(BF16)
        widths_full = (3 * GROUP_W, LORA_W, GROUP_W, 2 * GROUP_W, GROUP_W)
        px = _in_proj(xs, pos if first else None, mod_x, w_in_l, widths_full, tm_x)
        if last:
            pc = _in_proj(cs, None, mod_c, w_in_l[:, :rwkv_cols], widths_full[:2], tm_c)
        else:
            pc = _in_proj(cs, None, mod_c, w_in_l, widths_full, tm_c)

        wl = jnp.zeros((LORA_W, 5 * GROUP_W), F32)
        wl = wl.at[0:64, 0:256].set(decay_w2[l, 0]).at[64:128, 256:512].set(decay_w2[l, 1])
        wl = wl.at[128:192, 512:768].set(iclr_a2[l, 0]).at[192:256, 768:1024].set(iclr_a2[l, 1])
        wl = wl.at[256:384, 1024:1280].set(gate_g2[l]).astype(BF16)
        prep_args = (rkv_conv[l], wl, decay_w0[l], iclr_a0[l], k_k[l].reshape(1, -1), k_a[l].reshape(1, -1),
                     ones_bd)
        p_r, p_v, p_kk, p_gate, p_w2, p_b2, p_kd2 = _rwkv_prep(px[0], px[1], pc[0], pc[1], *prep_args)
        to_scan = functools.partial(_to_scan_layout, n_lat=seq, n_ctx=ctx_len)
        y_s = _rwkv_scan(to_scan(p_r), to_scan(p_v, step_major=True), to_scan(p_kk), to_scan(p_w2), to_scan(p_b2),
                         to_scan(p_kd2), tb)
        y2 = _from_scan_layout(y_s, bsz, seq, ctx_len)

        pool_bd = _block_diag(pool_w[l]).astype(BF16)
        fnet_bd = _block_diag(fnet_w[l]).astype(BF16)
        ws_b = gmlp_ws[l].astype(BF16)
        bias_tile = jnp.repeat(gmlp_bs[l].T, HEAD, axis=1)
        row = lambda a: a.reshape(1, -1)

        mix_params = (row(r_k[l]), row(gn_g[l]), row(gn_b[l]), ones_bd, row(gmlp_ln_g[l]), row(gmlp_ln_b[l]), ws_b,
                      bias_tile, w_out[l].astype(BF16), row(ln1_g[l]), row(ln1_b[l]))

        def mix_sublayer(parts, row0, stream, pos_tab, mod, cs_tab, tm, tmf):
            pooled = _pool_mixer(parts[2], pool_bd, row(pool_scale[l]))
            fourier = _fourier_mixer(parts[4], cs_tab, cc_bd, sc_bd, fnet_bd, row(fnet_b[l]), tmf)
            return _mix_out_proj(y2, p_r, p_v, p_kd2, p_gate, row0, parts[3], pooled, fourier, stream, pos_tab, mod,
                                 mix_params, alpha, tm)

        w1_l = ffn_w1[l].astype(BF16)
        w2_l = ffn_w2[l].astype(BF16)
        xs = mix_sublayer(px, 0, xs, pos if first else None, mod_x, cs_x, tm_x, 256)
        xs = _ffn(xs, mod_x, w1_l, w2_l, row(ln2_g[l]), row(ln2_b[l]), alpha, tm_x)
        if not last:
            cs = mix_sublayer(pc, seq, cs, None, mod_c, cs_c, tm_c, tm_c)
            cs = _ffn(cs, mod_c, w1_l, w2_l, row(ln2_g[l]), row(ln2_b[l]), alpha, tm_c)
    return xs
```

```python
import functools
import math

import jax
import jax.numpy as jnp
from jax import lax
from jax.experimental import pallas as pl
from jax.experimental.pallas import tpu as pltpu

F32 = jnp.float32
BF16 = jnp.bfloat16

V7X_LANES = 128
V7X_SUBLANES = 8
V7X_VMEM_BYTES = 64 * 1024 * 1024
V7X_VMEM_LIMIT_CAP = 56 * 1024 * 1024

GRID_W = 64
HEAD = 64
GROUP_W = 256
N_HEADS = GROUP_W // HEAD
LORA_W = 384
POOL_WINDOWS = (2, 4, 8, 16)
CHUNK = 128
LN_EPS = 1e-5
GN_EPS = 64e-5


def _cparams(sem, vmem_bytes):
    limit = int(min(max(vmem_bytes, 16 * 1024 * 1024), V7X_VMEM_LIMIT_CAP))
    return pltpu.CompilerParams(dimension_semantics=sem, vmem_limit_bytes=limit)


def _silu(x):
    return x * jax.nn.sigmoid(x)


def _seg_sum(x, ones_bd):
    hi = x.astype(BF16)
    lo = (x - hi.astype(F32)).astype(BF16)
    return (jnp.dot(hi, ones_bd, preferred_element_type=F32)
            + jnp.dot(lo, ones_bd, preferred_element_type=F32))


def _layer_norm(z, g, b):
    mu = jnp.mean(z, axis=-1, keepdims=True)
    zc = z - mu
    var = jnp.mean(zc * zc, axis=-1, keepdims=True)
    return zc * lax.rsqrt(var + LN_EPS) * g + b


def _mod_kernel(c_ref, w_ref, b_ref, o_ref):
    s = _silu(c_ref[...])
    o_ref[...] = jnp.dot(s.astype(BF16), w_ref[...], preferred_element_type=F32) + b_ref[...]


def _modulation(c_all, w_mod, b_mod):
    rows, d = c_all.shape
    n = w_mod.shape[1]
    tn = 1024
    return pl.pallas_call(
        _mod_kernel,
        grid=(n // tn,),
        in_specs=[pl.BlockSpec((rows, d), lambda j: (0, 0)),
                  pl.BlockSpec((d, tn), lambda j: (0, j)),
                  pl.BlockSpec((1, tn), lambda j: (0, j))],
        out_specs=pl.BlockSpec((rows, tn), lambda j: (0, j)),
        out_shape=jax.ShapeDtypeStruct((rows, n), F32),
        compiler_params=_cparams(("parallel",), 4 * d * tn * 2),
        name="modulation",
    )(c_all, w_mod, b_mod.reshape(1, n))


def _inproj_kernel(*refs, widths, add_pos):
    if add_pos:
        x_ref, pos_ref, mod_ref, w_ref = refs[:4]
        outs = refs[4:]
        x = x_ref[0] + pos_ref[...]
    else:
        x_ref, mod_ref, w_ref = refs[:3]
        outs = refs[3:]
        x = x_ref[0]
    sh = mod_ref[0, 0:1, :]
    sc = mod_ref[0, 1:2, :]
    h = (x * (1.0 + sc) + sh).astype(BF16)
    off = 0
    for o_ref, wd in zip(outs, widths):
        o_ref[0] = jnp.dot(h, w_ref[:, off:off + wd], preferred_element_type=F32)
        off += wd


def _in_proj(x, pos, mod, w, widths, tm):
    b, l, d = x.shape
    n = w.shape[1]
    assert sum(widths) == n
    add_pos = pos is not None
    in_specs = [pl.BlockSpec((1, tm, d), lambda i, bi: (bi, i, 0))]
    args = [x]
    if add_pos:
        in_specs.append(pl.BlockSpec((tm, d), lambda i, bi: (i, 0)))
        args.append(pos)
    in_specs += [pl.BlockSpec((1, 6, d), lambda i, bi: (bi, 0, 0)),
                 pl.BlockSpec((d, n), lambda i, bi: (0, 0))]
    args += [mod, w]
    vmem = 2 * (2 * tm * d * 4 + d * n * 2 + tm * n * 4) + tm * d * 8
    return pl.pallas_call(
        functools.partial(_inproj_kernel, widths=widths, add_pos=add_pos),
        grid=(l // tm, b),
        in_specs=in_specs,
        out_specs=[pl.BlockSpec((1, tm, wd), lambda i, bi: (bi, i, 0)) for wd in widths],
        out_shape=[jax.ShapeDtypeStruct((b, l, wd), F32) for wd in widths],
        compiler_params=_cparams(("parallel", "parallel"), vmem),
        name="in_proj",
    )(*args)


def _prep_kernel(zx_ref, zp_ref, zn_ref, zc_ref, lox_ref, loc_ref, conv_ref, wl_ref, w0_ref, a0_ref, kk_ref,
                 ka_ref, ones_ref, shared_o, pair_o, *, tm, nx):
    i = pl.program_id(1)
    is_ctx = i >= nx
    z = jnp.where(is_ctx, zc_ref[0], zx_ref[0])
    lo = jnp.where(is_ctx, loc_ref[0], lox_ref[0])
    row = lax.broadcasted_iota(jnp.int32, z.shape, 0)
    has_prev = jnp.logical_and(i > 0, i < nx).astype(F32)
    has_next = (i < nx - 1).astype(F32)
    prev_row = zp_ref[0, V7X_SUBLANES - 1:V7X_SUBLANES, :] * has_prev
    next_row = zn_ref[0, 0:1, :] * has_next
    zm1 = jnp.where(row == 0, prev_row, pltpu.roll(z, 1, 0))
    zp1 = jnp.where(row == tm - 1, next_row, pltpu.roll(z, tm - 1, 0))
    c = zm1 * conv_ref[0:1, :] + z * conv_ref[1:2, :] + zp1 * conv_ref[2:3, :]
    r = c[:, 0:GROUP_W]
    k = c[:, GROUP_W:2 * GROUP_W]
    v = c[:, 2 * GROUP_W:3 * GROUP_W]

    col = lax.broadcasted_iota(jnp.int32, lo.shape, 1)
    act = jnp.where(col < 128, jnp.tanh(lo), jnp.where(col < 256, lo, jax.nn.sigmoid(lo)))
    pre = jnp.dot(act.astype(BF16), wl_ref[...], preferred_element_type=F32)

    ones_bd = ones_ref[...]
    kk = k * kk_ref[...]
    nrm = jnp.sqrt(_seg_sum(kk * kk, ones_bd))
    kk = kk / jnp.maximum(nrm, 1e-12)
    blk = lambda j: slice(j * GROUP_W, (j + 1) * GROUP_W)
    for j, val in ((SH_R, r), (SH_V, v), (SH_KK, kk), (SH_GATE, pre[:, 4 * GROUP_W:5 * GROUP_W])):
        shared_o[0, :, blk(j)] = val
    ka = ka_ref[...]
    for d in range(2):
        xw = w0_ref[d:d + 1, :] + pre[:, blk(d)]
        pair_o[0, :, blk(PR_LW + d)] = -math.exp(-0.5) * jax.nn.sigmoid(xw)
        a = jax.nn.sigmoid(a0_ref[d:d + 1, :] + pre[:, blk(2 + d)])
        pair_o[0, :, blk(PR_KD + d)] = k * (1.0 + (a - 1.0) * ka)
        pair_o[0, :, blk(PR_B + d)] = kk * a


SH_R, SH_V, SH_KK, SH_GATE = range(4)
PR_LW, PR_B, PR_KD = 0, 2, 4


def _rwkv_prep(zx_rkv, zx_lora, zc_rkv, zc_lora, conv, wl, w0, a0, k_k, k_a, ones_bd):
    b, l, c3 = zx_rkv.shape
    tm = zc_rkv.shape[1]
    nx = l // tm
    nblk8 = l // V7X_SUBLANES
    r8 = tm // V7X_SUBLANES
    full = lambda shape: pl.BlockSpec(shape, lambda bi, i: (0,) * len(shape))
    xi = lambda i: jnp.minimum(i, nx - 1)
    in_specs = [
        pl.BlockSpec((1, tm, c3), lambda bi, i: (bi, xi(i), 0)),
        pl.BlockSpec((1, V7X_SUBLANES, c3), lambda bi, i: (bi, jnp.clip(i * r8 - 1, 0, nblk8 - 1), 0)),
        pl.BlockSpec((1, V7X_SUBLANES, c3), lambda bi, i: (bi, jnp.minimum((i + 1) * r8, nblk8 - 1), 0)),
        pl.BlockSpec((1, tm, c3), lambda bi, i: (bi, 0, 0)),
        pl.BlockSpec((1, tm, LORA_W), lambda bi, i: (bi, xi(i), 0)),
        pl.BlockSpec((1, tm, LORA_W), lambda bi, i: (bi, 0, 0)),
        full(conv.shape), full(wl.shape), full(w0.shape), full(a0.shape),
        full(k_k.shape), full(k_a.shape), full(ones_bd.shape),
    ]
    widths = (4 * GROUP_W, 6 * GROUP_W)
    vmem = 2 * (2 * tm * (c3 + LORA_W) * 4 + 10 * tm * GROUP_W * 4) + 14 * tm * c3 * 4
    return pl.pallas_call(
        functools.partial(_prep_kernel, tm=tm, nx=nx),
        grid=(b, nx + 1),
        in_specs=in_specs,
        out_specs=[pl.BlockSpec((1, tm, wd), lambda bi, i: (bi, i, 0)) for wd in widths],
        out_shape=[jax.ShapeDtypeStruct((b, l + tm, wd), F32) for wd in widths],
        compiler_params=_cparams(("parallel", "arbitrary"), vmem),
        name="rwkv_prep",
    )(zx_rkv, zx_rkv, zx_rkv, zc_rkv, zx_lora, zc_lora, conv, wl, w0, a0, k_k, k_a, ones_bd)


SCAN_ROW_PITCH = 72


def _to_scan_kernel(xf_ref, xb_ref, o_ref, y_ref, *, nb, tt, step_major):
    rev = (tt - 1) - lax.broadcasted_iota(jnp.int32, (GROUP_W, tt), 1)
    for b in range(nb):
        xf = xf_ref[b].T
        xb = jnp.take_along_axis(xb_ref[b].T, rev, axis=1)
        for h in range(N_HEADS):
            y_ref[pl.ds((b * 2 * N_HEADS + h) * SCAN_ROW_PITCH, HEAD), :] = xf[h * HEAD:(h + 1) * HEAD, :]
            y_ref[pl.ds((b * 2 * N_HEADS + N_HEADS + h) * SCAN_ROW_PITCH, HEAD), :] = xb[h * HEAD:(h + 1) * HEAD, :]
    for k in range(HEAD):
        z = y_ref[pl.ds(k, nb * 2 * N_HEADS, stride=SCAN_ROW_PITCH), :]
        if step_major:
            o_ref[pl.ds(k, tt, stride=HEAD), :] = z.T
        else:
            o_ref[k] = z.T


def _to_scan_layout(x, n_lat, n_ctx, fwd_cols, bwd_cols, step_major=False):
    nb, t, c = x.shape
    tt = V7X_LANES
    assert nb * 2 * N_HEADS == V7X_LANES and c % GROUP_W == 0 and n_lat % tt == 0 and n_ctx % tt == 0
    nl, nc = n_lat // tt, n_ctx // tt
    n = nl + nc
    if step_major:
        out_spec = pl.BlockSpec((tt * HEAD, V7X_LANES), lambda j: (j, 0))
        out_shape = jax.ShapeDtypeStruct((t * HEAD, V7X_LANES), F32)
    else:
        out_spec = pl.BlockSpec((HEAD, tt, V7X_LANES), lambda j: (0, j, 0))
        out_shape = jax.ShapeDtypeStruct((HEAD, t, V7X_LANES), F32)
    out = pl.pallas_call(
        functools.partial(_to_scan_kernel, nb=nb, tt=tt, step_major=step_major),
        grid=(n,),
        in_specs=[pl.BlockSpec((nb, tt, GROUP_W), lambda j: (0, jnp.where(j < nc, nl + j, j - nc), fwd_cols)),
                  pl.BlockSpec((nb, tt, GROUP_W), lambda j: (0, n - 1 - j, bwd_cols))],
        out_specs=out_spec,
        out_shape=out_shape,
        scratch_shapes=[pltpu.VMEM((V7X_LANES * SCAN_ROW_PITCH, tt), F32)],
        compiler_params=_cparams(("parallel",), 2 * (nb * tt * 2 * GROUP_W * 4 + HEAD * tt * V7X_LANES * 4)
                                 + 3 * V7X_LANES * SCAN_ROW_PITCH * tt * 4),
        name="to_scan_layout",
    )(x, x)
    return out.reshape(t, HEAD, V7X_LANES) if step_major else out


def _from_scan_kernel(yf_ref, yb_ref, o_ref, s_ref, *, nb, tt):
    rev = (tt - 1) - lax.broadcasted_iota(jnp.int32, (V7X_LANES, tt), 1)
    fwd_row = (lax.broadcasted_iota(jnp.int32, (V7X_LANES, tt), 0) & N_HEADS) == 0
    for v in range(HEAD):
        zf = yf_ref[pl.ds(v, tt, stride=SCAN_ROW_PITCH), :].T
        zb = jnp.take_along_axis(yb_ref[pl.ds(v, tt, stride=SCAN_ROW_PITCH), :].T, rev, axis=1)
        s_ref[pl.ds(v, V7X_LANES, stride=SCAN_ROW_PITCH), :] = jnp.where(fwd_row, zf, zb)
    groups = 2 * N_HEADS
    for b in range(nb):
        grp = lambda g: s_ref[pl.ds((b * groups + g) * SCAN_ROW_PITCH, HEAD), :]
        rows = [grp(h) + grp(N_HEADS + h) for h in range(N_HEADS)]
        o_ref[b] = jnp.concatenate(rows, axis=0).T


def _from_scan_layout(y, nb, n_lat, n_ctx):
    t, pitch, _ = y.shape
    assert pitch == SCAN_ROW_PITCH
    tt = V7X_LANES
    nl, nc = n_lat // tt, n_ctx // tt
    n = nl + nc
    y2d = y.reshape(t * pitch, V7X_LANES)
    blk = pl.BlockSpec((tt * pitch, V7X_LANES), lambda j: (jnp.where(j < nl, j + nc, j - nl), 0))
    rblk = pl.BlockSpec((tt * pitch, V7X_LANES), lambda j: (n - 1 - j, 0))
    return pl.pallas_call(
        functools.partial(_from_scan_kernel, nb=nb, tt=tt),
        grid=(n,),
        in_specs=[blk, rblk],
        out_specs=pl.BlockSpec((nb, tt, GROUP_W), lambda j: (0, j, 0)),
        out_shape=jax.ShapeDtypeStruct((nb, t, GROUP_W), F32),
        scratch_shapes=[pltpu.VMEM((V7X_LANES * SCAN_ROW_PITCH, tt), F32)],
        compiler_params=_cparams(("parallel",), 2 * (2 * tt * pitch * V7X_LANES * 4 + nb * tt * 2 * GROUP_W * 4)
                                 + 3 * V7X_LANES * SCAN_ROW_PITCH * tt * 4),
        name="from_scan_layout",
    )(y2d, y2d)


SCAN_SLAB_PAD = 8


def _scan_kernel(r_ref, v_ref, kk_ref, lw_ref, b_ref, kd_ref, y_ref, h_ref, s_ref, g_ref, *, tb):
    @pl.when(pl.program_id(0) == 0)
    def _():
        h_ref[...] = jnp.zeros_like(h_ref)

    tile = (V7X_SUBLANES, V7X_LANES)
    R_G, KK_G, B_G, KD_G = range(4)

    sub = lax.broadcasted_iota(jnp.int32, tile, 0)

    def scale(k, carry):
        lw = lw_ref[k]
        parts = []
        total = jnp.zeros(tile, F32)
        for j in range(tb // V7X_SUBLANES):
            x = lw[j * V7X_SUBLANES:(j + 1) * V7X_SUBLANES, :]
            for s in (1, 2, 4):
                x = x + jnp.where(sub >= s, pltpu.roll(x, s, 0), 0.0)
            x = x + total
            total = jnp.broadcast_to(x[V7X_SUBLANES - 1:V7X_SUBLANES, :], tile)
            parts.append(x)
        cum = jnp.concatenate(parts, axis=0)
        g = jnp.exp(cum)
        g_inv = jnp.exp(-cum)
        s_ref[R_G, k, 0:tb] = r_ref[k] * g
        s_ref[KK_G, k, 0:tb] = kk_ref[k] * jnp.exp(cum - lw)
        s_ref[B_G, k, 0:tb] = b_ref[k] * g_inv
        s_ref[KD_G, k, 0:tb] = kd_ref[k] * g_inv
        g_ref[pl.ds(k, 1), :] = g[tb - 1:tb, :]
        return carry

    lax.fori_loop(0, HEAD, scale, 0)

    u0 = jnp.zeros((HEAD, V7X_LANES), F32)
    for k in range(HEAD):
        u0 = u0 + s_ref[KK_G, k, 0:1, :] * h_ref[k]

    def step(t, u):
        tn = jnp.minimum(t + 1, tb - 1)
        vv = v_ref[t]
        y = jnp.zeros((HEAD, V7X_LANES), F32)
        un = jnp.zeros((HEAD, V7X_LANES), F32)
        for k in range(HEAD):
            hn = h_ref[k] - s_ref[B_G, k, pl.ds(t, 1), :] * u + s_ref[KD_G, k, pl.ds(t, 1), :] * vv
            h_ref[k] = hn
            y = y + s_ref[R_G, k, pl.ds(t, 1), :] * hn
            un = un + s_ref[KK_G, k, pl.ds(tn, 1), :] * hn
        y_ref[t, 0:HEAD] = y
        return un

    y_ref[:, HEAD:, :] = jnp.zeros((tb, SCAN_ROW_PITCH - HEAD, V7X_LANES), F32)
    lax.fori_loop(0, tb, step, u0)
    for k in range(HEAD):
        h_ref[k] = h_ref[k] * g_ref[k:k + 1, :]


def _rwkv_scan(r, v, kk, w, b, kd, tb):
    t_tot = r.shape[1]
    kspec = pl.BlockSpec((HEAD, tb, V7X_LANES), lambda i: (0, i, 0))
    sspec = pl.BlockSpec((tb, HEAD, V7X_LANES), lambda i: (i, 0, 0))
    blk = tb * HEAD * V7X_LANES * 4
    return pl.pallas_call(
        functools.partial(_scan_kernel, tb=tb),
        grid=(t_tot // tb,),
        in_specs=[kspec, sspec, kspec, kspec, kspec, kspec],
        out_specs=pl.BlockSpec((tb, SCAN_ROW_PITCH, V7X_LANES), lambda i: (i, 0, 0)),
        out_shape=jax.ShapeDtypeStruct((t_tot, SCAN_ROW_PITCH, V7X_LANES), F32),
        scratch_shapes=[pltpu.VMEM((HEAD, HEAD, V7X_LANES), F32),
                        pltpu.VMEM((4, HEAD, tb + SCAN_SLAB_PAD, V7X_LANES), F32),
                        pltpu.VMEM((HEAD, V7X_LANES), F32)],
        compiler_params=_cparams(("arbitrary",), 14 * blk + 5 * blk + 3 * HEAD * HEAD * V7X_LANES * 4),
        name="rwkv_scan",
    )(r, v, kk, w, b, kd)


def _readout(y, r, v, kd2, gate, r_k, gn_g, gn_b, ones_bd):
    inv = 1.0 / HEAD
    kd = kd2[:, 0:GROUP_W] + kd2[:, GROUP_W:2 * GROUP_W]
    mu = _seg_sum(y, ones_bd) * inv
    yc = y - mu
    var = _seg_sum(yc * yc, ones_bd) * inv
    yn = yc * lax.rsqrt(var + GN_EPS)
    bonus = _seg_sum(r * kd * r_k, ones_bd) * v
    return (yn * gn_g + gn_b + bonus) * gate


def _pool_kernel(z_ref, w_ref, s_ref, o_ref, *, l):
    z = z_ref[0]
    row = lax.broadcasted_iota(jnp.int32, z.shape, 0)
    lane = lax.broadcasted_iota(jnp.int32, z.shape, 1)

    def shifted(x, o):
        y = pltpu.roll(x, (-o) % l, 0)
        return jnp.where((row + o >= 0) & (row + o < l), y, 0.0)

    def count(wd):
        lo = jnp.maximum(row - wd // 2, 0)
        hi = jnp.minimum(row + wd - wd // 2, l)
        return (hi - lo).astype(F32)

    s2 = shifted(z, -1) + z
    pooled = s2 / count(2)
    s4 = shifted(s2, -1) + shifted(s2, 1)
    pooled = jnp.where(lane >= 64, s4 / count(4), pooled)
    s8 = shifted(s4, -2) + shifted(s4, 2)
    pooled = jnp.where(lane >= 128, s8 / count(8), pooled)
    s16 = shifted(s8, -4) + shifted(s8, 4)
    pooled = jnp.where(lane >= 192, s16 / count(16), pooled)
    d = pooled - z
    o_ref[0] = jnp.dot(d.astype(BF16), w_ref[...], preferred_element_type=F32) * s_ref[...]


def _pool_mixer(z, w_bd, scale):
    b, l, c = z.shape
    spec = pl.BlockSpec((1, l, c), lambda bi: (bi, 0, 0))
    return pl.pallas_call(
        functools.partial(_pool_kernel, l=l),
        grid=(b,),
        in_specs=[spec, pl.BlockSpec(w_bd.shape, lambda bi: (0, 0)), pl.BlockSpec(scale.shape, lambda bi: (0, 0))],
        out_specs=spec,
        out_shape=jax.ShapeDtypeStruct((b, l, c), F32),
        compiler_params=_cparams(("parallel",), 20 * l * c * 4),
        name="pool_mixer",
    )(z, w_bd, scale)


def _gelu_tanh(x):
    return 0.5 * x * (1.0 + jnp.tanh(math.sqrt(2.0 / math.pi) * (x + 0.044715 * (x * x * x))))


def _gmlp(z, ln_g, ln_b, ws_ref, bias, ones_bd):
    inv = 1.0 / HEAD
    tm = z.shape[0]
    u = _gelu_tanh(z[:, 0:GROUP_W])
    v = _gelu_tanh(z[:, GROUP_W:2 * GROUP_W])
    mu = _seg_sum(v, ones_bd) * inv
    vc = v - mu
    var = _seg_sum(vc * vc, ones_bd) * inv
    vn = (vc * lax.rsqrt(var + LN_EPS) * ln_g + ln_b).astype(BF16)
    lane = lax.broadcasted_iota(jnp.int32, (CHUNK, GROUP_W), 1)
    out = []
    for c in range(tm // CHUNK):
        vchunk = vn[c * CHUNK:(c + 1) * CHUNK, :]
        sv = bias
        for g in range(4):
            part = jnp.dot(ws_ref[g], vchunk, preferred_element_type=F32)
            sv = sv + jnp.where((lane >= g * HEAD) & (lane < (g + 1) * HEAD), part, 0.0)
        out.append(u[c * CHUNK:(c + 1) * CHUNK, :] * sv)
    return jnp.concatenate(out, axis=0)


def _fnet_kernel(z_ref, cs_ref, cc_ref, sc_ref, w_ref, b_ref, o_ref, zcs_ref, *, l, scale, nbg):
    c = z_ref.shape[2]

    @pl.when(pl.program_id(1) == 0)
    def _():
        for j in range(nbg):
            zb = z_ref[j].astype(BF16)
            zcs_ref[0:l, j * c:(j + 1) * c] = jnp.dot(zb, cc_ref[...], preferred_element_type=F32).astype(BF16)
            zcs_ref[l:2 * l, j * c:(j + 1) * c] = jnp.dot(zb, sc_ref[...], preferred_element_type=F32).astype(BF16)

    f = jnp.dot(cs_ref[...], zcs_ref[...], preferred_element_type=F32) * scale
    for j in range(nbg):
        o_ref[j] = jnp.dot(f[:, j * c:(j + 1) * c].astype(BF16), w_ref[...], preferred_element_type=F32) + b_ref[...]


def _fourier_mixer(z, cs, cc_bd, sc_bd, w_bd, bias, tm, nbg=4):
    b, l, c = z.shape
    assert b % nbg == 0
    full = lambda shape: pl.BlockSpec(shape, lambda bi, i: (0,) * len(shape))
    scale = 1.0 / math.sqrt(l * HEAD)
    vmem = 2 * (nbg * l * c * 4 + tm * 2 * l * 2 + nbg * tm * c * 4) + 2 * l * nbg * c * 2 + 4 * l * c * 4
    return pl.pallas_call(
        functools.partial(_fnet_kernel, l=l, scale=scale, nbg=nbg),
        grid=(b // nbg, l // tm),
        in_specs=[pl.BlockSpec((nbg, l, c), lambda bi, i: (bi, 0, 0)),
                  pl.BlockSpec((tm, 2 * l), lambda bi, i: (i, 0)),
                  full(cc_bd.shape), full(sc_bd.shape), full(w_bd.shape), full(bias.shape)],
        out_specs=pl.BlockSpec((nbg, tm, c), lambda bi, i: (bi, i, 0)),
        out_shape=jax.ShapeDtypeStruct((b, l, c), F32),
        scratch_shapes=[pltpu.VMEM((2 * l, nbg * c), BF16)],
        compiler_params=_cparams(("parallel", "arbitrary"), vmem),
        name="fourier_mixer",
    )(z, cs, cc_bd, sc_bd, w_bd, bias)


def _outproj_kernel(*refs, alpha, add_pos):
    (y_ref, r_ref, v_ref, kd_ref, gate_ref, zg_ref, p_ref, f_ref, x_ref), refs = refs[:9], refs[9:]
    if add_pos:
        pos_ref, refs = refs[0], refs[1:]
        x = x_ref[0] + pos_ref[...]
    else:
        x = x_ref[0]
    (mod_ref, rk_ref, gg_ref, gb_ref, ones_ref, lng_ref, lnb_ref, ws_ref, bias_ref, w_ref, lg_ref, lb_ref,
     o_ref) = refs
    ones_bd = ones_ref[...]
    a = _readout(y_ref[0], r_ref[0], v_ref[0], kd_ref[0], gate_ref[0], rk_ref[...], gg_ref[...], gb_ref[...], ones_bd)
    g = _gmlp(zg_ref[0], lng_ref[...], lnb_ref[...], ws_ref, bias_ref[...], ones_bd)
    mix = None
    for j, part in enumerate((a, p_ref[0], g, f_ref[0])):
        prod = jnp.dot(part.astype(BF16), w_ref[j * GROUP_W:(j + 1) * GROUP_W, :], preferred_element_type=F32)
        mix = prod if mix is None else mix + prod
    gate1 = mod_ref[0, 2:3, :]
    o_ref[0] = _layer_norm(alpha * x + gate1 * mix, lg_ref[...], lb_ref[...])


def _mix_out_proj(y, shared, pair, row0, z_gmlp, pooled, fourier, x, pos, mod, params, alpha, tm):
    b, l, d = x.shape
    add_pos = pos is not None
    off = row0 // tm
    comb = lambda width, col: pl.BlockSpec((1, tm, width), lambda i, bi: (bi, i + off, col))
    s1 = pl.BlockSpec((1, tm, GROUP_W), lambda i, bi: (bi, i, 0))
    s2 = pl.BlockSpec((1, tm, 2 * GROUP_W), lambda i, bi: (bi, i, 0))
    xspec = pl.BlockSpec((1, tm, d), lambda i, bi: (bi, i, 0))
    full = lambda shape: pl.BlockSpec(shape, lambda i, bi: (0,) * len(shape))
    in_specs = [comb(GROUP_W, 0), comb(GROUP_W, SH_R), comb(GROUP_W, SH_V), comb(2 * GROUP_W, PR_KD // 2),
                comb(GROUP_W, SH_GATE), s2, s1, s1, xspec]
    args = [y, shared, shared, pair, shared, z_gmlp, pooled, fourier, x]
    if add_pos:
        in_specs.append(pl.BlockSpec((tm, d), lambda i, bi: (i, 0)))
        args.append(pos)
    in_specs += [pl.BlockSpec((1, 6, d), lambda i, bi: (bi, 0, 0))] + [full(p.shape) for p in params]
    args += [mod, *params]
    vmem = 2 * (10 * tm * GROUP_W * 4 + 3 * tm * d * 4 + d * d * 2) + 6 * tm * d * 4 + 24 * tm * GROUP_W * 4
    return pl.pallas_call(
        functools.partial(_outproj_kernel, alpha=alpha, add_pos=add_pos),
        grid=(l // tm, b),
        in_specs=in_specs,
        out_specs=xspec,
        out_shape=jax.ShapeDtypeStruct((b, l, d), F32),
        compiler_params=_cparams(("parallel", "parallel"), vmem),
        name="mix_out_proj_ln",
    )(*args)


def _ffn_kernel(x_ref, mod_ref, wg_ref, wu_ref, w2_ref, lg_ref, lb_ref, o_ref, *, alpha, f_chunks):
    x = x_ref[0]
    sh = mod_ref[0, 3:4, :]
    sc = mod_ref[0, 4:5, :]
    h = (x * (1.0 + sc) + sh).astype(BF16)
    acc = None
    for lo, hi in f_chunks:
        gate = jnp.dot(h, wg_ref[:, lo:hi], preferred_element_type=F32)
        up = jnp.dot(h, wu_ref[:, lo:hi], preferred_element_type=F32)
        act = (_silu(gate) * up).astype(BF16)
        part = jnp.dot(act, w2_ref[lo:hi, :], preferred_element_type=F32)
        acc = part if acc is None else acc + part
    gate2 = mod_ref[0, 5:6, :]
    o_ref[0] = _layer_norm(alpha * x + gate2 * acc, lg_ref[...], lb_ref[...])


def _ffn(x, mod, w1, w2, ln_g, ln_b, alpha, tm):
    b, l, d = x.shape
    f = w2.shape[0]
    mxu_n = 256
    half = (f // mxu_n + 1) // 2 * mxu_n
    f_chunks = ((0, half), (half, f))
    xspec = pl.BlockSpec((1, tm, d), lambda bi, i: (bi, i, 0))
    full = lambda shape: pl.BlockSpec(shape, lambda bi, i: (0,) * len(shape))
    once = pl.Buffered(1)
    vmem = 4 * tm * d * 4 + 3 * d * f * 2 + 3 * tm * d * 4 + 4 * tm * half * 4
    return pl.pallas_call(
        functools.partial(_ffn_kernel, alpha=alpha, f_chunks=f_chunks),
        grid=(b, l // tm),
        in_specs=[xspec,
                  pl.BlockSpec((1, 6, d), lambda bi, i: (bi, 0, 0)),
                  pl.BlockSpec((d, f), lambda bi, i: (0, 0), pipeline_mode=once),
                  pl.BlockSpec((d, f), lambda bi, i: (0, 1), pipeline_mode=once),
                  pl.BlockSpec((f, d), lambda bi, i: (0, 0), pipeline_mode=once),
                  full(ln_g.shape), full(ln_b.shape)],
        out_specs=xspec,
        out_shape=jax.ShapeDtypeStruct((b, l, d), F32),
        compiler_params=_cparams(("parallel", "parallel"), vmem),
        name="ffn_ln",
    )(x, mod, w1, w1, w2, ln_g, ln_b)


def _pos_embed(n_tok, dim):
    rows = n_tok // GRID_W
    row, col = jnp.meshgrid(jnp.arange(rows, dtype=F32), jnp.arange(GRID_W, dtype=F32), indexing='ij')
    quarter = dim // 4
    freqs = jnp.exp(-math.log(10000.0) * jnp.arange(quarter, dtype=F32) / quarter)

    def enc(p):
        ang = p.reshape(-1, 1) * freqs[None, :]
        return jnp.concatenate([jnp.sin(ang), jnp.cos(ang)], -1)

    return jnp.concatenate([enc(row), enc(col)], -1)


def _block_diag(blocks):
    g, n, m = blocks.shape
    eye = jnp.eye(g, dtype=blocks.dtype)
    return (eye[:, None, :, None] * blocks[:, :, None, :]).reshape(g * n, g * m)


def _dft_tables(n):
    def direct(rows, cols, period):
        ang = ((rows[:, None] * cols[None, :]) % period).astype(F32) * (2.0 * math.pi / period)
        return jnp.cos(ang), jnp.sin(ang)

    idx = jnp.arange(n, dtype=jnp.int32)
    if n <= 1024:
        return direct(idx, idx, n)
    lo = 64
    ca, sa = direct(jnp.arange(n // lo, dtype=jnp.int32), idx, n // lo)
    cb, sb = direct(jnp.arange(lo, dtype=jnp.int32), idx, n)
    ca, sa, cb, sb = ca[:, None, :], sa[:, None, :], cb[None], sb[None]
    return (ca * cb - sa * sb).reshape(n, n), (sa * cb + ca * sb).reshape(n, n)


def kernel(x, c, ctx, c_ctx, w_mod, b_mod, w_in, rkv_conv, decay_w0, decay_w2, iclr_a0, iclr_a2,
           gate_g2, k_k, k_a, r_k, gn_g, gn_b, pool_w, pool_scale, gmlp_ln_g, gmlp_ln_b, gmlp_ws,
           gmlp_bs, fnet_w, fnet_b, w_out, ln1_g, ln1_b, ln2_g, ln2_b, ffn_w1, ffn_w2):
    bsz, seq, d = x.shape
    ctx_len = ctx.shape[1]
    depth = w_in.shape[0]
    d_ff = ffn_w2.shape[1]
    alpha = (2 * depth) ** 0.25
    rwkv_cols = 3 * GROUP_W + LORA_W
    assert bsz * N_HEADS * 2 == V7X_LANES, "scan layout packs (direction, batch, head) on the 128 lanes"

    pos = _pos_embed(seq, d)
    ones_bd = _block_diag(jnp.ones((N_HEADS, HEAD, HEAD), F32)).astype(BF16)
    cc64, ss64 = _dft_tables(HEAD)
    cc_bd = _block_diag(jnp.tile(cc64[None], (4, 1, 1))).astype(BF16)
    sc_bd = _block_diag(jnp.tile(ss64[None], (4, 1, 1))).astype(BF16)

    def dft_rows(n):
        cn, sn = _dft_tables(n)
        return jnp.concatenate([cn, -sn], axis=1).astype(BF16)

    cs_x = dft_rows(seq)
    cs_c = dft_rows(ctx_len)

    pad_rows = (-(bsz + 1)) % V7X_SUBLANES
    c_all = jnp.concatenate([c, c_ctx[None], jnp.zeros((pad_rows, d), F32)], axis=0)

    tm_x = 512
    tm_c = ctx_len
    tb = 64

    xs, cs = x, ctx
    for l in range(depth):
        last = l == depth - 1
        first = l == 0
        m = _modulation(c_all, w_mod[l].astype(BF16), b_mod[l])
        mod_x = m[:bsz].reshape(bsz, 6, d)
        mod_c = jnp.broadcast_to(m[bsz].reshape(1, 6, d), (bsz, 6, d))

        w_in_l = w_in[l].astype(BF16)
        widths_full = (3 * GROUP_W, LORA_W, GROUP_W, 2 * GROUP_W, GROUP_W)
        px = _in_proj(xs, pos if first else None, mod_x, w_in_l, widths_full, tm_x)
        if last:
            pc = _in_proj(cs, None, mod_c, w_in_l[:, :rwkv_cols], widths_full[:2], tm_c)
        else:
            pc = _in_proj(cs, None, mod_c, w_in_l, widths_full, tm_c)

        wl = jnp.zeros((LORA_W, 5 * GROUP_W), F32)
        wl = wl.at[0:64, 0:256].set(decay_w2[l, 0]).at[64:128, 256:512].set(decay_w2[l, 1])
        wl = wl.at[128:192, 512:768].set(iclr_a2[l, 0]).at[192:256, 768:1024].set(iclr_a2[l, 1])
        wl = wl.at[256:384, 1024:1280].set(gate_g2[l]).astype(BF16)
        prep_args = (rkv_conv[l], wl, decay_w0[l], iclr_a0[l], k_k[l].reshape(1, -1), k_a[l].reshape(1, -1),
                     ones_bd)
        shared, pair = _rwkv_prep(px[0], px[1], pc[0], pc[1], *prep_args)
        to_scan = lambda a, col_f, col_b, **kw: _to_scan_layout(a, seq, ctx_len, col_f, col_b, **kw)
        y_s = _rwkv_scan(to_scan(shared, SH_R, SH_R), to_scan(shared, SH_V, SH_V, step_major=True),
                         to_scan(shared, SH_KK, SH_KK), to_scan(pair, PR_LW, PR_LW + 1),
                         to_scan(pair, PR_B, PR_B + 1), to_scan(pair, PR_KD, PR_KD + 1), tb)
        y_nat = _from_scan_layout(y_s, bsz, seq, ctx_len)

        pool_bd = _block_diag(pool_w[l]).astype(BF16)
        fnet_bd = _block_diag(fnet_w[l]).astype(BF16)
        ws_b = gmlp_ws[l].astype(BF16)
        bias_tile = jnp.repeat(gmlp_bs[l].T, HEAD, axis=1)
        row = lambda a: a.reshape(1, -1)

        mix_params = (row(r_k[l]), row(gn_g[l]), row(gn_b[l]), ones_bd, row(gmlp_ln_g[l]), row(gmlp_ln_b[l]), ws_b,
                      bias_tile, w_out[l].astype(BF16), row(ln1_g[l]), row(ln1_b[l]))

        def mix_sublayer(parts, row0, stream, pos_tab, mod, cs_tab, tm, tmf):
            pooled = _pool_mixer(parts[2], pool_bd, row(pool_scale[l]))
            fourier = _fourier_mixer(parts[4], cs_tab, cc_bd, sc_bd, fnet_bd, row(fnet_b[l]), tmf)
            return _mix_out_proj(y_nat, shared, pair, row0, parts[3], pooled, fourier, stream, pos_tab, mod,
                                 mix_params, alpha, tm)

        w1_l = ffn_w1[l].astype(BF16)
        w2_l = ffn_w2[l].astype(BF16)
        xs = mix_sublayer(px, 0, xs, pos if first else None, mod_x, cs_x, tm_x, 256)
        xs = _ffn(xs, mod_x, w1_l, w2_l, row(ln2_g[l]), row(ln2_b[l]), alpha, tm_x)
        if not last:
            cs = mix_sublayer(pc, seq, cs, None, mod_c, cs_c, tm_c, tm_c)
            cs = _ffn(cs, mod_c, w1_l, w2_l, row(ln2_g[l]), row(ln2_b[l]), alpha, tm_c)
    return xs
```

```python
import functools
import math

import jax
import jax.numpy as jnp
from jax import lax
from jax.experimental import pallas as pl
from jax.experimental.pallas import tpu as pltpu

F32 = jnp.float32
BF16 = jnp.bfloat16

V7X_LANES = 128
V7X_SUBLANES = 8
V7X_VMEM_BYTES = 64 * 1024 * 1024
V7X_VMEM_LIMIT_CAP = 56 * 1024 * 1024

GRID_W = 64
HEAD = 64
GROUP_W = 256
N_HEADS = GROUP_W // HEAD
LORA_W = 384
POOL_WINDOWS = (2, 4, 8, 16)
CHUNK = 128
LN_EPS = 1e-5
GN_EPS = 64e-5


def _cparams(sem, vmem_bytes):
    limit = int(min(max(vmem_bytes, 16 * 1024 * 1024), V7X_VMEM_LIMIT_CAP))
    return pltpu.CompilerParams(dimension_semantics=sem, vmem_limit_bytes=limit)


def _silu(x):
    return x * jax.nn.sigmoid(x)


def _seg_sum(x, ones_bd):
    hi = x.astype(BF16)
    lo = (x - hi.astype(F32)).astype(BF16)
    return (jnp.dot(hi, ones_bd, preferred_element_type=F32)
            + jnp.dot(lo, ones_bd, preferred_element_type=F32))


def _layer_norm(z, g, b):
    mu = jnp.mean(z, axis=-1, keepdims=True)
    zc = z - mu
    var = jnp.mean(zc * zc, axis=-1, keepdims=True)
    return zc * lax.rsqrt(var + LN_EPS) * g + b


def _mod_kernel(c_ref, w_ref, b_ref, o_ref):
    s = _silu(c_ref[...])
    o_ref[...] = jnp.dot(s.astype(BF16), w_ref[...], preferred_element_type=F32) + b_ref[...]


def _modulation(c_all, w_mod, b_mod):
    rows, d = c_all.shape
    n = w_mod.shape[1]
    tn = 1024
    return pl.pallas_call(
        _mod_kernel,
        grid=(n // tn,),
        in_specs=[pl.BlockSpec((rows, d), lambda j: (0, 0)),
                  pl.BlockSpec((d, tn), lambda j: (0, j)),
                  pl.BlockSpec((1, tn), lambda j: (0, j))],
        out_specs=pl.BlockSpec((rows, tn), lambda j: (0, j)),
        out_shape=jax.ShapeDtypeStruct((rows, n), F32),
        compiler_params=_cparams(("parallel",), 4 * d * tn * 2),
        name="modulation",
    )(c_all, w_mod, b_mod.reshape(1, n))


SH_R, SH_V, SH_KK, SH_GATE = range(4)
PR_LW, PR_B, PR_KD = 0, 2, 4
RKV_W = 3 * GROUP_W


def _proj_prep_kernel(*refs, tm, nx, add_pos):
    (xx_ref, xp_ref, xn_ref, xc_ref), refs = refs[:4], refs[4:]
    xx, xp, xn = xx_ref[0], xp_ref[0], xn_ref[0]
    if add_pos:
        (pos_ref, pp_ref, pn_ref), refs = refs[:3], refs[3:]
        xx, xp, xn = xx + pos_ref[...], xp + pp_ref[...], xn + pn_ref[...]
    (modx_ref, modc_ref, w_ref, conv_ref, wl_ref, w0_ref, a0_ref, kk_ref, ka_ref, ones_ref,
     shared_o, pair_o, pool_o, gmlp_o, fnet_o) = refs
    i = pl.program_id(1)
    is_ctx = i >= nx
    x = jnp.where(is_ctx, xc_ref[0], xx)
    mod = jnp.where(is_ctx, modc_ref[0, 0:2, :], modx_ref[0, 0:2, :])
    adaln = lambda t: (t * (1.0 + mod[1:2]) + mod[0:1]).astype(BF16)
    proj = lambda t, lo, hi: jnp.dot(t, w_ref[:, lo:hi], preferred_element_type=F32)
    h = adaln(x)
    z = proj(h, 0, RKV_W)
    lo = proj(h, RKV_W, RKV_W + LORA_W)
    off = RKV_W + LORA_W
    pool_o[0] = proj(h, off, off + GROUP_W)
    gmlp_o[0] = proj(h, off + GROUP_W, off + 3 * GROUP_W)
    fnet_o[0] = proj(h, off + 3 * GROUP_W, off + 4 * GROUP_W)

    has_prev = jnp.logical_and(i > 0, i < nx).astype(F32)
    has_next = (i < nx - 1).astype(F32)
    prev_row = proj(adaln(xp), 0, RKV_W)[V7X_SUBLANES - 1:V7X_SUBLANES, :] * has_prev
    next_row = proj(adaln(xn), 0, RKV_W)[0:1, :] * has_next
    row = lax.broadcasted_iota(jnp.int32, z.shape, 0)
    zm1 = jnp.where(row == 0, prev_row, pltpu.roll(z, 1, 0))
    zp1 = jnp.where(row == tm - 1, next_row, pltpu.roll(z, tm - 1, 0))
    c = zm1 * conv_ref[0:1, :] + z * conv_ref[1:2, :] + zp1 * conv_ref[2:3, :]
    r = c[:, 0:GROUP_W]
    k = c[:, GROUP_W:2 * GROUP_W]
    v = c[:, 2 * GROUP_W:3 * GROUP_W]

    col = lax.broadcasted_iota(jnp.int32, lo.shape, 1)
    act = jnp.where(col < 128, jnp.tanh(lo), jnp.where(col < 256, lo, jax.nn.sigmoid(lo)))
    pre = jnp.dot(act.astype(BF16), wl_ref[...], preferred_element_type=F32)

    ones_bd = ones_ref[...]
    kk = k * kk_ref[...]
    nrm = jnp.sqrt(_seg_sum(kk * kk, ones_bd))
    kk = kk / jnp.maximum(nrm, 1e-12)
    blk = lambda j: slice(j * GROUP_W, (j + 1) * GROUP_W)
    for j, val in ((SH_R, r), (SH_V, v), (SH_KK, kk), (SH_GATE, pre[:, 4 * GROUP_W:5 * GROUP_W])):
        shared_o[0, :, blk(j)] = val
    ka = ka_ref[...]
    for d in range(2):
        xw = w0_ref[d:d + 1, :] + pre[:, blk(d)]
        pair_o[0, :, blk(PR_LW + d)] = -math.exp(-0.5) * jax.nn.sigmoid(xw)
        a = jax.nn.sigmoid(a0_ref[d:d + 1, :] + pre[:, blk(2 + d)])
        pair_o[0, :, blk(PR_KD + d)] = k * (1.0 + (a - 1.0) * ka)
        pair_o[0, :, blk(PR_B + d)] = kk * a


def _proj_prep(x, ctx, pos, mod_x, mod_c, w_in, conv, wl, w0, a0, k_k, k_a, ones_bd):
    b, l, d = x.shape
    tm = ctx.shape[1]
    nx = l // tm
    nblk8 = l // V7X_SUBLANES
    r8 = tm // V7X_SUBLANES
    add_pos = pos is not None
    full = lambda shape: pl.BlockSpec(shape, lambda bi, i: (0,) * len(shape))
    xi = lambda i: jnp.minimum(i, nx - 1)
    prev8 = lambda i: jnp.clip(i * r8 - 1, 0, nblk8 - 1)
    next8 = lambda i: jnp.minimum((i + 1) * r8, nblk8 - 1)
    in_specs = [pl.BlockSpec((1, tm, d), lambda bi, i: (bi, xi(i), 0)),
                pl.BlockSpec((1, V7X_SUBLANES, d), lambda bi, i: (bi, prev8(i), 0)),
                pl.BlockSpec((1, V7X_SUBLANES, d), lambda bi, i: (bi, next8(i), 0)),
                pl.BlockSpec((1, tm, d), lambda bi, i: (bi, 0, 0))]
    args = [x, x, x, ctx]
    if add_pos:
        in_specs += [pl.BlockSpec((tm, d), lambda bi, i: (xi(i), 0)),
                     pl.BlockSpec((V7X_SUBLANES, d), lambda bi, i: (prev8(i), 0)),
                     pl.BlockSpec((V7X_SUBLANES, d), lambda bi, i: (next8(i), 0))]
        args += [pos, pos, pos]
    mod_spec = pl.BlockSpec((1, 6, d), lambda bi, i: (bi, 0, 0))
    params = (w_in, conv, wl, w0, a0, k_k, k_a, ones_bd)
    in_specs += [mod_spec, mod_spec] + [full(p.shape) for p in params]
    args += [mod_x, mod_c, *params]
    widths = (4 * GROUP_W, 6 * GROUP_W, GROUP_W, 2 * GROUP_W, GROUP_W)
    n = w_in.shape[1]
    vmem = 2 * (4 * tm * d * 4 + d * n * 2 + tm * sum(widths) * 4) + 3 * tm * d * 4 + 16 * tm * RKV_W * 4
    return pl.pallas_call(
        functools.partial(_proj_prep_kernel, tm=tm, nx=nx, add_pos=add_pos),
        grid=(b, nx + 1),
        in_specs=in_specs,
        out_specs=[pl.BlockSpec((1, tm, wd), lambda bi, i: (bi, i, 0)) for wd in widths],
        out_shape=[jax.ShapeDtypeStruct((b, l + tm, wd), F32) for wd in widths],
        compiler_params=_cparams(("parallel", "arbitrary"), vmem),
        name="proj_prep",
    )(*args)


SCAN_ROW_PITCH = 72


def _to_scan_kernel(xf_ref, xb_ref, o_ref, y_ref, *, nb, tt, step_major):
    rev = (tt - 1) - lax.broadcasted_iota(jnp.int32, (GROUP_W, tt), 1)
    for b in range(nb):
        xf = xf_ref[b].T
        xb = jnp.take_along_axis(xb_ref[b].T, rev, axis=1)
        for h in range(N_HEADS):
            y_ref[pl.ds((b * 2 * N_HEADS + h) * SCAN_ROW_PITCH, HEAD), :] = xf[h * HEAD:(h + 1) * HEAD, :]
            y_ref[pl.ds((b * 2 * N_HEADS + N_HEADS + h) * SCAN_ROW_PITCH, HEAD), :] = xb[h * HEAD:(h + 1) * HEAD, :]
    for k in range(HEAD):
        z = y_ref[pl.ds(k, nb * 2 * N_HEADS, stride=SCAN_ROW_PITCH), :]
        if step_major:
            o_ref[pl.ds(k, tt, stride=HEAD), :] = z.T
        else:
            o_ref[k] = z.T


def _to_scan_layout(x, n_lat, n_ctx, fwd_cols, bwd_cols, step_major=False):
    nb, t, c = x.shape
    tt = V7X_LANES
    assert nb * 2 * N_HEADS == V7X_LANES and c % GROUP_W == 0 and n_lat % tt == 0 and n_ctx % tt == 0
    nl, nc = n_lat // tt, n_ctx // tt
    n = nl + nc
    if step_major:
        out_spec = pl.BlockSpec((tt * HEAD, V7X_LANES), lambda j: (j, 0))
        out_shape = jax.ShapeDtypeStruct((t * HEAD, V7X_LANES), F32)
    else:
        out_spec = pl.BlockSpec((HEAD, tt, V7X_LANES), lambda j: (0, j, 0))
        out_shape = jax.ShapeDtypeStruct((HEAD, t, V7X_LANES), F32)
    out = pl.pallas_call(
        functools.partial(_to_scan_kernel, nb=nb, tt=tt, step_major=step_major),
        grid=(n,),
        in_specs=[pl.BlockSpec((nb, tt, GROUP_W), lambda j: (0, jnp.where(j < nc, nl + j, j - nc), fwd_cols)),
                  pl.BlockSpec((nb, tt, GROUP_W), lambda j: (0, n - 1 - j, bwd_cols))],
        out_specs=out_spec,
        out_shape=out_shape,
        scratch_shapes=[pltpu.VMEM((V7X_LANES * SCAN_ROW_PITCH, tt), F32)],
        compiler_params=_cparams(("parallel",), 2 * (nb * tt * 2 * GROUP_W * 4 + HEAD * tt * V7X_LANES * 4)
                                 + 3 * V7X_LANES * SCAN_ROW_PITCH * tt * 4),
        name="to_scan_layout",
    )(x, x)
    return out.reshape(t, HEAD, V7X_LANES) if step_major else out


def _from_scan_kernel(yf_ref, yb_ref, o_ref, s_ref, *, nb, tt):
    rev = (tt - 1) - lax.broadcasted_iota(jnp.int32, (V7X_LANES, tt), 1)
    fwd_row = (lax.broadcasted_iota(jnp.int32, (V7X_LANES, tt), 0) & N_HEADS) == 0
    for v in range(HEAD):
        zf = yf_ref[pl.ds(v, tt, stride=SCAN_ROW_PITCH), :].T
        zb = jnp.take_along_axis(yb_ref[pl.ds(v, tt, stride=SCAN_ROW_PITCH), :].T, rev, axis=1)
        s_ref[pl.ds(v, V7X_LANES, stride=SCAN_ROW_PITCH), :] = jnp.where(fwd_row, zf, zb)
    groups = 2 * N_HEADS
    for b in range(nb):
        grp = lambda g: s_ref[pl.ds((b * groups + g) * SCAN_ROW_PITCH, HEAD), :]
        rows = [grp(h) + grp(N_HEADS + h) for h in range(N_HEADS)]
        o_ref[b] = jnp.concatenate(rows, axis=0).T


def _from_scan_layout(y, nb, n_lat, n_ctx):
    t, pitch, _ = y.shape
    assert pitch == SCAN_ROW_PITCH
    tt = V7X_LANES
    nl, nc = n_lat // tt, n_ctx // tt
    n = nl + nc
    y2d = y.reshape(t * pitch, V7X_LANES)
    blk = pl.BlockSpec((tt * pitch, V7X_LANES), lambda j: (jnp.where(j < nl, j + nc, j - nl), 0))
    rblk = pl.BlockSpec((tt * pitch, V7X_LANES), lambda j: (n - 1 - j, 0))
    return pl.pallas_call(
        functools.partial(_from_scan_kernel, nb=nb, tt=tt),
        grid=(n,),
        in_specs=[blk, rblk],
        out_specs=pl.BlockSpec((nb, tt, GROUP_W), lambda j: (0, j, 0)),
        out_shape=jax.ShapeDtypeStruct((nb, t, GROUP_W), F32),
        scratch_shapes=[pltpu.VMEM((V7X_LANES * SCAN_ROW_PITCH, tt), F32)],
        compiler_params=_cparams(("parallel",), 2 * (2 * tt * pitch * V7X_LANES * 4 + nb * tt * 2 * GROUP_W * 4)
                                 + 3 * V7X_LANES * SCAN_ROW_PITCH * tt * 4),
        name="from_scan_layout",
    )(y2d, y2d)


SCAN_SLAB_PAD = 8


def _scan_kernel(r_ref, v_ref, kk_ref, lw_ref, b_ref, kd_ref, y_ref, h_ref, s_ref, g_ref, *, tb):
    @pl.when(pl.program_id(0) == 0)
    def _():
        h_ref[...] = jnp.zeros_like(h_ref)

    tile = (V7X_SUBLANES, V7X_LANES)
    R_G, KK_G, B_G, KD_G = range(4)

    sub = lax.broadcasted_iota(jnp.int32, tile, 0)

    def scale(k, carry):
        lw = lw_ref[k]
        parts = []
        total = jnp.zeros(tile, F32)
        for j in range(tb // V7X_SUBLANES):
            x = lw[j * V7X_SUBLANES:(j + 1) * V7X_SUBLANES, :]
            for s in (1, 2, 4):
                x = x + jnp.where(sub >= s, pltpu.roll(x, s, 0), 0.0)
            x = x + total
            total = jnp.broadcast_to(x[V7X_SUBLANES - 1:V7X_SUBLANES, :], tile)
            parts.append(x)
        cum = jnp.concatenate(parts, axis=0)
        g = jnp.exp(cum)
        g_inv = jnp.exp(-cum)
        s_ref[R_G, k, 0:tb] = r_ref[k] * g
        s_ref[KK_G, k, 0:tb] = kk_ref[k] * jnp.exp(cum - lw)
        s_ref[B_G, k, 0:tb] = b_ref[k] * g_inv
        s_ref[KD_G, k, 0:tb] = kd_ref[k] * g_inv
        g_ref[pl.ds(k, 1), :] = g[tb - 1:tb, :]
        return carry

    lax.fori_loop(0, HEAD, scale, 0)

    u0 = jnp.zeros((HEAD, V7X_LANES), F32)
    for k in range(HEAD):
        u0 = u0 + s_ref[KK_G, k, 0:1, :] * h_ref[k]

    def step(t, u):
        tn = jnp.minimum(t + 1, tb - 1)
        vv = v_ref[t]
        y = jnp.zeros((HEAD, V7X_LANES), F32)
        un = jnp.zeros((HEAD, V7X_LANES), F32)
        for k in range(HEAD):
            hn = h_ref[k] - s_ref[B_G, k, pl.ds(t, 1), :] * u + s_ref[KD_G, k, pl.ds(t, 1), :] * vv
            h_ref[k] = hn
            y = y + s_ref[R_G, k, pl.ds(t, 1), :] * hn
            un = un + s_ref[KK_G, k, pl.ds(tn, 1), :] * hn
        y_ref[t, 0:HEAD] = y
        return un

    y_ref[:, HEAD:, :] = jnp.zeros((tb, SCAN_ROW_PITCH - HEAD, V7X_LANES), F32)
    lax.fori_loop(0, tb, step, u0)
    for k in range(HEAD):
        h_ref[k] = h_ref[k] * g_ref[k:k + 1, :]


def _rwkv_scan(r, v, kk, w, b, kd, tb):
    t_tot = r.shape[1]
    kspec = pl.BlockSpec((HEAD, tb, V7X_LANES), lambda i: (0, i, 0))
    sspec = pl.BlockSpec((tb, HEAD, V7X_LANES), lambda i: (i, 0, 0))
    blk = tb * HEAD * V7X_LANES * 4
    return pl.pallas_call(
        functools.partial(_scan_kernel, tb=tb),
        grid=(t_tot // tb,),
        in_specs=[kspec, sspec, kspec, kspec, kspec, kspec],
        out_specs=pl.BlockSpec((tb, SCAN_ROW_PITCH, V7X_LANES), lambda i: (i, 0, 0)),
        out_shape=jax.ShapeDtypeStruct((t_tot, SCAN_ROW_PITCH, V7X_LANES), F32),
        scratch_shapes=[pltpu.VMEM((HEAD, HEAD, V7X_LANES), F32),
                        pltpu.VMEM((4, HEAD, tb + SCAN_SLAB_PAD, V7X_LANES), F32),
                        pltpu.VMEM((HEAD, V7X_LANES), F32)],
        compiler_params=_cparams(("arbitrary",), 14 * blk + 5 * blk + 3 * HEAD * HEAD * V7X_LANES * 4),
        name="rwkv_scan",
    )(r, v, kk, w, b, kd)


def _readout(y, r, v, kd2, gate, r_k, gn_g, gn_b, ones_bd):
    inv = 1.0 / HEAD
    kd = kd2[:, 0:GROUP_W] + kd2[:, GROUP_W:2 * GROUP_W]
    mu = _seg_sum(y, ones_bd) * inv
    yc = y - mu
    var = _seg_sum(yc * yc, ones_bd) * inv
    yn = yc * lax.rsqrt(var + GN_EPS)
    bonus = _seg_sum(r * kd * r_k, ones_bd) * v
    return (yn * gn_g + gn_b + bonus) * gate


def _pool_kernel(z_ref, w_ref, s_ref, o_ref, *, l):
    z = z_ref[0]
    row = lax.broadcasted_iota(jnp.int32, z.shape, 0)
    lane = lax.broadcasted_iota(jnp.int32, z.shape, 1)

    def shifted(x, o):
        y = pltpu.roll(x, (-o) % l, 0)
        return jnp.where((row + o >= 0) & (row + o < l), y, 0.0)

    def count(wd):
        lo = jnp.maximum(row - wd // 2, 0)
        hi = jnp.minimum(row + wd - wd // 2, l)
        return (hi - lo).astype(F32)

    s2 = shifted(z, -1) + z
    pooled = s2 / count(2)
    s4 = shifted(s2, -1) + shifted(s2, 1)
    pooled = jnp.where(lane >= 64, s4 / count(4), pooled)
    s8 = shifted(s4, -2) + shifted(s4, 2)
    pooled = jnp.where(lane >= 128, s8 / count(8), pooled)
    s16 = shifted(s8, -4) + shifted(s8, 4)
    pooled = jnp.where(lane >= 192, s16 / count(16), pooled)
    d = pooled - z
    o_ref[0] = jnp.dot(d.astype(BF16), w_ref[...], preferred_element_type=F32) * s_ref[...]


def _pool_mixer(z, w_bd, scale, row0, l):
    b, _, c = z.shape
    spec = pl.BlockSpec((1, l, c), lambda bi: (bi, 0, 0))
    return pl.pallas_call(
        functools.partial(_pool_kernel, l=l),
        grid=(b,),
        in_specs=[pl.BlockSpec((1, l, c), lambda bi: (bi, row0 // l, 0)),
                  pl.BlockSpec(w_bd.shape, lambda bi: (0, 0)), pl.BlockSpec(scale.shape, lambda bi: (0, 0))],
        out_specs=spec,
        out_shape=jax.ShapeDtypeStruct((b, l, c), F32),
        compiler_params=_cparams(("parallel",), 20 * l * c * 4),
        name="pool_mixer",
    )(z, w_bd, scale)


def _gelu_tanh(x):
    return 0.5 * x * (1.0 + jnp.tanh(math.sqrt(2.0 / math.pi) * (x + 0.044715 * (x * x * x))))


def _gmlp(z, ln_g, ln_b, ws_ref, bias, ones_bd):
    inv = 1.0 / HEAD
    tm = z.shape[0]
    u = _gelu_tanh(z[:, 0:GROUP_W])
    v = _gelu_tanh(z[:, GROUP_W:2 * GROUP_W])
    mu = _seg_sum(v, ones_bd) * inv
    vc = v - mu
    var = _seg_sum(vc * vc, ones_bd) * inv
    vn = (vc * lax.rsqrt(var + LN_EPS) * ln_g + ln_b).astype(BF16)
    lane = lax.broadcasted_iota(jnp.int32, (CHUNK, GROUP_W), 1)
    out = []
    for c in range(tm // CHUNK):
        vchunk = vn[c * CHUNK:(c + 1) * CHUNK, :]
        sv = bias
        for g in range(4):
            part = jnp.dot(ws_ref[g], vchunk, preferred_element_type=F32)
            sv = sv + jnp.where((lane >= g * HEAD) & (lane < (g + 1) * HEAD), part, 0.0)
        out.append(u[c * CHUNK:(c + 1) * CHUNK, :] * sv)
    return jnp.concatenate(out, axis=0)


def _fnet_kernel(z_ref, cs_ref, cc_ref, sc_ref, w_ref, b_ref, o_ref, zcs_ref, *, l, scale, nbg):
    c = z_ref.shape[2]

    @pl.when(pl.program_id(1) == 0)
    def _():
        for j in range(nbg):
            zb = z_ref[j].astype(BF16)
            zcs_ref[0:l, j * c:(j + 1) * c] = jnp.dot(zb, cc_ref[...], preferred_element_type=F32).astype(BF16)
            zcs_ref[l:2 * l, j * c:(j + 1) * c] = jnp.dot(zb, sc_ref[...], preferred_element_type=F32).astype(BF16)

    f = jnp.dot(cs_ref[...], zcs_ref[...], preferred_element_type=F32) * scale
    for j in range(nbg):
        o_ref[j] = jnp.dot(f[:, j * c:(j + 1) * c].astype(BF16), w_ref[...], preferred_element_type=F32) + b_ref[...]


def _fourier_mixer(z, cs, cc_bd, sc_bd, w_bd, bias, tm, row0, l, nbg=4):
    b, _, c = z.shape
    assert b % nbg == 0 and row0 % l == 0
    full = lambda shape: pl.BlockSpec(shape, lambda bi, i: (0,) * len(shape))
    scale = 1.0 / math.sqrt(l * HEAD)
    vmem = 2 * (nbg * l * c * 4 + tm * 2 * l * 2 + nbg * tm * c * 4) + 2 * l * nbg * c * 2 + 4 * l * c * 4
    return pl.pallas_call(
        functools.partial(_fnet_kernel, l=l, scale=scale, nbg=nbg),
        grid=(b // nbg, l // tm),
        in_specs=[pl.BlockSpec((nbg, l, c), lambda bi, i: (bi, row0 // l, 0)),
                  pl.BlockSpec((tm, 2 * l), lambda bi, i: (i, 0)),
                  full(cc_bd.shape), full(sc_bd.shape), full(w_bd.shape), full(bias.shape)],
        out_specs=pl.BlockSpec((nbg, tm, c), lambda bi, i: (bi, i, 0)),
        out_shape=jax.ShapeDtypeStruct((b, l, c), F32),
        scratch_shapes=[pltpu.VMEM((2 * l, nbg * c), BF16)],
        compiler_params=_cparams(("parallel", "arbitrary"), vmem),
        name="fourier_mixer",
    )(z, cs, cc_bd, sc_bd, w_bd, bias)


def _outproj_kernel(*refs, alpha, add_pos):
    (y_ref, r_ref, v_ref, kd_ref, gate_ref, zg_ref, p_ref, f_ref, x_ref), refs = refs[:9], refs[9:]
    if add_pos:
        pos_ref, refs = refs[0], refs[1:]
        x = x_ref[0] + pos_ref[...]
    else:
        x = x_ref[0]
    (mod_ref, rk_ref, gg_ref, gb_ref, ones_ref, lng_ref, lnb_ref, ws_ref, bias_ref, w_ref, lg_ref, lb_ref,
     o_ref) = refs
    ones_bd = ones_ref[...]
    a = _readout(y_ref[0], r_ref[0], v_ref[0], kd_ref[0], gate_ref[0], rk_ref[...], gg_ref[...], gb_ref[...], ones_bd)
    g = _gmlp(zg_ref[0], lng_ref[...], lnb_ref[...], ws_ref, bias_ref[...], ones_bd)
    mix = None
    for j, part in enumerate((a, p_ref[0], g, f_ref[0])):
        prod = jnp.dot(part.astype(BF16), w_ref[j * GROUP_W:(j + 1) * GROUP_W, :], preferred_element_type=F32)
        mix = prod if mix is None else mix + prod
    gate1 = mod_ref[0, 2:3, :]
    o_ref[0] = _layer_norm(alpha * x + gate1 * mix, lg_ref[...], lb_ref[...])


def _mix_out_proj(y, shared, pair, z_gmlp, row0, pooled, fourier, x, pos, mod, params, alpha, tm):
    b, l, d = x.shape
    add_pos = pos is not None
    off = row0 // tm
    comb = lambda width, col: pl.BlockSpec((1, tm, width), lambda i, bi: (bi, i + off, col))
    s1 = pl.BlockSpec((1, tm, GROUP_W), lambda i, bi: (bi, i, 0))
    xspec = pl.BlockSpec((1, tm, d), lambda i, bi: (bi, i, 0))
    full = lambda shape: pl.BlockSpec(shape, lambda i, bi: (0,) * len(shape))
    in_specs = [comb(GROUP_W, 0), comb(GROUP_W, SH_R), comb(GROUP_W, SH_V), comb(2 * GROUP_W, PR_KD // 2),
                comb(GROUP_W, SH_GATE), comb(2 * GROUP_W, 0), s1, s1, xspec]
    args = [y, shared, shared, pair, shared, z_gmlp, pooled, fourier, x]
    if add_pos:
        in_specs.append(pl.BlockSpec((tm, d), lambda i, bi: (i, 0)))
        args.append(pos)
    in_specs += [pl.BlockSpec((1, 6, d), lambda i, bi: (bi, 0, 0))] + [full(p.shape) for p in params]
    args += [mod, *params]
    vmem = 2 * (10 * tm * GROUP_W * 4 + 3 * tm * d * 4 + d * d * 2) + 6 * tm * d * 4 + 24 * tm * GROUP_W * 4
    return pl.pallas_call(
        functools.partial(_outproj_kernel, alpha=alpha, add_pos=add_pos),
        grid=(l // tm, b),
        in_specs=in_specs,
        out_specs=xspec,
        out_shape=jax.ShapeDtypeStruct((b, l, d), F32),
        compiler_params=_cparams(("parallel", "parallel"), vmem),
        name="mix_out_proj_ln",
    )(*args)


def _ffn_kernel(x_ref, mod_ref, wg_ref, wu_ref, w2_ref, lg_ref, lb_ref, o_ref, *, alpha, f_chunks):
    x = x_ref[0]
    sh = mod_ref[0, 3:4, :]
    sc = mod_ref[0, 4:5, :]
    h = (x * (1.0 + sc) + sh).astype(BF16)
    acc = None
    for lo, hi in f_chunks:
        gate = jnp.dot(h, wg_ref[:, lo:hi], preferred_element_type=F32)
        up = jnp.dot(h, wu_ref[:, lo:hi], preferred_element_type=F32)
        act = (_silu(gate) * up).astype(BF16)
        part = jnp.dot(act, w2_ref[lo:hi, :], preferred_element_type=F32)
        acc = part if acc is None else acc + part
    gate2 = mod_ref[0, 5:6, :]
    o_ref[0] = _layer_norm(alpha * x + gate2 * acc, lg_ref[...], lb_ref[...])


def _ffn(x, mod, w1, w2, ln_g, ln_b, alpha, tm):
    b, l, d = x.shape
    f = w2.shape[0]
    mxu_n = 256
    half = (f // mxu_n + 1) // 2 * mxu_n
    f_chunks = ((0, half), (half, f))
    xspec = pl.BlockSpec((1, tm, d), lambda bi, i: (bi, i, 0))
    full = lambda shape: pl.BlockSpec(shape, lambda bi, i: (0,) * len(shape))
    once = pl.Buffered(1)
    vmem = 4 * tm * d * 4 + 3 * d * f * 2 + 3 * tm * d * 4 + 4 * tm * half * 4
    return pl.pallas_call(
        functools.partial(_ffn_kernel, alpha=alpha, f_chunks=f_chunks),
        grid=(b, l // tm),
        in_specs=[xspec,
                  pl.BlockSpec((1, 6, d), lambda bi, i: (bi, 0, 0)),
                  pl.BlockSpec((d, f), lambda bi, i: (0, 0), pipeline_mode=once),
                  pl.BlockSpec((d, f), lambda bi, i: (0, 1), pipeline_mode=once),
                  pl.BlockSpec((f, d), lambda bi, i: (0, 0), pipeline_mode=once),
                  full(ln_g.shape), full(ln_b.shape)],
        out_specs=xspec,
        out_shape=jax.ShapeDtypeStruct((b, l, d), F32),
        compiler_params=_cparams(("parallel", "parallel"), vmem),
        name="ffn_ln",
    )(x, mod, w1, w1, w2, ln_g, ln_b)


def _pos_embed(n_tok, dim):
    rows = n_tok // GRID_W
    row, col = jnp.meshgrid(jnp.arange(rows, dtype=F32), jnp.arange(GRID_W, dtype=F32), indexing='ij')
    quarter = dim // 4
    freqs = jnp.exp(-math.log(10000.0) * jnp.arange(quarter, dtype=F32) / quarter)

    def enc(p):
        ang = p.reshape(-1, 1) * freqs[None, :]
        return jnp.concatenate([jnp.sin(ang), jnp.cos(ang)], -1)

    return jnp.concatenate([enc(row), enc(col)], -1)


def _block_diag(blocks):
    g, n, m = blocks.shape
    eye = jnp.eye(g, dtype=blocks.dtype)
    return (eye[:, None, :, None] * blocks[:, :, None, :]).reshape(g * n, g * m)


def _dft_tables(n):
    def direct(rows, cols, period):
        ang = ((rows[:, None] * cols[None, :]) % period).astype(F32) * (2.0 * math.pi / period)
        return jnp.cos(ang), jnp.sin(ang)

    idx = jnp.arange(n, dtype=jnp.int32)
    if n <= 1024:
        return direct(idx, idx, n)
    lo = 64
    ca, sa = direct(jnp.arange(n // lo, dtype=jnp.int32), idx, n // lo)
    cb, sb = direct(jnp.arange(lo, dtype=jnp.int32), idx, n)
    ca, sa, cb, sb = ca[:, None, :], sa[:, None, :], cb[None], sb[None]
    return (ca * cb - sa * sb).reshape(n, n), (sa * cb + ca * sb).reshape(n, n)


def kernel(x, c, ctx, c_ctx, w_mod, b_mod, w_in, rkv_conv, decay_w0, decay_w2, iclr_a0, iclr_a2,
           gate_g2, k_k, k_a, r_k, gn_g, gn_b, pool_w, pool_scale, gmlp_ln_g, gmlp_ln_b, gmlp_ws,
           gmlp_bs, fnet_w, fnet_b, w_out, ln1_g, ln1_b, ln2_g, ln2_b, ffn_w1, ffn_w2):
    bsz, seq, d = x.shape
    ctx_len = ctx.shape[1]
    depth = w_in.shape[0]
    alpha = (2 * depth) ** 0.25
    assert bsz * N_HEADS * 2 == V7X_LANES, "scan layout packs (direction, batch, head) on the 128 lanes"

    pos = _pos_embed(seq, d)
    ones_bd = _block_diag(jnp.ones((N_HEADS, HEAD, HEAD), F32)).astype(BF16)
    cc64, ss64 = _dft_tables(HEAD)
    cc_bd = _block_diag(jnp.tile(cc64[None], (4, 1, 1))).astype(BF16)
    sc_bd = _block_diag(jnp.tile(ss64[None], (4, 1, 1))).astype(BF16)

    def dft_rows(n):
        cn, sn = _dft_tables(n)
        return jnp.concatenate([cn, -sn], axis=1).astype(BF16)

    cs_x = dft_rows(seq)
    cs_c = dft_rows(ctx_len)

    pad_rows = (-(bsz + 1)) % V7X_SUBLANES
    c_all = jnp.concatenate([c, c_ctx[None], jnp.zeros((pad_rows, d), F32)], axis=0)

    tm_x = 512
    tm_c = ctx_len
    tb = 64

    xs, cs = x, ctx
    for l in range(depth):
        last = l == depth - 1
        first = l == 0
        m = _modulation(c_all, w_mod[l].astype(BF16), b_mod[l])
        mod_x = m[:bsz].reshape(bsz, 6, d)
        mod_c = jnp.broadcast_to(m[bsz].reshape(1, 6, d), (bsz, 6, d))

        wl = jnp.zeros((LORA_W, 5 * GROUP_W), F32)
        wl = wl.at[0:64, 0:256].set(decay_w2[l, 0]).at[64:128, 256:512].set(decay_w2[l, 1])
        wl = wl.at[128:192, 512:768].set(iclr_a2[l, 0]).at[192:256, 768:1024].set(iclr_a2[l, 1])
        wl = wl.at[256:384, 1024:1280].set(gate_g2[l]).astype(BF16)
        prep_args = (rkv_conv[l], wl, decay_w0[l], iclr_a0[l], k_k[l].reshape(1, -1), k_a[l].reshape(1, -1),
                     ones_bd)
        shared, pair, z_pool, z_gmlp, z_fnet = _proj_prep(xs, cs, pos if first else None, mod_x, mod_c,
                                                          w_in[l].astype(BF16), *prep_args)
        to_scan = lambda a, col_f, col_b, **kw: _to_scan_layout(a, seq, ctx_len, col_f, col_b, **kw)
        y_s = _rwkv_scan(to_scan(shared, SH_R, SH_R), to_scan(shared, SH_V, SH_V, step_major=True),
                         to_scan(shared, SH_KK, SH_KK), to_scan(pair, PR_LW, PR_LW + 1),
                         to_scan(pair, PR_B, PR_B + 1), to_scan(pair, PR_KD, PR_KD + 1), tb)
        y_nat = _from_scan_layout(y_s, bsz, seq, ctx_len)

        pool_bd = _block_diag(pool_w[l]).astype(BF16)
        fnet_bd = _block_diag(fnet_w[l]).astype(BF16)
        ws_b = gmlp_ws[l].astype(BF16)
        bias_tile = jnp.repeat(gmlp_bs[l].T, HEAD, axis=1)
        row = lambda a: a.reshape(1, -1)

        mix_params = (row(r_k[l]), row(gn_g[l]), row(gn_b[l]), ones_bd, row(gmlp_ln_g[l]), row(gmlp_ln_b[l]), ws_b,
                      bias_tile, w_out[l].astype(BF16), row(ln1_g[l]), row(ln1_b[l]))

        def mix_sublayer(row0, n_rows, stream, pos_tab, mod, cs_tab, tm, tmf):
            pooled = _pool_mixer(z_pool, pool_bd, row(pool_scale[l]), row0, n_rows)
            fourier = _fourier_mixer(z_fnet, cs_tab, cc_bd, sc_bd, fnet_bd, row(fnet_b[l]), tmf, row0, n_rows)
            return _mix_out_proj(y_nat, shared, pair, z_gmlp, row0, pooled, fourier, stream, pos_tab, mod,
                                 mix_params, alpha, tm)

        w1_l = ffn_w1[l].astype(BF16)
        w2_l = ffn_w2[l].astype(BF16)
        xs = mix_sublayer(0, seq, xs, pos if first else None, mod_x, cs_x, tm_x, 256)
        xs = _ffn(xs, mod_x, w1_l, w2_l, row(ln2_g[l]), row(ln2_b[l]), alpha, tm_x)
        if not last:
            cs = mix_sublayer(seq, ctx_len, cs, None, mod_c, cs_c, tm_c, tm_c)
            cs = _ffn(cs, mod_c, w1_l, w2_l, row(ln2_g[l]), row(ln2_b[l]), alpha, tm_c)
    return xs
```

```python
import functools
import math

import jax
import jax.numpy as jnp
from jax import lax
from jax.experimental import pallas as pl
from jax.experimental.pallas import tpu as pltpu

F32 = jnp.float32
BF16 = jnp.bfloat16

V7X_LANES = 128
V7X_SUBLANES = 8
V7X_VMEM_BYTES = 64 * 1024 * 1024
V7X_VMEM_LIMIT_CAP = 56 * 1024 * 1024

GRID_W = 64
HEAD = 64
GROUP_W = 256
N_HEADS = GROUP_W // HEAD
LORA_W = 384
POOL_WINDOWS = (2, 4, 8, 16)
CHUNK = 128
LN_EPS = 1e-5
GN_EPS = 64e-5


def _cparams(sem, vmem_bytes):
    limit = int(min(max(vmem_bytes, 16 * 1024 * 1024), V7X_VMEM_LIMIT_CAP))
    return pltpu.CompilerParams(dimension_semantics=sem, vmem_limit_bytes=limit)


def _silu(x):
    return x * jax.nn.sigmoid(x)


def _seg_sum(x, ones_bd):
    hi = x.astype(BF16)
    lo = (x - hi.astype(F32)).astype(BF16)
    return (jnp.dot(hi, ones_bd, preferred_element_type=F32)
            + jnp.dot(lo, ones_bd, preferred_element_type=F32))


def _layer_norm(z, g, b):
    mu = jnp.mean(z, axis=-1, keepdims=True)
    zc = z - mu
    var = jnp.mean(zc * zc, axis=-1, keepdims=True)
    return zc * lax.rsqrt(var + LN_EPS) * g + b


def _mod_kernel(c_ref, w_ref, b_ref, o_ref):
    s = _silu(c_ref[...])
    o_ref[...] = jnp.dot(s.astype(BF16), w_ref[...], preferred_element_type=F32) + b_ref[...]


def _modulation(c_all, w_mod, b_mod):
    rows, d = c_all.shape
    n = w_mod.shape[1]
    tn = 1024
    return pl.pallas_call(
        _mod_kernel,
        grid=(n // tn,),
        in_specs=[pl.BlockSpec((rows, d), lambda j: (0, 0)),
                  pl.BlockSpec((d, tn), lambda j: (0, j)),
                  pl.BlockSpec((1, tn), lambda j: (0, j))],
        out_specs=pl.BlockSpec((rows, tn), lambda j: (0, j)),
        out_shape=jax.ShapeDtypeStruct((rows, n), F32),
        compiler_params=_cparams(("parallel",), 4 * d * tn * 2),
        name="modulation",
    )(c_all, w_mod, b_mod.reshape(1, n))


SH_R, SH_V, SH_KK, SH_GATE = range(4)
PR_LW, PR_B, PR_KD = 0, 2, 4
RKV_W = 3 * GROUP_W


def _proj_prep_kernel(*refs, tm, nx, add_pos):
    (xx_ref, xp_ref, xn_ref, xc_ref), refs = refs[:4], refs[4:]
    xx, xp, xn = xx_ref[0], xp_ref[0], xn_ref[0]
    if add_pos:
        (pos_ref, pp_ref, pn_ref), refs = refs[:3], refs[3:]
        xx, xp, xn = xx + pos_ref[...], xp + pp_ref[...], xn + pn_ref[...]
    (modx_ref, modc_ref, w_ref, conv_ref, wl_ref, w0_ref, a0_ref, kk_ref, ka_ref, ones_ref,
     shared_o, pair_o, pool_o, gmlp_o, fnet_o) = refs
    i = pl.program_id(1)
    is_ctx = i >= nx
    x = jnp.where(is_ctx, xc_ref[0], xx)
    mod = jnp.where(is_ctx, modc_ref[0, 0:2, :], modx_ref[0, 0:2, :])
    adaln = lambda t: (t * (1.0 + mod[1:2]) + mod[0:1]).astype(BF16)
    proj = lambda t, lo, hi: jnp.dot(t, w_ref[:, lo:hi], preferred_element_type=F32)
    h = adaln(x)
    z = proj(h, 0, RKV_W)
    lo = proj(h, RKV_W, RKV_W + LORA_W)
    off = RKV_W + LORA_W
    pool_o[0] = proj(h, off, off + GROUP_W)
    gmlp_o[0] = proj(h, off + GROUP_W, off + 3 * GROUP_W)
    fnet_o[0] = proj(h, off + 3 * GROUP_W, off + 4 * GROUP_W)

    has_prev = jnp.logical_and(i > 0, i < nx).astype(F32)
    has_next = (i < nx - 1).astype(F32)
    prev_row = proj(adaln(xp), 0, RKV_W)[V7X_SUBLANES - 1:V7X_SUBLANES, :] * has_prev
    next_row = proj(adaln(xn), 0, RKV_W)[0:1, :] * has_next
    row = lax.broadcasted_iota(jnp.int32, z.shape, 0)
    zm1 = jnp.where(row == 0, prev_row, pltpu.roll(z, 1, 0))
    zp1 = jnp.where(row == tm - 1, next_row, pltpu.roll(z, tm - 1, 0))
    c = zm1 * conv_ref[0:1, :] + z * conv_ref[1:2, :] + zp1 * conv_ref[2:3, :]
    r = c[:, 0:GROUP_W]
    k = c[:, GROUP_W:2 * GROUP_W]
    v = c[:, 2 * GROUP_W:3 * GROUP_W]

    col = lax.broadcasted_iota(jnp.int32, lo.shape, 1)
    act = jnp.where(col < 128, jnp.tanh(lo), jnp.where(col < 256, lo, jax.nn.sigmoid(lo)))
    pre = jnp.dot(act.astype(BF16), wl_ref[...], preferred_element_type=F32)

    ones_bd = ones_ref[...]
    kk = k * kk_ref[...]
    nrm = jnp.sqrt(_seg_sum(kk * kk, ones_bd))
    kk = kk / jnp.maximum(nrm, 1e-12)
    blk = lambda j: slice(j * GROUP_W, (j + 1) * GROUP_W)
    for j, val in ((SH_R, r), (SH_V, v), (SH_KK, kk), (SH_GATE, pre[:, 4 * GROUP_W:5 * GROUP_W])):
        shared_o[0, :, blk(j)] = val
    ka = ka_ref[...]
    for d in range(2):
        xw = w0_ref[d:d + 1, :] + pre[:, blk(d)]
        pair_o[0, :, blk(PR_LW + d)] = -math.exp(-0.5) * jax.nn.sigmoid(xw)
        a = jax.nn.sigmoid(a0_ref[d:d + 1, :] + pre[:, blk(2 + d)])
        pair_o[0, :, blk(PR_KD + d)] = k * (1.0 + (a - 1.0) * ka)
        pair_o[0, :, blk(PR_B + d)] = kk * a


def _proj_prep(x, ctx, pos, mod_x, mod_c, w_in, conv, wl, w0, a0, k_k, k_a, ones_bd):
    b, l, d = x.shape
    tm = ctx.shape[1]
    nx = l // tm
    nblk8 = l // V7X_SUBLANES
    r8 = tm // V7X_SUBLANES
    add_pos = pos is not None
    full = lambda shape: pl.BlockSpec(shape, lambda bi, i: (0,) * len(shape))
    xi = lambda i: jnp.minimum(i, nx - 1)
    prev8 = lambda i: jnp.clip(i * r8 - 1, 0, nblk8 - 1)
    next8 = lambda i: jnp.minimum((i + 1) * r8, nblk8 - 1)
    in_specs = [pl.BlockSpec((1, tm, d), lambda bi, i: (bi, xi(i), 0)),
                pl.BlockSpec((1, V7X_SUBLANES, d), lambda bi, i: (bi, prev8(i), 0)),
                pl.BlockSpec((1, V7X_SUBLANES, d), lambda bi, i: (bi, next8(i), 0)),
                pl.BlockSpec((1, tm, d), lambda bi, i: (bi, 0, 0))]
    args = [x, x, x, ctx]
    if add_pos:
        in_specs += [pl.BlockSpec((tm, d), lambda bi, i: (xi(i), 0)),
                     pl.BlockSpec((V7X_SUBLANES, d), lambda bi, i: (prev8(i), 0)),
                     pl.BlockSpec((V7X_SUBLANES, d), lambda bi, i: (next8(i), 0))]
        args += [pos, pos, pos]
    mod_spec = pl.BlockSpec((1, 6, d), lambda bi, i: (bi, 0, 0))
    params = (w_in, conv, wl, w0, a0, k_k, k_a, ones_bd)
    in_specs += [mod_spec, mod_spec] + [full(p.shape) for p in params]
    args += [mod_x, mod_c, *params]
    widths = (4 * GROUP_W, 6 * GROUP_W, GROUP_W, 2 * GROUP_W, GROUP_W)
    n = w_in.shape[1]
    vmem = 2 * (4 * tm * d * 4 + d * n * 2 + tm * sum(widths) * 4) + 3 * tm * d * 4 + 16 * tm * RKV_W * 4
    return pl.pallas_call(
        functools.partial(_proj_prep_kernel, tm=tm, nx=nx, add_pos=add_pos),
        grid=(b, nx + 1),
        in_specs=in_specs,
        out_specs=[pl.BlockSpec((1, tm, wd), lambda bi, i: (bi, i, 0)) for wd in widths],
        out_shape=[jax.ShapeDtypeStruct((b, l + tm, wd), F32) for wd in widths],
        compiler_params=_cparams(("parallel", "arbitrary"), vmem),
        name="proj_prep",
    )(*args)


SCAN_ROW_PITCH = 72


def _to_scan_kernel(xf_ref, xb_ref, o_ref, y_ref, *, nb, tt, step_major):
    rev = (tt - 1) - lax.broadcasted_iota(jnp.int32, (GROUP_W, tt), 1)
    for b in range(nb):
        xf = xf_ref[b].T
        xb = jnp.take_along_axis(xb_ref[b].T, rev, axis=1)
        for h in range(N_HEADS):
            y_ref[pl.ds((b * 2 * N_HEADS + h) * SCAN_ROW_PITCH, HEAD), :] = xf[h * HEAD:(h + 1) * HEAD, :]
            y_ref[pl.ds((b * 2 * N_HEADS + N_HEADS + h) * SCAN_ROW_PITCH, HEAD), :] = xb[h * HEAD:(h + 1) * HEAD, :]
    for k in range(HEAD):
        z = y_ref[pl.ds(k, nb * 2 * N_HEADS, stride=SCAN_ROW_PITCH), :]
        if step_major:
            o_ref[pl.ds(k, tt, stride=HEAD), :] = z.T
        else:
            o_ref[k] = z.T


def _to_scan_layout(x, n_lat, n_ctx, fwd_cols, bwd_cols, step_major=False):
    nb, t, c = x.shape
    tt = V7X_LANES
    assert nb * 2 * N_HEADS == V7X_LANES and c % GROUP_W == 0 and n_lat % tt == 0 and n_ctx % tt == 0
    nl, nc = n_lat // tt, n_ctx // tt
    n = nl + nc
    if step_major:
        out_spec = pl.BlockSpec((tt * HEAD, V7X_LANES), lambda j: (j, 0))
        out_shape = jax.ShapeDtypeStruct((t * HEAD, V7X_LANES), F32)
    else:
        out_spec = pl.BlockSpec((HEAD, tt, V7X_LANES), lambda j: (0, j, 0))
        out_shape = jax.ShapeDtypeStruct((HEAD, t, V7X_LANES), F32)
    out = pl.pallas_call(
        functools.partial(_to_scan_kernel, nb=nb, tt=tt, step_major=step_major),
        grid=(n,),
        in_specs=[pl.BlockSpec((nb, tt, GROUP_W), lambda j: (0, jnp.where(j < nc, nl + j, j - nc), fwd_cols)),
                  pl.BlockSpec((nb, tt, GROUP_W), lambda j: (0, n - 1 - j, bwd_cols))],
        out_specs=out_spec,
        out_shape=out_shape,
        scratch_shapes=[pltpu.VMEM((V7X_LANES * SCAN_ROW_PITCH, tt), F32)],
        compiler_params=_cparams(("parallel",), 2 * (nb * tt * 2 * GROUP_W * 4 + HEAD * tt * V7X_LANES * 4)
                                 + 3 * V7X_LANES * SCAN_ROW_PITCH * tt * 4),
        name="to_scan_layout",
    )(x, x)
    return out.reshape(t, HEAD, V7X_LANES) if step_major else out


def _from_scan_kernel(yf_ref, yb_ref, o_ref, s_ref, *, nb, tt):
    rev = (tt - 1) - lax.broadcasted_iota(jnp.int32, (V7X_LANES, tt), 1)
    fwd_row = (lax.broadcasted_iota(jnp.int32, (V7X_LANES, tt), 0) & N_HEADS) == 0
    for v in range(HEAD):
        zf = yf_ref[pl.ds(v, tt, stride=SCAN_ROW_PITCH), :].T
        zb = jnp.take_along_axis(yb_ref[pl.ds(v, tt, stride=SCAN_ROW_PITCH), :].T, rev, axis=1)
        s_ref[pl.ds(v, V7X_LANES, stride=SCAN_ROW_PITCH), :] = jnp.where(fwd_row, zf, zb)
    groups = 2 * N_HEADS
    for b in range(nb):
        grp = lambda g: s_ref[pl.ds((b * groups + g) * SCAN_ROW_PITCH, HEAD), :]
        rows = [grp(h) + grp(N_HEADS + h) for h in range(N_HEADS)]
        o_ref[b] = jnp.concatenate(rows, axis=0).T


def _from_scan_layout(y, nb, n_lat, n_ctx):
    t, pitch, _ = y.shape
    assert pitch == SCAN_ROW_PITCH
    tt = V7X_LANES
    nl, nc = n_lat // tt, n_ctx // tt
    n = nl + nc
    y2d = y.reshape(t * pitch, V7X_LANES)
    blk = pl.BlockSpec((tt * pitch, V7X_LANES), lambda j: (jnp.where(j < nl, j + nc, j - nl), 0))
    rblk = pl.BlockSpec((tt * pitch, V7X_LANES), lambda j: (n - 1 - j, 0))
    return pl.pallas_call(
        functools.partial(_from_scan_kernel, nb=nb, tt=tt),
        grid=(n,),
        in_specs=[blk, rblk],
        out_specs=pl.BlockSpec((nb, tt, GROUP_W), lambda j: (0, j, 0)),
        out_shape=jax.ShapeDtypeStruct((nb, t, GROUP_W), F32),
        scratch_shapes=[pltpu.VMEM((V7X_LANES * SCAN_ROW_PITCH, tt), F32)],
        compiler_params=_cparams(("parallel",), 2 * (2 * tt * pitch * V7X_LANES * 4 + nb * tt * 2 * GROUP_W * 4)
                                 + 3 * V7X_LANES * SCAN_ROW_PITCH * tt * 4),
        name="from_scan_layout",
    )(y2d, y2d)


SCAN_SLAB_PAD = 8


def _scan_kernel(r_ref, v_ref, kk_ref, lw_ref, b_ref, kd_ref, y_ref, h_ref, s_ref, g_ref, *, tb):
    @pl.when(pl.program_id(0) == 0)
    def _():
        h_ref[...] = jnp.zeros_like(h_ref)

    tile = (V7X_SUBLANES, V7X_LANES)
    R_G, KK_G, B_G, KD_G = range(4)

    sub = lax.broadcasted_iota(jnp.int32, tile, 0)

    def scale(k, carry):
        lw = lw_ref[k]
        parts = []
        total = jnp.zeros(tile, F32)
        for j in range(tb // V7X_SUBLANES):
            x = lw[j * V7X_SUBLANES:(j + 1) * V7X_SUBLANES, :]
            for s in (1, 2, 4):
                x = x + jnp.where(sub >= s, pltpu.roll(x, s, 0), 0.0)
            x = x + total
            total = jnp.broadcast_to(x[V7X_SUBLANES - 1:V7X_SUBLANES, :], tile)
            parts.append(x)
        cum = jnp.concatenate(parts, axis=0)
        g = jnp.exp(cum)
        g_inv = jnp.exp(-cum)
        s_ref[R_G, k, 0:tb] = r_ref[k] * g
        s_ref[KK_G, k, 0:tb] = kk_ref[k] * jnp.exp(cum - lw)
        s_ref[B_G, k, 0:tb] = b_ref[k] * g_inv
        s_ref[KD_G, k, 0:tb] = kd_ref[k] * g_inv
        g_ref[pl.ds(k, 1), :] = g[tb - 1:tb, :]
        return carry

    lax.fori_loop(0, HEAD, scale, 0)

    u0 = jnp.zeros((HEAD, V7X_LANES), F32)
    for k in range(HEAD):
        u0 = u0 + s_ref[KK_G, k, 0:1, :] * h_ref[k]

    def step(t, u):
        tn = jnp.minimum(t + 1, tb - 1)
        vv = v_ref[t]
        y = jnp.zeros((HEAD, V7X_LANES), F32)
        un = jnp.zeros((HEAD, V7X_LANES), F32)
        for k in range(HEAD):
            hn = h_ref[k] - s_ref[B_G, k, pl.ds(t, 1), :] * u + s_ref[KD_G, k, pl.ds(t, 1), :] * vv
            h_ref[k] = hn
            y = y + s_ref[R_G, k, pl.ds(t, 1), :] * hn
            un = un + s_ref[KK_G, k, pl.ds(tn, 1), :] * hn
        y_ref[t, 0:HEAD] = y
        return un

    y_ref[:, HEAD:, :] = jnp.zeros((tb, SCAN_ROW_PITCH - HEAD, V7X_LANES), F32)
    lax.fori_loop(0, tb, step, u0)
    for k in range(HEAD):
        h_ref[k] = h_ref[k] * g_ref[k:k + 1, :]


def _rwkv_scan(r, v, kk, w, b, kd, tb):
    t_tot = r.shape[1]
    kspec = pl.BlockSpec((HEAD, tb, V7X_LANES), lambda i: (0, i, 0))
    sspec = pl.BlockSpec((tb, HEAD, V7X_LANES), lambda i: (i, 0, 0))
    blk = tb * HEAD * V7X_LANES * 4
    return pl.pallas_call(
        functools.partial(_scan_kernel, tb=tb),
        grid=(t_tot // tb,),
        in_specs=[kspec, sspec, kspec, kspec, kspec, kspec],
        out_specs=pl.BlockSpec((tb, SCAN_ROW_PITCH, V7X_LANES), lambda i: (i, 0, 0)),
        out_shape=jax.ShapeDtypeStruct((t_tot, SCAN_ROW_PITCH, V7X_LANES), F32),
        scratch_shapes=[pltpu.VMEM((HEAD, HEAD, V7X_LANES), F32),
                        pltpu.VMEM((4, HEAD, tb + SCAN_SLAB_PAD, V7X_LANES), F32),
                        pltpu.VMEM((HEAD, V7X_LANES), F32)],
        compiler_params=_cparams(("arbitrary",), 14 * blk + 5 * blk + 3 * HEAD * HEAD * V7X_LANES * 4),
        name="rwkv_scan",
    )(r, v, kk, w, b, kd)


def _readout(y, r, v, kd2, gate, r_k, gn_g, gn_b, ones_bd):
    inv = 1.0 / HEAD
    kd = kd2[:, 0:GROUP_W] + kd2[:, GROUP_W:2 * GROUP_W]
    mu = _seg_sum(y, ones_bd) * inv
    yc = y - mu
    var = _seg_sum(yc * yc, ones_bd) * inv
    yn = yc * lax.rsqrt(var + GN_EPS)
    bonus = _seg_sum(r * kd * r_k, ones_bd) * v
    return (yn * gn_g + gn_b + bonus) * gate


POOL_HALO = 8


def _pool(z, prev_rows, next_rows, t0, seq_len, w_bd, scale):
    tm = z.shape[0]
    n = tm + 2 * POOL_HALO
    e = jnp.concatenate([prev_rows, z, next_rows], axis=0)
    shifted = lambda x, o: pltpu.roll(x, (-o) % n, 0)
    core = lambda x: x[POOL_HALO:POOL_HALO + tm, :]
    pos = t0 + lax.broadcasted_iota(jnp.int32, z.shape, 0)
    lane = lax.broadcasted_iota(jnp.int32, z.shape, 1)

    def count(wd):
        lo = jnp.maximum(pos - wd // 2, 0)
        hi = jnp.minimum(pos + wd - wd // 2, seq_len)
        return (hi - lo).astype(F32)

    s2 = shifted(e, -1) + e
    pooled = core(s2) / count(2)
    s4 = shifted(s2, -1) + shifted(s2, 1)
    pooled = jnp.where(lane >= 64, core(s4) / count(4), pooled)
    s8 = shifted(s4, -2) + shifted(s4, 2)
    pooled = jnp.where(lane >= 128, core(s8) / count(8), pooled)
    s16 = shifted(s8, -4) + shifted(s8, 4)
    pooled = jnp.where(lane >= 192, core(s16) / count(16), pooled)
    d = pooled - z
    return jnp.dot(d.astype(BF16), w_bd, preferred_element_type=F32) * scale


def _gelu_tanh(x):
    return 0.5 * x * (1.0 + jnp.tanh(math.sqrt(2.0 / math.pi) * (x + 0.044715 * (x * x * x))))


def _gmlp(z, ln_g, ln_b, ws_ref, bias, ones_bd):
    inv = 1.0 / HEAD
    tm = z.shape[0]
    u = _gelu_tanh(z[:, 0:GROUP_W])
    v = _gelu_tanh(z[:, GROUP_W:2 * GROUP_W])
    mu = _seg_sum(v, ones_bd) * inv
    vc = v - mu
    var = _seg_sum(vc * vc, ones_bd) * inv
    vn = (vc * lax.rsqrt(var + LN_EPS) * ln_g + ln_b).astype(BF16)
    lane = lax.broadcasted_iota(jnp.int32, (CHUNK, GROUP_W), 1)
    out = []
    for c in range(tm // CHUNK):
        vchunk = vn[c * CHUNK:(c + 1) * CHUNK, :]
        sv = bias
        for g in range(4):
            part = jnp.dot(ws_ref[g], vchunk, preferred_element_type=F32)
            sv = sv + jnp.where((lane >= g * HEAD) & (lane < (g + 1) * HEAD), part, 0.0)
        out.append(u[c * CHUNK:(c + 1) * CHUNK, :] * sv)
    return jnp.concatenate(out, axis=0)


def _fnet_kernel(z_ref, cs_ref, cc_ref, sc_ref, w_ref, b_ref, o_ref, zcs_ref, *, l, scale, nbg):
    c = z_ref.shape[2]

    @pl.when(pl.program_id(1) == 0)
    def _():
        for j in range(nbg):
            zb = z_ref[j].astype(BF16)
            zcs_ref[0:l, j * c:(j + 1) * c] = jnp.dot(zb, cc_ref[...], preferred_element_type=F32).astype(BF16)
            zcs_ref[l:2 * l, j * c:(j + 1) * c] = jnp.dot(zb, sc_ref[...], preferred_element_type=F32).astype(BF16)

    f = jnp.dot(cs_ref[...], zcs_ref[...], preferred_element_type=F32) * scale
    for j in range(nbg):
        o_ref[j] = jnp.dot(f[:, j * c:(j + 1) * c].astype(BF16), w_ref[...], preferred_element_type=F32) + b_ref[...]


def _fourier_mixer(z, cs, cc_bd, sc_bd, w_bd, bias, tm, row0, l, nbg=4):
    b, _, c = z.shape
    assert b % nbg == 0 and row0 % l == 0
    full = lambda shape: pl.BlockSpec(shape, lambda bi, i: (0,) * len(shape))
    scale = 1.0 / math.sqrt(l * HEAD)
    vmem = 2 * (nbg * l * c * 4 + tm * 2 * l * 2 + nbg * tm * c * 4) + 2 * l * nbg * c * 2 + 4 * l * c * 4
    return pl.pallas_call(
        functools.partial(_fnet_kernel, l=l, scale=scale, nbg=nbg),
        grid=(b // nbg, l // tm),
        in_specs=[pl.BlockSpec((nbg, l, c), lambda bi, i: (bi, row0 // l, 0)),
                  pl.BlockSpec((tm, 2 * l), lambda bi, i: (i, 0)),
                  full(cc_bd.shape), full(sc_bd.shape), full(w_bd.shape), full(bias.shape)],
        out_specs=pl.BlockSpec((nbg, tm, c), lambda bi, i: (bi, i, 0)),
        out_shape=jax.ShapeDtypeStruct((b, l, c), F32),
        scratch_shapes=[pltpu.VMEM((2 * l, nbg * c), BF16)],
        compiler_params=_cparams(("parallel", "arbitrary"), vmem),
        name="fourier_mixer",
    )(z, cs, cc_bd, sc_bd, w_bd, bias)


def _outproj_kernel(*refs, alpha, add_pos, tm, seq_len):
    (y_ref, r_ref, v_ref, kd_ref, gate_ref, zg_ref, zp_ref, zpp_ref, zpn_ref, f_ref, x_ref), refs = refs[:11], refs[11:]
    if add_pos:
        pos_ref, refs = refs[0], refs[1:]
        x = x_ref[0] + pos_ref[...]
    else:
        x = x_ref[0]
    (mod_ref, rk_ref, gg_ref, gb_ref, ones_ref, lng_ref, lnb_ref, ws_ref, bias_ref, pw_ref, ps_ref, w_ref, lg_ref,
     lb_ref, o_ref) = refs
    i = pl.program_id(0)
    ones_bd = ones_ref[...]
    a = _readout(y_ref[0], r_ref[0], v_ref[0], kd_ref[0], gate_ref[0], rk_ref[...], gg_ref[...], gb_ref[...], ones_bd)
    prev_rows = zpp_ref[0] * (i > 0).astype(F32)
    next_rows = zpn_ref[0] * (i < pl.num_programs(0) - 1).astype(F32)
    p = _pool(zp_ref[0], prev_rows, next_rows, i * tm, seq_len, pw_ref[...], ps_ref[...])
    g = _gmlp(zg_ref[0], lng_ref[...], lnb_ref[...], ws_ref, bias_ref[...], ones_bd)
    mix = None
    for j, part in enumerate((a, p, g, f_ref[0])):
        prod = jnp.dot(part.astype(BF16), w_ref[j * GROUP_W:(j + 1) * GROUP_W, :], preferred_element_type=F32)
        mix = prod if mix is None else mix + prod
    gate1 = mod_ref[0, 2:3, :]
    o_ref[0] = _layer_norm(alpha * x + gate1 * mix, lg_ref[...], lb_ref[...])


def _mix_out_proj(y, shared, pair, z_gmlp, z_pool, row0, fourier, x, pos, mod, params, alpha, tm):
    b, l, d = x.shape
    add_pos = pos is not None
    off = row0 // tm
    comb = lambda width, col: pl.BlockSpec((1, tm, width), lambda i, bi: (bi, i + off, col))
    s1 = pl.BlockSpec((1, tm, GROUP_W), lambda i, bi: (bi, i, 0))
    xspec = pl.BlockSpec((1, tm, d), lambda i, bi: (bi, i, 0))
    full = lambda shape: pl.BlockSpec(shape, lambda i, bi: (0,) * len(shape))
    h0, h1, hr = row0 // POOL_HALO, (row0 + l) // POOL_HALO - 1, tm // POOL_HALO
    halo = lambda f: pl.BlockSpec((1, POOL_HALO, GROUP_W), lambda i, bi: (bi, jnp.clip(f(i), h0, h1), 0))
    in_specs = [comb(GROUP_W, 0), comb(GROUP_W, SH_R), comb(GROUP_W, SH_V), comb(2 * GROUP_W, PR_KD // 2),
                comb(GROUP_W, SH_GATE), comb(2 * GROUP_W, 0), comb(GROUP_W, 0),
                halo(lambda i: h0 + i * hr - 1), halo(lambda i: h0 + (i + 1) * hr), s1, xspec]
    args = [y, shared, shared, pair, shared, z_gmlp, z_pool, z_pool, z_pool, fourier, x]
    if add_pos:
        in_specs.append(pl.BlockSpec((tm, d), lambda i, bi: (i, 0)))
        args.append(pos)
    in_specs += [pl.BlockSpec((1, 6, d), lambda i, bi: (bi, 0, 0))] + [full(p.shape) for p in params]
    args += [mod, *params]
    vmem = 2 * (10 * tm * GROUP_W * 4 + 3 * tm * d * 4 + d * d * 2) + 6 * tm * d * 4 + 24 * tm * GROUP_W * 4
    return pl.pallas_call(
        functools.partial(_outproj_kernel, alpha=alpha, add_pos=add_pos, tm=tm, seq_len=l),
        grid=(l // tm, b),
        in_specs=in_specs,
        out_specs=xspec,
        out_shape=jax.ShapeDtypeStruct((b, l, d), F32),
        compiler_params=_cparams(("parallel", "parallel"), vmem),
        name="mix_out_proj_ln",
    )(*args)


def _ffn_kernel(x_ref, mod_ref, wg_ref, wu_ref, w2_ref, lg_ref, lb_ref, o_ref, *, alpha, f_chunks):
    x = x_ref[0]
    sh = mod_ref[0, 3:4, :]
    sc = mod_ref[0, 4:5, :]
    h = (x * (1.0 + sc) + sh).astype(BF16)
    acc = None
    for lo, hi in f_chunks:
        gate = jnp.dot(h, wg_ref[:, lo:hi], preferred_element_type=F32)
        up = jnp.dot(h, wu_ref[:, lo:hi], preferred_element_type=F32)
        act = (_silu(gate) * up).astype(BF16)
        part = jnp.dot(act, w2_ref[lo:hi, :], preferred_element_type=F32)
        acc = part if acc is None else acc + part
    gate2 = mod_ref[0, 5:6, :]
    o_ref[0] = _layer_norm(alpha * x + gate2 * acc, lg_ref[...], lb_ref[...])


def _ffn(x, mod, w1, w2, ln_g, ln_b, alpha, tm):
    b, l, d = x.shape
    f = w2.shape[0]
    mxu_n = 256
    half = (f // mxu_n + 1) // 2 * mxu_n
    f_chunks = ((0, half), (half, f))
    xspec = pl.BlockSpec((1, tm, d), lambda bi, i: (bi, i, 0))
    full = lambda shape: pl.BlockSpec(shape, lambda bi, i: (0,) * len(shape))
    once = pl.Buffered(1)
    vmem = 4 * tm * d * 4 + 3 * d * f * 2 + 3 * tm * d * 4 + 4 * tm * half * 4
    return pl.pallas_call(
        functools.partial(_ffn_kernel, alpha=alpha, f_chunks=f_chunks),
        grid=(b, l // tm),
        in_specs=[xspec,
                  pl.BlockSpec((1, 6, d), lambda bi, i: (bi, 0, 0)),
                  pl.BlockSpec((d, f), lambda bi, i: (0, 0), pipeline_mode=once),
                  pl.BlockSpec((d, f), lambda bi, i: (0, 1), pipeline_mode=once),
                  pl.BlockSpec((f, d), lambda bi, i: (0, 0), pipeline_mode=once),
                  full(ln_g.shape), full(ln_b.shape)],
        out_specs=xspec,
        out_shape=jax.ShapeDtypeStruct((b, l, d), F32),
        compiler_params=_cparams(("parallel", "parallel"), vmem),
        name="ffn_ln",
    )(x, mod, w1, w1, w2, ln_g, ln_b)


def _pos_embed(n_tok, dim):
    rows = n_tok // GRID_W
    row, col = jnp.meshgrid(jnp.arange(rows, dtype=F32), jnp.arange(GRID_W, dtype=F32), indexing='ij')
    quarter = dim // 4
    freqs = jnp.exp(-math.log(10000.0) * jnp.arange(quarter, dtype=F32) / quarter)

    def enc(p):
        ang = p.reshape(-1, 1) * freqs[None, :]
        return jnp.concatenate([jnp.sin(ang), jnp.cos(ang)], -1)

    return jnp.concatenate([enc(row), enc(col)], -1)


def _block_diag(blocks):
    g, n, m = blocks.shape
    eye = jnp.eye(g, dtype=blocks.dtype)
    return (eye[:, None, :, None] * blocks[:, :, None, :]).reshape(g * n, g * m)


def _dft_tables(n):
    def direct(rows, cols, period):
        ang = ((rows[:, None] * cols[None, :]) % period).astype(F32) * (2.0 * math.pi / period)
        return jnp.cos(ang), jnp.sin(ang)

    idx = jnp.arange(n, dtype=jnp.int32)
    if n <= 1024:
        return direct(idx, idx, n)
    lo = 64
    ca, sa = direct(jnp.arange(n // lo, dtype=jnp.int32), idx, n // lo)
    cb, sb = direct(jnp.arange(lo, dtype=jnp.int32), idx, n)
    ca, sa, cb, sb = ca[:, None, :], sa[:, None, :], cb[None], sb[None]
    return (ca * cb - sa * sb).reshape(n, n), (sa * cb + ca * sb).reshape(n, n)


def kernel(x, c, ctx, c_ctx, w_mod, b_mod, w_in, rkv_conv, decay_w0, decay_w2, iclr_a0, iclr_a2,
           gate_g2, k_k, k_a, r_k, gn_g, gn_b, pool_w, pool_scale, gmlp_ln_g, gmlp_ln_b, gmlp_ws,
           gmlp_bs, fnet_w, fnet_b, w_out, ln1_g, ln1_b, ln2_g, ln2_b, ffn_w1, ffn_w2):
    bsz, seq, d = x.shape
    ctx_len = ctx.shape[1]
    depth = w_in.shape[0]
    alpha = (2 * depth) ** 0.25
    assert bsz * N_HEADS * 2 == V7X_LANES, "scan layout packs (direction, batch, head) on the 128 lanes"

    pos = _pos_embed(seq, d)
    ones_bd = _block_diag(jnp.ones((N_HEADS, HEAD, HEAD), F32)).astype(BF16)
    cc64, ss64 = _dft_tables(HEAD)
    cc_bd = _block_diag(jnp.tile(cc64[None], (4, 1, 1))).astype(BF16)
    sc_bd = _block_diag(jnp.tile(ss64[None], (4, 1, 1))).astype(BF16)

    def dft_rows(n):
        cn, sn = _dft_tables(n)
        return jnp.concatenate([cn, -sn], axis=1).astype(BF16)

    cs_x = dft_rows(seq)
    cs_c = dft_rows(ctx_len)

    pad_rows = (-(bsz + 1)) % V7X_SUBLANES
    c_all = jnp.concatenate([c, c_ctx[None], jnp.zeros((pad_rows, d), F32)], axis=0)

    tm_x = 512
    tm_c = ctx_len
    tb = 64

    xs, cs = x, ctx
    for l in range(depth):
        last = l == depth - 1
        first = l == 0
        m = _modulation(c_all, w_mod[l].astype(BF16), b_mod[l])
        mod_x = m[:bsz].reshape(bsz, 6, d)
        mod_c = jnp.broadcast_to(m[bsz].reshape(1, 6, d), (bsz, 6, d))

        wl = jnp.zeros((LORA_W, 5 * GROUP_W), F32)
        wl = wl.at[0:64, 0:256].set(decay_w2[l, 0]).at[64:128, 256:512].set(decay_w2[l, 1])
        wl = wl.at[128:192, 512:768].set(iclr_a2[l, 0]).at[192:256, 768:1024].set(iclr_a2[l, 1])
        wl = wl.at[256:384, 1024:1280].set(gate_g2[l]).astype(BF16)
        prep_args = (rkv_conv[l], wl, decay_w0[l], iclr_a0[l], k_k[l].reshape(1, -1), k_a[l].reshape(1, -1),
                     ones_bd)
        shared, pair, z_pool, z_gmlp, z_fnet = _proj_prep(xs, cs, pos if first else None, mod_x, mod_c,
                                                          w_in[l].astype(BF16), *prep_args)
        to_scan = lambda a, col_f, col_b, **kw: _to_scan_layout(a, seq, ctx_len, col_f, col_b, **kw)
        y_s = _rwkv_scan(to_scan(shared, SH_R, SH_R), to_scan(shared, SH_V, SH_V, step_major=True),
                         to_scan(shared, SH_KK, SH_KK), to_scan(pair, PR_LW, PR_LW + 1),
                         to_scan(pair, PR_B, PR_B + 1), to_scan(pair, PR_KD, PR_KD + 1), tb)
        y_nat = _from_scan_layout(y_s, bsz, seq, ctx_len)

        pool_bd = _block_diag(pool_w[l]).astype(BF16)
        fnet_bd = _block_diag(fnet_w[l]).astype(BF16)
        ws_b = gmlp_ws[l].astype(BF16)
        bias_tile = jnp.repeat(gmlp_bs[l].T, HEAD, axis=1)
        row = lambda a: a.reshape(1, -1)

        mix_params = (row(r_k[l]), row(gn_g[l]), row(gn_b[l]), ones_bd, row(gmlp_ln_g[l]), row(gmlp_ln_b[l]), ws_b,
                      bias_tile, pool_bd, row(pool_scale[l]), w_out[l].astype(BF16), row(ln1_g[l]), row(ln1_b[l]))

        def mix_sublayer(row0, n_rows, stream, pos_tab, mod, cs_tab, tm, tmf):
            fourier = _fourier_mixer(z_fnet, cs_tab, cc_bd, sc_bd, fnet_bd, row(fnet_b[l]), tmf, row0, n_rows)
            return _mix_out_proj(y_nat, shared, pair, z_gmlp, z_pool, row0, fourier, stream, pos_tab, mod,
                                 mix_params, alpha, tm)

        w1_l = ffn_w1[l].astype(BF16)
        w2_l = ffn_w2[l].astype(BF16)
        xs = mix_sublayer(0, seq, xs, pos if first else None, mod_x, cs_x, tm_x, 256)
        xs = _ffn(xs, mod_x, w1_l, w2_l, row(ln2_g[l]), row(ln2_b[l]), alpha, tm_x)
        if not last:
            cs = mix_sublayer(seq, ctx_len, cs, None, mod_c, cs_c, tm_c, tm_c)
            cs = _ffn(cs, mod_c, w1_l, w2_l, row(ln2_g[l]), row(ln2_b[l]), alpha, tm_c)
    return xs
```

```python
import functools
import math

import jax
import jax.numpy as jnp
from jax import lax
from jax.experimental import pallas as pl
from jax.experimental.pallas import tpu as pltpu

F32 = jnp.float32
BF16 = jnp.bfloat16

V7X_LANES = 128
V7X_SUBLANES = 8
V7X_VMEM_BYTES = 64 * 1024 * 1024
V7X_VMEM_LIMIT_CAP = V7X_VMEM_BYTES - 8 * 1024 * 1024

GRID_W = 64
HEAD = 64
GROUP_W = 256
N_HEADS = GROUP_W // HEAD
LORA_W = 384
POOL_WINDOWS = (2, 4, 8, 16)
CHUNK = 128
LN_EPS = 1e-5
GN_EPS = 64e-5


def _cparams(sem, vmem_bytes):
    limit = int(min(max(vmem_bytes, 16 * 1024 * 1024), V7X_VMEM_LIMIT_CAP))
    return pltpu.CompilerParams(dimension_semantics=sem, vmem_limit_bytes=limit)


def _silu(x):
    return x * jax.nn.sigmoid(x)


def _seg_sum(x, ones_bd):
    hi = x.astype(BF16)
    lo = (x - hi.astype(F32)).astype(BF16)
    return (jnp.dot(hi, ones_bd, preferred_element_type=F32)
            + jnp.dot(lo, ones_bd, preferred_element_type=F32))


def _layer_norm(z, g, b):
    mu = jnp.mean(z, axis=-1, keepdims=True)
    zc = z - mu
    var = jnp.mean(zc * zc, axis=-1, keepdims=True)
    return zc * lax.rsqrt(var + LN_EPS) * g + b


def _mod_kernel(c_ref, w_ref, b_ref, o_ref):
    s = _silu(c_ref[...])
    o_ref[...] = jnp.dot(s.astype(BF16), w_ref[...], preferred_element_type=F32) + b_ref[...]


def _modulation(c_all, w_mod, b_mod):
    rows, d = c_all.shape
    n = w_mod.shape[1]
    tn = 1024
    return pl.pallas_call(
        _mod_kernel,
        grid=(n // tn,),
        in_specs=[pl.BlockSpec((rows, d), lambda j: (0, 0)),
                  pl.BlockSpec((d, tn), lambda j: (0, j)),
                  pl.BlockSpec((1, tn), lambda j: (0, j))],
        out_specs=pl.BlockSpec((rows, tn), lambda j: (0, j)),
        out_shape=jax.ShapeDtypeStruct((rows, n), F32),
        compiler_params=_cparams(("parallel",), 4 * d * tn * 2),
        name="modulation",
    )(c_all, w_mod, b_mod.reshape(1, n))


SH_R, SH_V, SH_KK, SH_GATE = range(4)
PR_LW, PR_B, PR_KD = 0, 2, 4
RKV_W = 3 * GROUP_W


def _proj_prep_kernel(*refs, tm, nx, add_pos):
    (xx_ref, xp_ref, xn_ref, xc_ref), refs = refs[:4], refs[4:]
    xx, xp, xn = xx_ref[0], xp_ref[0], xn_ref[0]
    if add_pos:
        (pos_ref, pp_ref, pn_ref), refs = refs[:3], refs[3:]
        xx, xp, xn = xx + pos_ref[...], xp + pp_ref[...], xn + pn_ref[...]
    (modx_ref, modc_ref, w_ref, conv_ref, wl_ref, w0_ref, a0_ref, kk_ref, ka_ref, ones_ref,
     shared_o, pair_o, pool_o, gmlp_o, fnet_o) = refs
    i = pl.program_id(1)
    is_ctx = i >= nx
    x = jnp.where(is_ctx, xc_ref[0], xx)
    mod = jnp.where(is_ctx, modc_ref[0, 0:2, :], modx_ref[0, 0:2, :])
    adaln = lambda t: (t * (1.0 + mod[1:2]) + mod[0:1]).astype(BF16)
    proj = lambda t, lo, hi: jnp.dot(t, w_ref[:, lo:hi], preferred_element_type=F32)
    h = adaln(x)
    z = proj(h, 0, RKV_W)
    lo = proj(h, RKV_W, RKV_W + LORA_W)
    off = RKV_W + LORA_W
    pool_o[0] = proj(h, off, off + GROUP_W)
    gmlp_o[0] = proj(h, off + GROUP_W, off + 3 * GROUP_W)
    fnet_o[0] = proj(h, off + 3 * GROUP_W, off + 4 * GROUP_W)

    has_prev = jnp.logical_and(i > 0, i < nx).astype(F32)
    has_next = (i < nx - 1).astype(F32)
    prev_row = proj(adaln(xp), 0, RKV_W)[V7X_SUBLANES - 1:V7X_SUBLANES, :] * has_prev
    next_row = proj(adaln(xn), 0, RKV_W)[0:1, :] * has_next
    row = lax.broadcasted_iota(jnp.int32, z.shape, 0)
    zm1 = jnp.where(row == 0, prev_row, pltpu.roll(z, 1, 0))
    zp1 = jnp.where(row == tm - 1, next_row, pltpu.roll(z, tm - 1, 0))
    c = zm1 * conv_ref[0:1, :] + z * conv_ref[1:2, :] + zp1 * conv_ref[2:3, :]
    r = c[:, 0:GROUP_W]
    k = c[:, GROUP_W:2 * GROUP_W]
    v = c[:, 2 * GROUP_W:3 * GROUP_W]

    col = lax.broadcasted_iota(jnp.int32, lo.shape, 1)
    act = jnp.where(col < 128, jnp.tanh(lo), jnp.where(col < 256, lo, jax.nn.sigmoid(lo)))
    pre = jnp.dot(act.astype(BF16), wl_ref[...], preferred_element_type=F32)

    ones_bd = ones_ref[...]
    kk = k * kk_ref[...]
    nrm = jnp.sqrt(_seg_sum(kk * kk, ones_bd))
    kk = kk / jnp.maximum(nrm, 1e-12)
    blk = lambda j: slice(j * GROUP_W, (j + 1) * GROUP_W)
    for j, val in ((SH_R, r), (SH_V, v), (SH_KK, kk), (SH_GATE, pre[:, 4 * GROUP_W:5 * GROUP_W])):
        shared_o[0, :, blk(j)] = val
    ka = ka_ref[...]
    for d in range(2):
        xw = w0_ref[d:d + 1, :] + pre[:, blk(d)]
        pair_o[0, :, blk(PR_LW + d)] = -math.exp(-0.5) * jax.nn.sigmoid(xw)
        a = jax.nn.sigmoid(a0_ref[d:d + 1, :] + pre[:, blk(2 + d)])
        pair_o[0, :, blk(PR_KD + d)] = k * (1.0 + (a - 1.0) * ka)
        pair_o[0, :, blk(PR_B + d)] = kk * a


def _proj_prep(x, ctx, pos, mod_x, mod_c, w_in, conv, wl, w0, a0, k_k, k_a, ones_bd):
    b, l, d = x.shape
    tm = ctx.shape[1]
    nx = l // tm
    nblk8 = l // V7X_SUBLANES
    r8 = tm // V7X_SUBLANES
    add_pos = pos is not None
    full = lambda shape: pl.BlockSpec(shape, lambda bi, i: (0,) * len(shape))
    xi = lambda i: jnp.minimum(i, nx - 1)
    prev8 = lambda i: jnp.clip(i * r8 - 1, 0, nblk8 - 1)
    next8 = lambda i: jnp.minimum((i + 1) * r8, nblk8 - 1)
    in_specs = [pl.BlockSpec((1, tm, d), lambda bi, i: (bi, xi(i), 0)),
                pl.BlockSpec((1, V7X_SUBLANES, d), lambda bi, i: (bi, prev8(i), 0)),
                pl.BlockSpec((1, V7X_SUBLANES, d), lambda bi, i: (bi, next8(i), 0)),
                pl.BlockSpec((1, tm, d), lambda bi, i: (bi, 0, 0))]
    args = [x, x, x, ctx]
    if add_pos:
        in_specs += [pl.BlockSpec((tm, d), lambda bi, i: (xi(i), 0)),
                     pl.BlockSpec((V7X_SUBLANES, d), lambda bi, i: (prev8(i), 0)),
                     pl.BlockSpec((V7X_SUBLANES, d), lambda bi, i: (next8(i), 0))]
        args += [pos, pos, pos]
    mod_spec = pl.BlockSpec((1, 6, d), lambda bi, i: (bi, 0, 0))
    params = (w_in, conv, wl, w0, a0, k_k, k_a, ones_bd)
    in_specs += [mod_spec, mod_spec] + [full(p.shape) for p in params]
    args += [mod_x, mod_c, *params]
    widths = (4 * GROUP_W, 6 * GROUP_W, GROUP_W, 2 * GROUP_W, GROUP_W)
    n = w_in.shape[1]
    vmem = 2 * (4 * tm * d * 4 + d * n * 2 + tm * sum(widths) * 4) + 3 * tm * d * 4 + 16 * tm * RKV_W * 4
    return pl.pallas_call(
        functools.partial(_proj_prep_kernel, tm=tm, nx=nx, add_pos=add_pos),
        grid=(b, nx + 1),
        in_specs=in_specs,
        out_specs=[pl.BlockSpec((1, tm, wd), lambda bi, i: (bi, i, 0)) for wd in widths],
        out_shape=[jax.ShapeDtypeStruct((b, l + tm, wd), F32) for wd in widths],
        compiler_params=_cparams(("parallel", "arbitrary"), vmem),
        name="proj_prep",
    )(*args)


SCAN_ROW_PITCH = 72


def _to_scan_kernel(xf_ref, xb_ref, o_ref, y_ref, *, nb, tt, step_major):
    rev = (tt - 1) - lax.broadcasted_iota(jnp.int32, (GROUP_W, tt), 1)
    for b in range(nb):
        xf = xf_ref[b].T
        xb = jnp.take_along_axis(xb_ref[b].T, rev, axis=1)
        for h in range(N_HEADS):
            y_ref[pl.ds((b * 2 * N_HEADS + h) * SCAN_ROW_PITCH, HEAD), :] = xf[h * HEAD:(h + 1) * HEAD, :]
            y_ref[pl.ds((b * 2 * N_HEADS + N_HEADS + h) * SCAN_ROW_PITCH, HEAD), :] = xb[h * HEAD:(h + 1) * HEAD, :]
    for k in range(HEAD):
        z = y_ref[pl.ds(k, nb * 2 * N_HEADS, stride=SCAN_ROW_PITCH), :]
        if step_major:
            o_ref[pl.ds(k, tt, stride=HEAD), :] = z.T
        else:
            o_ref[k] = z.T


def _to_scan_layout(x, n_lat, n_ctx, fwd_cols, bwd_cols, step_major=False):
    nb, t, c = x.shape
    tt = V7X_LANES
    assert nb * 2 * N_HEADS == V7X_LANES and c % GROUP_W == 0 and n_lat % tt == 0 and n_ctx % tt == 0
    nl, nc = n_lat // tt, n_ctx // tt
    n = nl + nc
    if step_major:
        out_spec = pl.BlockSpec((tt * HEAD, V7X_LANES), lambda j: (j, 0))
        out_shape = jax.ShapeDtypeStruct((t * HEAD, V7X_LANES), F32)
    else:
        out_spec = pl.BlockSpec((HEAD, tt, V7X_LANES), lambda j: (0, j, 0))
        out_shape = jax.ShapeDtypeStruct((HEAD, t, V7X_LANES), F32)
    out = pl.pallas_call(
        functools.partial(_to_scan_kernel, nb=nb, tt=tt, step_major=step_major),
        grid=(n,),
        in_specs=[pl.BlockSpec((nb, tt, GROUP_W), lambda j: (0, jnp.where(j < nc, nl + j, j - nc), fwd_cols)),
                  pl.BlockSpec((nb, tt, GROUP_W), lambda j: (0, n - 1 - j, bwd_cols))],
        out_specs=out_spec,
        out_shape=out_shape,
        scratch_shapes=[pltpu.VMEM((V7X_LANES * SCAN_ROW_PITCH, tt), F32)],
        compiler_params=_cparams(("parallel",), 2 * (nb * tt * 2 * GROUP_W * 4 + HEAD * tt * V7X_LANES * 4)
                                 + 3 * V7X_LANES * SCAN_ROW_PITCH * tt * 4),
        name="to_scan_layout",
    )(x, x)
    return out.reshape(t, HEAD, V7X_LANES) if step_major else out


def _from_scan_kernel(yf_ref, yb_ref, o_ref, s_ref, *, nb, tt):
    rev = (tt - 1) - lax.broadcasted_iota(jnp.int32, (V7X_LANES, tt), 1)
    fwd_row = (lax.broadcasted_iota(jnp.int32, (V7X_LANES, tt), 0) & N_HEADS) == 0
    for v in range(HEAD):
        zf = yf_ref[pl.ds(v, tt, stride=SCAN_ROW_PITCH), :].T
        zb = jnp.take_along_axis(yb_ref[pl.ds(v, tt, stride=SCAN_ROW_PITCH), :].T, rev, axis=1)
        s_ref[pl.ds(v, V7X_LANES, stride=SCAN_ROW_PITCH), :] = jnp.where(fwd_row, zf, zb)
    groups = 2 * N_HEADS
    for b in range(nb):
        grp = lambda g: s_ref[pl.ds((b * groups + g) * SCAN_ROW_PITCH, HEAD), :]
        rows = [grp(h) + grp(N_HEADS + h) for h in range(N_HEADS)]
        o_ref[b] = jnp.concatenate(rows, axis=0).T


def _from_scan_layout(y, nb, n_lat, n_ctx):
    t, pitch, _ = y.shape
    assert pitch == SCAN_ROW_PITCH
    tt = V7X_LANES
    nl, nc = n_lat // tt, n_ctx // tt
    n = nl + nc
    y2d = y.reshape(t * pitch, V7X_LANES)
    blk = pl.BlockSpec((tt * pitch, V7X_LANES), lambda j: (jnp.where(j < nl, j + nc, j - nl), 0))
    rblk = pl.BlockSpec((tt * pitch, V7X_LANES), lambda j: (n - 1 - j, 0))
    return pl.pallas_call(
        functools.partial(_from_scan_kernel, nb=nb, tt=tt),
        grid=(n,),
        in_specs=[blk, rblk],
        out_specs=pl.BlockSpec((nb, tt, GROUP_W), lambda j: (0, j, 0)),
        out_shape=jax.ShapeDtypeStruct((nb, t, GROUP_W), F32),
        scratch_shapes=[pltpu.VMEM((V7X_LANES * SCAN_ROW_PITCH, tt), F32)],
        compiler_params=_cparams(("parallel",), 2 * (2 * tt * pitch * V7X_LANES * 4 + nb * tt * 2 * GROUP_W * 4)
                                 + 3 * V7X_LANES * SCAN_ROW_PITCH * tt * 4),
        name="from_scan_layout",
    )(y2d, y2d)


def _scan_kernel(r_ref, v_ref, kk_ref, lw_ref, b_ref, kd_ref, y_ref, h_ref, s_ref, g_ref, *, tb):
    @pl.when(pl.program_id(0) == 0)
    def _():
        h_ref[...] = jnp.zeros_like(h_ref)

    tile = (V7X_SUBLANES, V7X_LANES)
    R_G, KK_G, B_G, KD_G = range(4)

    sub = lax.broadcasted_iota(jnp.int32, tile, 0)

    def scale(k, carry):
        lw = lw_ref[k]
        parts = []
        total = jnp.zeros(tile, F32)
        for j in range(tb // V7X_SUBLANES):
            x = lw[j * V7X_SUBLANES:(j + 1) * V7X_SUBLANES, :]
            for s in (1, 2, 4):
                x = x + jnp.where(sub >= s, pltpu.roll(x, s, 0), 0.0)
            x = x + total
            total = jnp.broadcast_to(x[V7X_SUBLANES - 1:V7X_SUBLANES, :], tile)
            parts.append(x)
        cum = jnp.concatenate(parts, axis=0)
        g = jnp.exp(cum)
        g_inv = jnp.exp(-cum)
        s_ref[R_G, k, 0:tb] = r_ref[k] * g
        s_ref[KK_G, k, 0:tb] = kk_ref[k] * jnp.exp(cum - lw)
        s_ref[B_G, k, 0:tb] = b_ref[k] * g_inv
        s_ref[KD_G, k, 0:tb] = kd_ref[k] * g_inv
        g_ref[pl.ds(k, 1), :] = g[tb - 1:tb, :]
        return carry

    lax.fori_loop(0, HEAD, scale, 0)

    u0 = jnp.zeros((HEAD, V7X_LANES), F32)
    for k in range(HEAD):
        u0 = u0 + s_ref[KK_G, k, 0:1, :] * h_ref[k]

    def step(t, u):
        tn = jnp.minimum(t + 1, tb - 1)
        vv = v_ref[t]
        y = jnp.zeros((HEAD, V7X_LANES), F32)
        un = jnp.zeros((HEAD, V7X_LANES), F32)
        for k in range(HEAD):
            hn = h_ref[k] - s_ref[B_G, k, pl.ds(t, 1), :] * u + s_ref[KD_G, k, pl.ds(t, 1), :] * vv
            h_ref[k] = hn
            y = y + s_ref[R_G, k, pl.ds(t, 1), :] * hn
            un = un + s_ref[KK_G, k, pl.ds(tn, 1), :] * hn
        y_ref[t, 0:HEAD] = y
        return un

    y_ref[:, HEAD:, :] = jnp.zeros((tb, SCAN_ROW_PITCH - HEAD, V7X_LANES), F32)
    lax.fori_loop(0, tb, step, u0)
    for k in range(HEAD):
        h_ref[k] = h_ref[k] * g_ref[k:k + 1, :]


def _rwkv_scan(r, v, kk, w, b, kd, tb):
    t_tot = r.shape[1]
    kspec = pl.BlockSpec((HEAD, tb, V7X_LANES), lambda i: (0, i, 0))
    sspec = pl.BlockSpec((tb, HEAD, V7X_LANES), lambda i: (i, 0, 0))
    blk = tb * HEAD * V7X_LANES * 4
    return pl.pallas_call(
        functools.partial(_scan_kernel, tb=tb),
        grid=(t_tot // tb,),
        in_specs=[kspec, sspec, kspec, kspec, kspec, kspec],
        out_specs=pl.BlockSpec((tb, SCAN_ROW_PITCH, V7X_LANES), lambda i: (i, 0, 0)),
        out_shape=jax.ShapeDtypeStruct((t_tot, SCAN_ROW_PITCH, V7X_LANES), F32),
        scratch_shapes=[pltpu.VMEM((HEAD, HEAD, V7X_LANES), F32),
                        pltpu.VMEM((4, HEAD, tb, V7X_LANES), F32),
                        pltpu.VMEM((HEAD, V7X_LANES), F32)],
        compiler_params=_cparams(("arbitrary",), 14 * blk + 5 * blk + 3 * HEAD * HEAD * V7X_LANES * 4),
        name="rwkv_scan",
    )(r, v, kk, w, b, kd)


def _readout(y, r, v, kd2, gate, r_k, gn_g, gn_b, ones_bd):
    inv = 1.0 / HEAD
    kd = kd2[:, 0:GROUP_W] + kd2[:, GROUP_W:2 * GROUP_W]
    mu = _seg_sum(y, ones_bd) * inv
    yc = y - mu
    var = _seg_sum(yc * yc, ones_bd) * inv
    yn = yc * lax.rsqrt(var + GN_EPS)
    bonus = _seg_sum(r * kd * r_k, ones_bd) * v
    return (yn * gn_g + gn_b + bonus) * gate


POOL_HALO = 8


def _pool(z, prev_rows, next_rows, t0, seq_len, w_bd, scale):
    assert POOL_WINDOWS == (2, 4, 8, 16) and POOL_HALO == max(POOL_WINDOWS) // 2
    tm = z.shape[0]
    n = tm + 2 * POOL_HALO
    e = jnp.concatenate([prev_rows, z, next_rows], axis=0)
    shifted = lambda x, o: pltpu.roll(x, (-o) % n, 0)
    core = lambda x: x[POOL_HALO:POOL_HALO + tm, :]
    pos = t0 + lax.broadcasted_iota(jnp.int32, z.shape, 0)
    lane = lax.broadcasted_iota(jnp.int32, z.shape, 1)

    def count(wd):
        lo = jnp.maximum(pos - wd // 2, 0)
        hi = jnp.minimum(pos + wd - wd // 2, seq_len)
        return (hi - lo).astype(F32)

    s2 = shifted(e, -1) + e
    pooled = core(s2) / count(2)
    s4 = shifted(s2, -1) + shifted(s2, 1)
    pooled = jnp.where(lane >= 64, core(s4) / count(4), pooled)
    s8 = shifted(s4, -2) + shifted(s4, 2)
    pooled = jnp.where(lane >= 128, core(s8) / count(8), pooled)
    s16 = shifted(s8, -4) + shifted(s8, 4)
    pooled = jnp.where(lane >= 192, core(s16) / count(16), pooled)
    d = pooled - z
    return jnp.dot(d.astype(BF16), w_bd, preferred_element_type=F32) * scale


def _gelu_tanh(x):
    return 0.5 * x * (1.0 + jnp.tanh(math.sqrt(2.0 / math.pi) * (x + 0.044715 * (x * x * x))))


def _gmlp(z, ln_g, ln_b, ws_ref, bias, ones_bd):
    inv = 1.0 / HEAD
    tm = z.shape[0]
    u = _gelu_tanh(z[:, 0:GROUP_W])
    v = _gelu_tanh(z[:, GROUP_W:2 * GROUP_W])
    mu = _seg_sum(v, ones_bd) * inv
    vc = v - mu
    var = _seg_sum(vc * vc, ones_bd) * inv
    vn = (vc * lax.rsqrt(var + LN_EPS) * ln_g + ln_b).astype(BF16)
    lane = lax.broadcasted_iota(jnp.int32, (CHUNK, GROUP_W), 1)
    out = []
    for c in range(tm // CHUNK):
        vchunk = vn[c * CHUNK:(c + 1) * CHUNK, :]
        sv = bias
        for g in range(4):
            part = jnp.dot(ws_ref[g], vchunk, preferred_element_type=F32)
            sv = sv + jnp.where((lane >= g * HEAD) & (lane < (g + 1) * HEAD), part, 0.0)
        out.append(u[c * CHUNK:(c + 1) * CHUNK, :] * sv)
    return jnp.concatenate(out, axis=0)


def _fnet_kernel(z_ref, cs_ref, cc_ref, sc_ref, w_ref, b_ref, o_ref, zcs_ref, *, l, scale, nbg):
    c = z_ref.shape[2]

    @pl.when(pl.program_id(1) == 0)
    def _():
        for j in range(nbg):
            zb = z_ref[j].astype(BF16)
            zcs_ref[0:l, j * c:(j + 1) * c] = jnp.dot(zb, cc_ref[...], preferred_element_type=F32).astype(BF16)
            zcs_ref[l:2 * l, j * c:(j + 1) * c] = jnp.dot(zb, sc_ref[...], preferred_element_type=F32).astype(BF16)

    f = jnp.dot(cs_ref[...], zcs_ref[...], preferred_element_type=F32) * scale
    for j in range(nbg):
        o_ref[j] = jnp.dot(f[:, j * c:(j + 1) * c].astype(BF16), w_ref[...], preferred_element_type=F32) + b_ref[...]


def _fourier_mixer(z, cs, cc_bd, sc_bd, w_bd, bias, tm, row0, l, nbg=4):
    b, _, c = z.shape
    assert b % nbg == 0 and row0 % l == 0
    full = lambda shape: pl.BlockSpec(shape, lambda bi, i: (0,) * len(shape))
    scale = 1.0 / math.sqrt(l * HEAD)
    vmem = 2 * (nbg * l * c * 4 + tm * 2 * l * 2 + nbg * tm * c * 4) + 2 * l * nbg * c * 2 + 4 * l * c * 4
    return pl.pallas_call(
        functools.partial(_fnet_kernel, l=l, scale=scale, nbg=nbg),
        grid=(b // nbg, l // tm),
        in_specs=[pl.BlockSpec((nbg, l, c), lambda bi, i: (bi, row0 // l, 0)),
                  pl.BlockSpec((tm, 2 * l), lambda bi, i: (i, 0)),
                  full(cc_bd.shape), full(sc_bd.shape), full(w_bd.shape), full(bias.shape)],
        out_specs=pl.BlockSpec((nbg, tm, c), lambda bi, i: (bi, i, 0)),
        out_shape=jax.ShapeDtypeStruct((b, l, c), F32),
        scratch_shapes=[pltpu.VMEM((2 * l, nbg * c), BF16)],
        compiler_params=_cparams(("parallel", "arbitrary"), vmem),
        name="fourier_mixer",
    )(z, cs, cc_bd, sc_bd, w_bd, bias)


def _outproj_kernel(*refs, alpha, add_pos, tm, seq_len):
    (y_ref, r_ref, v_ref, kd_ref, gate_ref, zg_ref, zp_ref, zpp_ref, zpn_ref, f_ref, x_ref), refs = refs[:11], refs[11:]
    if add_pos:
        pos_ref, refs = refs[0], refs[1:]
        x = x_ref[0] + pos_ref[...]
    else:
        x = x_ref[0]
    (mod_ref, rk_ref, gg_ref, gb_ref, ones_ref, lng_ref, lnb_ref, ws_ref, bias_ref, pw_ref, ps_ref, w_ref, lg_ref,
     lb_ref, o_ref) = refs
    i = pl.program_id(0)
    ones_bd = ones_ref[...]
    a = _readout(y_ref[0], r_ref[0], v_ref[0], kd_ref[0], gate_ref[0], rk_ref[...], gg_ref[...], gb_ref[...], ones_bd)
    prev_rows = zpp_ref[0] * (i > 0).astype(F32)
    next_rows = zpn_ref[0] * (i < pl.num_programs(0) - 1).astype(F32)
    p = _pool(zp_ref[0], prev_rows, next_rows, i * tm, seq_len, pw_ref[...], ps_ref[...])
    g = _gmlp(zg_ref[0], lng_ref[...], lnb_ref[...], ws_ref, bias_ref[...], ones_bd)
    mix = None
    for j, part in enumerate((a, p, g, f_ref[0])):
        prod = jnp.dot(part.astype(BF16), w_ref[j * GROUP_W:(j + 1) * GROUP_W, :], preferred_element_type=F32)
        mix = prod if mix is None else mix + prod
    gate1 = mod_ref[0, 2:3, :]
    o_ref[0] = _layer_norm(alpha * x + gate1 * mix, lg_ref[...], lb_ref[...])


def _mix_out_proj(y, shared, pair, z_gmlp, z_pool, row0, fourier, x, pos, mod, params, alpha, tm):
    b, l, d = x.shape
    add_pos = pos is not None
    off = row0 // tm
    comb = lambda width, col: pl.BlockSpec((1, tm, width), lambda i, bi: (bi, i + off, col))
    s1 = pl.BlockSpec((1, tm, GROUP_W), lambda i, bi: (bi, i, 0))
    xspec = pl.BlockSpec((1, tm, d), lambda i, bi: (bi, i, 0))
    full = lambda shape: pl.BlockSpec(shape, lambda i, bi: (0,) * len(shape))
    h0, h1, hr = row0 // POOL_HALO, (row0 + l) // POOL_HALO - 1, tm // POOL_HALO
    halo = lambda f: pl.BlockSpec((1, POOL_HALO, GROUP_W), lambda i, bi: (bi, jnp.clip(f(i), h0, h1), 0))
    in_specs = [comb(GROUP_W, 0), comb(GROUP_W, SH_R), comb(GROUP_W, SH_V), comb(2 * GROUP_W, PR_KD // 2),
                comb(GROUP_W, SH_GATE), comb(2 * GROUP_W, 0), comb(GROUP_W, 0),
                halo(lambda i: h0 + i * hr - 1), halo(lambda i: h0 + (i + 1) * hr), s1, xspec]
    args = [y, shared, shared, pair, shared, z_gmlp, z_pool, z_pool, z_pool, fourier, x]
    if add_pos:
        in_specs.append(pl.BlockSpec((tm, d), lambda i, bi: (i, 0)))
        args.append(pos)
    in_specs += [pl.BlockSpec((1, 6, d), lambda i, bi: (bi, 0, 0))] + [full(p.shape) for p in params]
    args += [mod, *params]
    vmem = 2 * (10 * tm * GROUP_W * 4 + 3 * tm * d * 4 + d * d * 2) + 6 * tm * d * 4 + 24 * tm * GROUP_W * 4
    return pl.pallas_call(
        functools.partial(_outproj_kernel, alpha=alpha, add_pos=add_pos, tm=tm, seq_len=l),
        grid=(l // tm, b),
        in_specs=in_specs,
        out_specs=xspec,
        out_shape=jax.ShapeDtypeStruct((b, l, d), F32),
        compiler_params=_cparams(("parallel", "parallel"), vmem),
        name="mix_out_proj_ln",
    )(*args)


def _ffn_kernel(x_ref, mod_ref, wg_ref, wu_ref, w2_ref, lg_ref, lb_ref, o_ref, *, alpha, f_chunks):
    x = x_ref[0]
    sh = mod_ref[0, 3:4, :]
    sc = mod_ref[0, 4:5, :]
    h = (x * (1.0 + sc) + sh).astype(BF16)
    acc = None
    for lo, hi in f_chunks:
        gate = jnp.dot(h, wg_ref[:, lo:hi], preferred_element_type=F32)
        up = jnp.dot(h, wu_ref[:, lo:hi], preferred_element_type=F32)
        act = (_silu(gate) * up).astype(BF16)
        part = jnp.dot(act, w2_ref[lo:hi, :], preferred_element_type=F32)
        acc = part if acc is None else acc + part
    gate2 = mod_ref[0, 5:6, :]
    o_ref[0] = _layer_norm(alpha * x + gate2 * acc, lg_ref[...], lb_ref[...])


def _ffn(x, mod, w1, w2, ln_g, ln_b, alpha, tm):
    b, l, d = x.shape
    f = w2.shape[0]
    mxu_n = 256
    half = (f // mxu_n + 1) // 2 * mxu_n
    f_chunks = ((0, half), (half, f))
    xspec = pl.BlockSpec((1, tm, d), lambda bi, i: (bi, i, 0))
    full = lambda shape: pl.BlockSpec(shape, lambda bi, i: (0,) * len(shape))
    once = pl.Buffered(1)
    vmem = 4 * tm * d * 4 + 3 * d * f * 2 + 3 * tm * d * 4 + 4 * tm * half * 4
    return pl.pallas_call(
        functools.partial(_ffn_kernel, alpha=alpha, f_chunks=f_chunks),
        grid=(b, l // tm),
        in_specs=[xspec,
                  pl.BlockSpec((1, 6, d), lambda bi, i: (bi, 0, 0)),
                  pl.BlockSpec((d, f), lambda bi, i: (0, 0), pipeline_mode=once),
                  pl.BlockSpec((d, f), lambda bi, i: (0, 1), pipeline_mode=once),
                  pl.BlockSpec((f, d), lambda bi, i: (0, 0), pipeline_mode=once),
                  full(ln_g.shape), full(ln_b.shape)],
        out_specs=xspec,
        out_shape=jax.ShapeDtypeStruct((b, l, d), F32),
        compiler_params=_cparams(("parallel", "parallel"), vmem),
        name="ffn_ln",
    )(x, mod, w1, w1, w2, ln_g, ln_b)


def _pos_embed(n_tok, dim):
    rows = n_tok // GRID_W
    row, col = jnp.meshgrid(jnp.arange(rows, dtype=F32), jnp.arange(GRID_W, dtype=F32), indexing='ij')
    quarter = dim // 4
    freqs = jnp.exp(-math.log(10000.0) * jnp.arange(quarter, dtype=F32) / quarter)

    def enc(p):
        ang = p.reshape(-1, 1) * freqs[None, :]
        return jnp.concatenate([jnp.sin(ang), jnp.cos(ang)], -1)

    return jnp.concatenate([enc(row), enc(col)], -1)


def _block_diag(blocks):
    g, n, m = blocks.shape
    eye = jnp.eye(g, dtype=blocks.dtype)
    return (eye[:, None, :, None] * blocks[:, :, None, :]).reshape(g * n, g * m)


def _dft_tables(n):
    def direct(rows, cols, period):
        ang = ((rows[:, None] * cols[None, :]) % period).astype(F32) * (2.0 * math.pi / period)
        return jnp.cos(ang), jnp.sin(ang)

    idx = jnp.arange(n, dtype=jnp.int32)
    if n <= 1024:
        return direct(idx, idx, n)
    lo = 64
    ca, sa = direct(jnp.arange(n // lo, dtype=jnp.int32), idx, n // lo)
    cb, sb = direct(jnp.arange(lo, dtype=jnp.int32), idx, n)
    ca, sa, cb, sb = ca[:, None, :], sa[:, None, :], cb[None], sb[None]
    return (ca * cb - sa * sb).reshape(n, n), (sa * cb + ca * sb).reshape(n, n)


def kernel(x, c, ctx, c_ctx, w_mod, b_mod, w_in, rkv_conv, decay_w0, decay_w2, iclr_a0, iclr_a2,
           gate_g2, k_k, k_a, r_k, gn_g, gn_b, pool_w, pool_scale, gmlp_ln_g, gmlp_ln_b, gmlp_ws,
           gmlp_bs, fnet_w, fnet_b, w_out, ln1_g, ln1_b, ln2_g, ln2_b, ffn_w1, ffn_w2):
    bsz, seq, d = x.shape
    ctx_len = ctx.shape[1]
    depth = w_in.shape[0]
    alpha = (2 * depth) ** 0.25
    assert bsz * N_HEADS * 2 == V7X_LANES, "scan layout packs (direction, batch, head) on the 128 lanes"

    pos = _pos_embed(seq, d)
    ones_bd = _block_diag(jnp.ones((N_HEADS, HEAD, HEAD), F32)).astype(BF16)
    cc64, ss64 = _dft_tables(HEAD)
    cc_bd = _block_diag(jnp.tile(cc64[None], (4, 1, 1))).astype(BF16)
    sc_bd = _block_diag(jnp.tile(ss64[None], (4, 1, 1))).astype(BF16)

    def dft_rows(n):
        cn, sn = _dft_tables(n)
        return jnp.concatenate([cn, -sn], axis=1).astype(BF16)

    cs_x = dft_rows(seq)
    cs_c = dft_rows(ctx_len)

    pad_rows = (-(bsz + 1)) % V7X_SUBLANES
    c_all = jnp.concatenate([c, c_ctx[None], jnp.zeros((pad_rows, d), F32)], axis=0)

    tm_x = 512
    tm_c = ctx_len
    tb = 64

    xs, cs = x, ctx
    for l in range(depth):
        last = l == depth - 1
        first = l == 0
        m = _modulation(c_all, w_mod[l].astype(BF16), b_mod[l])
        mod_x = m[:bsz].reshape(bsz, 6, d)
        mod_c = jnp.broadcast_to(m[bsz].reshape(1, 6, d), (bsz, 6, d))

        wl = jnp.zeros((LORA_W, 5 * GROUP_W), F32)
        wl = wl.at[0:64, 0:256].set(decay_w2[l, 0]).at[64:128, 256:512].set(decay_w2[l, 1])
        wl = wl.at[128:192, 512:768].set(iclr_a2[l, 0]).at[192:256, 768:1024].set(iclr_a2[l, 1])
        wl = wl.at[256:384, 1024:1280].set(gate_g2[l]).astype(BF16)
        prep_args = (rkv_conv[l], wl, decay_w0[l], iclr_a0[l], k_k[l].reshape(1, -1), k_a[l].reshape(1, -1),
                     ones_bd)
        shared, pair, z_pool, z_gmlp, z_fnet = _proj_prep(xs, cs, pos if first else None, mod_x, mod_c,
                                                          w_in[l].astype(BF16), *prep_args)
        to_scan = lambda a, col_f, col_b, **kw: _to_scan_layout(a, seq, ctx_len, col_f, col_b, **kw)
        y_s = _rwkv_scan(to_scan(shared, SH_R, SH_R), to_scan(shared, SH_V, SH_V, step_major=True),
                         to_scan(shared, SH_KK, SH_KK), to_scan(pair, PR_LW, PR_LW + 1),
                         to_scan(pair, PR_B, PR_B + 1), to_scan(pair, PR_KD, PR_KD + 1), tb)
        y_nat = _from_scan_layout(y_s, bsz, seq, ctx_len)

        pool_bd = _block_diag(pool_w[l]).astype(BF16)
        fnet_bd = _block_diag(fnet_w[l]).astype(BF16)
        ws_b = gmlp_ws[l].astype(BF16)
        bias_tile = jnp.repeat(gmlp_bs[l].T, HEAD, axis=1)
        row = lambda a: a.reshape(1, -1)

        mix_params = (row(r_k[l]), row(gn_g[l]), row(gn_b[l]), ones_bd, row(gmlp_ln_g[l]), row(gmlp_ln_b[l]), ws_b,
                      bias_tile, pool_bd, row(pool_scale[l]), w_out[l].astype(BF16), row(ln1_g[l]), row(ln1_b[l]))

        def mix_sublayer(row0, n_rows, stream, pos_tab, mod, cs_tab, tm, tmf):
            fourier = _fourier_mixer(z_fnet, cs_tab, cc_bd, sc_bd, fnet_bd, row(fnet_b[l]), tmf, row0, n_rows)
            return _mix_out_proj(y_nat, shared, pair, z_gmlp, z_pool, row0, fourier, stream, pos_tab, mod,
                                 mix_params, alpha, tm)

        w1_l = ffn_w1[l].astype(BF16)
        w2_l = ffn_w2[l].astype(BF16)
        xs = mix_sublayer(0, seq, xs, pos if first else None, mod_x, cs_x, tm_x, 256)
        xs = _ffn(xs, mod_x, w1_l, w2_l, row(ln2_g[l]), row(ln2_b[l]), alpha, tm_x)
        if not last:
            cs = mix_sublayer(seq, ctx_len, cs, None, mod_c, cs_c, tm_c, tm_c)
            cs = _ffn(cs, mod_c, w1_l, w2_l, row(ln2_g[l]), row(ln2_b[l]), alpha, tm_c)
    return xs
```

```python
import functools
import math

import jax
import jax.numpy as jnp
from jax import lax
from jax.experimental import pallas as pl
from jax.experimental.pallas import tpu as pltpu

F32 = jnp.float32
BF16 = jnp.bfloat16

V7X_LANES = 128
V7X_SUBLANES = 8
V7X_VMEM_BYTES = 64 * 1024 * 1024
V7X_VMEM_LIMIT_CAP = V7X_VMEM_BYTES - 8 * 1024 * 1024

GRID_W = 64
HEAD = 64
GROUP_W = 256
N_HEADS = GROUP_W // HEAD
LORA_W = 384
POOL_WINDOWS = (2, 4, 8, 16)
CHUNK = 128
LN_EPS = 1e-5
GN_EPS = 64e-5


def _cparams(sem, vmem_bytes):
    limit = int(min(max(vmem_bytes, 16 * 1024 * 1024), V7X_VMEM_LIMIT_CAP))
    return pltpu.CompilerParams(dimension_semantics=sem, vmem_limit_bytes=limit)


def _silu(x):
    return x * jax.nn.sigmoid(x)


def _seg_sum(x, ones_bd):
    hi = x.astype(BF16)
    lo = (x - hi.astype(F32)).astype(BF16)
    return (jnp.dot(hi, ones_bd, preferred_element_type=F32)
            + jnp.dot(lo, ones_bd, preferred_element_type=F32))


def _layer_norm(z, g, b):
    mu = jnp.mean(z, axis=-1, keepdims=True)
    zc = z - mu
    var = jnp.mean(zc * zc, axis=-1, keepdims=True)
    return zc * lax.rsqrt(var + LN_EPS) * g + b


def _mod_kernel(c_ref, w_ref, b_ref, o_ref):
    s = _silu(c_ref[...])
    o_ref[...] = jnp.dot(s.astype(BF16), w_ref[...], preferred_element_type=F32) + b_ref[...]


def _modulation(c_all, w_mod, b_mod):
    rows, d = c_all.shape
    n = w_mod.shape[1]
    tn = 1024
    return pl.pallas_call(
        _mod_kernel,
        grid=(n // tn,),
        in_specs=[pl.BlockSpec((rows, d), lambda j: (0, 0)),
                  pl.BlockSpec((d, tn), lambda j: (0, j)),
                  pl.BlockSpec((1, tn), lambda j: (0, j))],
        out_specs=pl.BlockSpec((rows, tn), lambda j: (0, j)),
        out_shape=jax.ShapeDtypeStruct((rows, n), F32),
        compiler_params=_cparams(("parallel",), 4 * d * tn * 2),
        name="modulation",
    )(c_all, w_mod, b_mod.reshape(1, n))


SH_R, SH_V, SH_KK, SH_GATE = range(4)
PR_LW, PR_B, PR_KD = 0, 2, 4
RKV_W = 3 * GROUP_W


def _proj_prep_kernel(*refs, tm, nx, add_pos):
    (xx_ref, xp_ref, xn_ref, xc_ref), refs = refs[:4], refs[4:]
    xx, xp, xn = xx_ref[0], xp_ref[0], xn_ref[0]
    if add_pos:
        (pos_ref, pp_ref, pn_ref), refs = refs[:3], refs[3:]
        xx, xp, xn = xx + pos_ref[...], xp + pp_ref[...], xn + pn_ref[...]
    (modx_ref, modc_ref, w_ref, conv_ref, wl_ref, w0_ref, a0_ref, kk_ref, ka_ref, ones_ref,
     shared_o, pair_o, pool_o, gmlp_o, fnet_o) = refs
    i = pl.program_id(1)
    is_ctx = i >= nx
    x = jnp.where(is_ctx, xc_ref[0], xx)
    mod = jnp.where(is_ctx, modc_ref[0, 0:2, :], modx_ref[0, 0:2, :])
    adaln = lambda t: (t * (1.0 + mod[1:2]) + mod[0:1]).astype(BF16)
    proj = lambda t, lo, hi: jnp.dot(t, w_ref[:, lo:hi], preferred_element_type=F32)
    h = adaln(x)
    z = proj(h, 0, RKV_W)
    lo = proj(h, RKV_W, RKV_W + LORA_W)
    off = RKV_W + LORA_W
    pool_o[0] = proj(h, off, off + GROUP_W)
    gmlp_o[0] = proj(h, off + GROUP_W, off + 3 * GROUP_W)
    fnet_o[0] = proj(h, off + 3 * GROUP_W, off + 4 * GROUP_W)

    has_prev = jnp.logical_and(i > 0, i < nx).astype(F32)
    has_next = (i < nx - 1).astype(F32)
    prev_row = proj(adaln(xp), 0, RKV_W)[V7X_SUBLANES - 1:V7X_SUBLANES, :] * has_prev
    next_row = proj(adaln(xn), 0, RKV_W)[0:1, :] * has_next
    row = lax.broadcasted_iota(jnp.int32, z.shape, 0)
    zm1 = jnp.where(row == 0, prev_row, pltpu.roll(z, 1, 0))
    zp1 = jnp.where(row == tm - 1, next_row, pltpu.roll(z, tm - 1, 0))
    c = zm1 * conv_ref[0:1, :] + z * conv_ref[1:2, :] + zp1 * conv_ref[2:3, :]
    r = c[:, 0:GROUP_W]
    k = c[:, GROUP_W:2 * GROUP_W]
    v = c[:, 2 * GROUP_W:3 * GROUP_W]

    col = lax.broadcasted_iota(jnp.int32, lo.shape, 1)
    act = jnp.where(col < 128, jnp.tanh(lo), jnp.where(col < 256, lo, jax.nn.sigmoid(lo)))
    pre = jnp.dot(act.astype(BF16), wl_ref[...], preferred_element_type=F32)

    ones_bd = ones_ref[...]
    kk = k * kk_ref[...]
    nrm = jnp.sqrt(_seg_sum(kk * kk, ones_bd))
    kk = kk / jnp.maximum(nrm, 1e-12)
    blk = lambda j: slice(j * GROUP_W, (j + 1) * GROUP_W)
    for j, val in ((SH_R, r), (SH_V, v), (SH_KK, kk), (SH_GATE, pre[:, 4 * GROUP_W:5 * GROUP_W])):
        shared_o[0, :, blk(j)] = val
    ka = ka_ref[...]
    for d in range(2):
        xw = w0_ref[d:d + 1, :] + pre[:, blk(d)]
        pair_o[0, :, blk(PR_LW + d)] = -math.exp(-0.5) * jax.nn.sigmoid(xw)
        a = jax.nn.sigmoid(a0_ref[d:d + 1, :] + pre[:, blk(2 + d)])
        pair_o[0, :, blk(PR_KD + d)] = k * (1.0 + (a - 1.0) * ka)
        pair_o[0, :, blk(PR_B + d)] = kk * a


def _proj_prep(x, ctx, pos, mod_x, mod_c, w_in, conv, wl, w0, a0, k_k, k_a, ones_bd):
    b, l, d = x.shape
    tm = ctx.shape[1]
    nx = l // tm
    nblk8 = l // V7X_SUBLANES
    r8 = tm // V7X_SUBLANES
    add_pos = pos is not None
    full = lambda shape: pl.BlockSpec(shape, lambda bi, i: (0,) * len(shape))
    xi = lambda i: jnp.minimum(i, nx - 1)
    prev8 = lambda i: jnp.clip(i * r8 - 1, 0, nblk8 - 1)
    next8 = lambda i: jnp.minimum((i + 1) * r8, nblk8 - 1)
    in_specs = [pl.BlockSpec((1, tm, d), lambda bi, i: (bi, xi(i), 0)),
                pl.BlockSpec((1, V7X_SUBLANES, d), lambda bi, i: (bi, prev8(i), 0)),
                pl.BlockSpec((1, V7X_SUBLANES, d), lambda bi, i: (bi, next8(i), 0)),
                pl.BlockSpec((1, tm, d), lambda bi, i: (bi, 0, 0))]
    args = [x, x, x, ctx]
    if add_pos:
        in_specs += [pl.BlockSpec((tm, d), lambda bi, i: (xi(i), 0)),
                     pl.BlockSpec((V7X_SUBLANES, d), lambda bi, i: (prev8(i), 0)),
                     pl.BlockSpec((V7X_SUBLANES, d), lambda bi, i: (next8(i), 0))]
        args += [pos, pos, pos]
    mod_spec = pl.BlockSpec((1, 6, d), lambda bi, i: (bi, 0, 0))
    params = (w_in, conv, wl, w0, a0, k_k, k_a, ones_bd)
    in_specs += [mod_spec, mod_spec] + [full(p.shape) for p in params]
    args += [mod_x, mod_c, *params]
    widths = (4 * GROUP_W, 6 * GROUP_W, GROUP_W, 2 * GROUP_W, GROUP_W)
    n = w_in.shape[1]
    vmem = 2 * (4 * tm * d * 4 + d * n * 2 + tm * sum(widths) * 4) + 3 * tm * d * 4 + 16 * tm * RKV_W * 4
    return pl.pallas_call(
        functools.partial(_proj_prep_kernel, tm=tm, nx=nx, add_pos=add_pos),
        grid=(b, nx + 1),
        in_specs=in_specs,
        out_specs=[pl.BlockSpec((1, tm, wd), lambda bi, i: (bi, i, 0)) for wd in widths],
        out_shape=[jax.ShapeDtypeStruct((b, l + tm, wd), F32) for wd in widths],
        compiler_params=_cparams(("parallel", "arbitrary"), vmem),
        name="proj_prep",
    )(*args)


SCAN_ROW_PITCH = 72


def _to_scan_kernel(xf_ref, xb_ref, o_ref, y_ref, *, nb, tt, step_major):
    rev = (tt - 1) - lax.broadcasted_iota(jnp.int32, (GROUP_W, tt), 1)
    for b in range(nb):
        xf = xf_ref[b].T
        xb = jnp.take_along_axis(xb_ref[b].T, rev, axis=1)
        for h in range(N_HEADS):
            y_ref[pl.ds((b * 2 * N_HEADS + h) * SCAN_ROW_PITCH, HEAD), :] = xf[h * HEAD:(h + 1) * HEAD, :]
            y_ref[pl.ds((b * 2 * N_HEADS + N_HEADS + h) * SCAN_ROW_PITCH, HEAD), :] = xb[h * HEAD:(h + 1) * HEAD, :]
    for k in range(HEAD):
        z = y_ref[pl.ds(k, nb * 2 * N_HEADS, stride=SCAN_ROW_PITCH), :]
        if step_major:
            o_ref[pl.ds(k, tt, stride=HEAD), :] = z.T
        else:
            o_ref[k] = z.T


def _to_scan_layout(x, n_lat, n_ctx, fwd_cols, bwd_cols, step_major=False):
    nb, t, c = x.shape
    tt = V7X_LANES
    assert nb * 2 * N_HEADS == V7X_LANES and c % GROUP_W == 0 and n_lat % tt == 0 and n_ctx % tt == 0
    nl, nc = n_lat // tt, n_ctx // tt
    n = nl + nc
    if step_major:
        out_spec = pl.BlockSpec((tt * HEAD, V7X_LANES), lambda j: (j, 0))
        out_shape = jax.ShapeDtypeStruct((t * HEAD, V7X_LANES), F32)
    else:
        out_spec = pl.BlockSpec((HEAD, tt, V7X_LANES), lambda j: (0, j, 0))
        out_shape = jax.ShapeDtypeStruct((HEAD, t, V7X_LANES), F32)
    out = pl.pallas_call(
        functools.partial(_to_scan_kernel, nb=nb, tt=tt, step_major=step_major),
        grid=(n,),
        in_specs=[pl.BlockSpec((nb, tt, GROUP_W), lambda j: (0, jnp.where(j < nc, nl + j, j - nc), fwd_cols)),
                  pl.BlockSpec((nb, tt, GROUP_W), lambda j: (0, n - 1 - j, bwd_cols))],
        out_specs=out_spec,
        out_shape=out_shape,
        scratch_shapes=[pltpu.VMEM((V7X_LANES * SCAN_ROW_PITCH, tt), F32)],
        compiler_params=_cparams(("parallel",), 2 * (nb * tt * 2 * GROUP_W * 4 + HEAD * tt * V7X_LANES * 4)
                                 + 3 * V7X_LANES * SCAN_ROW_PITCH * tt * 4),
        name="to_scan_layout",
    )(x, x)
    return out.reshape(t, HEAD, V7X_LANES) if step_major else out


def _from_scan_kernel(yf_ref, yb_ref, o_ref, s_ref, *, nb, tt):
    rev = (tt - 1) - lax.broadcasted_iota(jnp.int32, (V7X_LANES, tt), 1)
    fwd_row = (lax.broadcasted_iota(jnp.int32, (V7X_LANES, tt), 0) & N_HEADS) == 0
    for v in range(HEAD):
        zf = yf_ref[pl.ds(v, tt, stride=SCAN_ROW_PITCH), :].T
        zb = jnp.take_along_axis(yb_ref[pl.ds(v, tt, stride=SCAN_ROW_PITCH), :].T, rev, axis=1)
        s_ref[pl.ds(v, V7X_LANES, stride=SCAN_ROW_PITCH), :] = jnp.where(fwd_row, zf, zb)
    groups = 2 * N_HEADS
    for b in range(nb):
        grp = lambda g: s_ref[pl.ds((b * groups + g) * SCAN_ROW_PITCH, HEAD), :]
        rows = [grp(h) + grp(N_HEADS + h) for h in range(N_HEADS)]
        o_ref[b] = jnp.concatenate(rows, axis=0).T


def _from_scan_layout(y, nb, n_lat, n_ctx):
    t, pitch, _ = y.shape
    assert pitch == SCAN_ROW_PITCH
    tt = V7X_LANES
    nl, nc = n_lat // tt, n_ctx // tt
    n = nl + nc
    y2d = y.reshape(t * pitch, V7X_LANES)
    blk = pl.BlockSpec((tt * pitch, V7X_LANES), lambda j: (jnp.where(j < nl, j + nc, j - nl), 0))
    rblk = pl.BlockSpec((tt * pitch, V7X_LANES), lambda j: (n - 1 - j, 0))
    return pl.pallas_call(
        functools.partial(_from_scan_kernel, nb=nb, tt=tt),
        grid=(n,),
        in_specs=[blk, rblk],
        out_specs=pl.BlockSpec((nb, tt, GROUP_W), lambda j: (0, j, 0)),
        out_shape=jax.ShapeDtypeStruct((nb, t, GROUP_W), F32),
        scratch_shapes=[pltpu.VMEM((V7X_LANES * SCAN_ROW_PITCH, tt), F32)],
        compiler_params=_cparams(("parallel",), 2 * (2 * tt * pitch * V7X_LANES * 4 + nb * tt * 2 * GROUP_W * 4)
                                 + 3 * V7X_LANES * SCAN_ROW_PITCH * tt * 4),
        name="from_scan_layout",
    )(y2d, y2d)


def _scan_kernel(r_ref, v_ref, kk_ref, lw_ref, b_ref, kd_ref, y_ref, h_ref, s_ref, g_ref, *, tb):
    @pl.when(pl.program_id(0) == 0)
    def _():
        h_ref[...] = jnp.zeros_like(h_ref)

    tile = (V7X_SUBLANES, V7X_LANES)
    R_G, KK_G, B_G, KD_G = range(4)

    sub = lax.broadcasted_iota(jnp.int32, tile, 0)

    def scale(k, carry):
        lw = lw_ref[k]
        parts = []
        total = jnp.zeros(tile, F32)
        for j in range(tb // V7X_SUBLANES):
            x = lw[j * V7X_SUBLANES:(j + 1) * V7X_SUBLANES, :]
            for s in (1, 2, 4):
                x = x + jnp.where(sub >= s, pltpu.roll(x, s, 0), 0.0)
            x = x + total
            total = jnp.broadcast_to(x[V7X_SUBLANES - 1:V7X_SUBLANES, :], tile)
            parts.append(x)
        cum = jnp.concatenate(parts, axis=0)
        g = jnp.exp(cum)
        g_inv = jnp.exp(-cum)
        s_ref[R_G, k, 0:tb] = r_ref[k] * g
        s_ref[KK_G, k, 0:tb] = kk_ref[k] * jnp.exp(cum - lw)
        s_ref[B_G, k, 0:tb] = b_ref[k] * g_inv
        s_ref[KD_G, k, 0:tb] = kd_ref[k] * g_inv
        g_ref[pl.ds(k, 1), :] = g[tb - 1:tb, :]
        return carry

    lax.fori_loop(0, HEAD, scale, 0)

    u0 = jnp.zeros((HEAD, V7X_LANES), F32)
    for k in range(HEAD):
        u0 = u0 + s_ref[KK_G, k, 0:1, :] * h_ref[k]

    def step(t, u):
        tn = jnp.minimum(t + 1, tb - 1)
        vv = v_ref[t]
        y = jnp.zeros((HEAD, V7X_LANES), F32)
        un = jnp.zeros((HEAD, V7X_LANES), F32)
        for k in range(HEAD):
            hn = h_ref[k] - s_ref[B_G, k, pl.ds(t, 1), :] * u + s_ref[KD_G, k, pl.ds(t, 1), :] * vv
            h_ref[k] = hn
            y = y + s_ref[R_G, k, pl.ds(t, 1), :] * hn
            un = un + s_ref[KK_G, k, pl.ds(tn, 1), :] * hn
        y_ref[t, 0:HEAD] = y
        return un

    y_ref[:, HEAD:, :] = jnp.zeros((tb, SCAN_ROW_PITCH - HEAD, V7X_LANES), F32)
    lax.fori_loop(0, tb, step, u0)
    for k in range(HEAD):
        h_ref[k] = h_ref[k] * g_ref[k:k + 1, :]


def _rwkv_scan(r, v, kk, w, b, kd, tb):
    t_tot = r.shape[1]
    kspec = pl.BlockSpec((HEAD, tb, V7X_LANES), lambda i: (0, i, 0))
    sspec = pl.BlockSpec((tb, HEAD, V7X_LANES), lambda i: (i, 0, 0))
    blk = tb * HEAD * V7X_LANES * 4
    return pl.pallas_call(
        functools.partial(_scan_kernel, tb=tb),
        grid=(t_tot // tb,),
        in_specs=[kspec, sspec, kspec, kspec, kspec, kspec],
        out_specs=pl.BlockSpec((tb, SCAN_ROW_PITCH, V7X_LANES), lambda i: (i, 0, 0)),
        out_shape=jax.ShapeDtypeStruct((t_tot, SCAN_ROW_PITCH, V7X_LANES), F32),
        scratch_shapes=[pltpu.VMEM((HEAD, HEAD, V7X_LANES), F32),
                        pltpu.VMEM((4, HEAD, tb, V7X_LANES), F32),
                        pltpu.VMEM((HEAD, V7X_LANES), F32)],
        compiler_params=_cparams(("arbitrary",), 14 * blk + 5 * blk + 3 * HEAD * HEAD * V7X_LANES * 4),
        name="rwkv_scan",
    )(r, v, kk, w, b, kd)


def _readout(y, r, v, kd2, gate, r_k, gn_g, gn_b, ones_bd):
    inv = 1.0 / HEAD
    kd = kd2[:, 0:GROUP_W] + kd2[:, GROUP_W:2 * GROUP_W]
    mu = _seg_sum(y, ones_bd) * inv
    yc = y - mu
    var = _seg_sum(yc * yc, ones_bd) * inv
    yn = yc * lax.rsqrt(var + GN_EPS)
    bonus = _seg_sum(r * kd * r_k, ones_bd) * v
    return (yn * gn_g + gn_b + bonus) * gate


POOL_HALO = 8


def _pool(z, prev_rows, next_rows, t0, seq_len, w_bd, scale):
    assert POOL_WINDOWS == (2, 4, 8, 16) and POOL_HALO == max(POOL_WINDOWS) // 2
    tm = z.shape[0]
    n = tm + 2 * POOL_HALO
    e = jnp.concatenate([prev_rows, z, next_rows], axis=0)
    shifted = lambda x, o: pltpu.roll(x, (-o) % n, 0)
    core = lambda x: x[POOL_HALO:POOL_HALO + tm, :]
    pos = t0 + lax.broadcasted_iota(jnp.int32, z.shape, 0)
    lane = lax.broadcasted_iota(jnp.int32, z.shape, 1)

    def count(wd):
        lo = jnp.maximum(pos - wd // 2, 0)
        hi = jnp.minimum(pos + wd - wd // 2, seq_len)
        return (hi - lo).astype(F32)

    s2 = shifted(e, -1) + e
    pooled = core(s2) / count(2)
    s4 = shifted(s2, -1) + shifted(s2, 1)
    pooled = jnp.where(lane >= 64, core(s4) / count(4), pooled)
    s8 = shifted(s4, -2) + shifted(s4, 2)
    pooled = jnp.where(lane >= 128, core(s8) / count(8), pooled)
    s16 = shifted(s8, -4) + shifted(s8, 4)
    pooled = jnp.where(lane >= 192, core(s16) / count(16), pooled)
    d = pooled - z
    return jnp.dot(d.astype(BF16), w_bd, preferred_element_type=F32) * scale


def _gelu_tanh(x):
    return 0.5 * x * (1.0 + jnp.tanh(math.sqrt(2.0 / math.pi) * (x + 0.044715 * (x * x * x))))


def _gmlp(z, ln_g, ln_b, ws_ref, bias, ones_bd):
    inv = 1.0 / HEAD
    tm = z.shape[0]
    u = _gelu_tanh(z[:, 0:GROUP_W])
    v = _gelu_tanh(z[:, GROUP_W:2 * GROUP_W])
    mu = _seg_sum(v, ones_bd) * inv
    vc = v - mu
    var = _seg_sum(vc * vc, ones_bd) * inv
    vn = (vc * lax.rsqrt(var + LN_EPS) * ln_g + ln_b).astype(BF16)
    lane = lax.broadcasted_iota(jnp.int32, (CHUNK, GROUP_W), 1)
    out = []
    for c in range(tm // CHUNK):
        vchunk = vn[c * CHUNK:(c + 1) * CHUNK, :]
        sv = bias
        for g in range(4):
            part = jnp.dot(ws_ref[g], vchunk, preferred_element_type=F32)
            sv = sv + jnp.where((lane >= g * HEAD) & (lane < (g + 1) * HEAD), part, 0.0)
        out.append(u[c * CHUNK:(c + 1) * CHUNK, :] * sv)
    return jnp.concatenate(out, axis=0)


def _fnet_kernel(z_ref, cs_ref, cc_ref, sc_ref, w_ref, b_ref, o_ref, zcs_ref, *, l, scale, nbg):
    c = z_ref.shape[2]

    @pl.when(pl.program_id(1) == 0)
    def _():
        for j in range(nbg):
            zb = z_ref[j].astype(BF16)
            zcs_ref[0:l, j * c:(j + 1) * c] = jnp.dot(zb, cc_ref[...], preferred_element_type=F32).astype(BF16)
            zcs_ref[l:2 * l, j * c:(j + 1) * c] = jnp.dot(zb, sc_ref[...], preferred_element_type=F32).astype(BF16)

    f = jnp.dot(cs_ref[...], zcs_ref[...], preferred_element_type=F32) * scale
    for j in range(nbg):
        o_ref[j] = jnp.dot(f[:, j * c:(j + 1) * c].astype(BF16), w_ref[...], preferred_element_type=F32) + b_ref[...]


def _fourier_mixer(z, cs, cc_bd, sc_bd, w_bd, bias, tm, row0, l, nbg=4):
    b, _, c = z.shape
    assert b % nbg == 0 and row0 % l == 0
    full = lambda shape: pl.BlockSpec(shape, lambda bi, i: (0,) * len(shape))
    scale = 1.0 / math.sqrt(l * HEAD)
    vmem = 2 * (nbg * l * c * 4 + tm * 2 * l * 2 + nbg * tm * c * 4) + 2 * l * nbg * c * 2 + 4 * l * c * 4
    return pl.pallas_call(
        functools.partial(_fnet_kernel, l=l, scale=scale, nbg=nbg),
        grid=(b // nbg, l // tm),
        in_specs=[pl.BlockSpec((nbg, l, c), lambda bi, i: (bi, row0 // l, 0)),
                  pl.BlockSpec((tm, 2 * l), lambda bi, i: (i, 0)),
                  full(cc_bd.shape), full(sc_bd.shape), full(w_bd.shape), full(bias.shape)],
        out_specs=pl.BlockSpec((nbg, tm, c), lambda bi, i: (bi, i, 0)),
        out_shape=jax.ShapeDtypeStruct((b, l, c), F32),
        scratch_shapes=[pltpu.VMEM((2 * l, nbg * c), BF16)],
        compiler_params=_cparams(("parallel", "arbitrary"), vmem),
        name="fourier_mixer",
    )(z, cs, cc_bd, sc_bd, w_bd, bias)


def _outproj_kernel(*refs, alpha, add_pos, tm, seq_len, f_chunks):
    (y_ref, r_ref, v_ref, kd_ref, gate_ref, zg_ref, zp_ref, zpp_ref, zpn_ref, f_ref, x_ref), refs = refs[:11], refs[11:]
    if add_pos:
        pos_ref, refs = refs[0], refs[1:]
        x = x_ref[0] + pos_ref[...]
    else:
        x = x_ref[0]
    (mod_ref, rk_ref, gg_ref, gb_ref, ones_ref, lng_ref, lnb_ref, ws_ref, bias_ref, pw_ref, ps_ref, w_ref, lg_ref,
     lb_ref, wg_ref, wu_ref, w2_ref, l2g_ref, l2b_ref, o_ref) = refs
    i = pl.program_id(0)
    ones_bd = ones_ref[...]
    a = _readout(y_ref[0], r_ref[0], v_ref[0], kd_ref[0], gate_ref[0], rk_ref[...], gg_ref[...], gb_ref[...], ones_bd)
    prev_rows = zpp_ref[0] * (i > 0).astype(F32)
    next_rows = zpn_ref[0] * (i < pl.num_programs(0) - 1).astype(F32)
    p = _pool(zp_ref[0], prev_rows, next_rows, i * tm, seq_len, pw_ref[...], ps_ref[...])
    g = _gmlp(zg_ref[0], lng_ref[...], lnb_ref[...], ws_ref, bias_ref[...], ones_bd)
    mix = None
    for j, part in enumerate((a, p, g, f_ref[0])):
        prod = jnp.dot(part.astype(BF16), w_ref[j * GROUP_W:(j + 1) * GROUP_W, :], preferred_element_type=F32)
        mix = prod if mix is None else mix + prod
    gate1 = mod_ref[0, 2:3, :]
    x1 = _layer_norm(alpha * x + gate1 * mix, lg_ref[...], lb_ref[...])

    h = (x1 * (1.0 + mod_ref[0, 4:5, :]) + mod_ref[0, 3:4, :]).astype(BF16)
    acc = None
    for lo, hi in f_chunks:
        gate = jnp.dot(h, wg_ref[:, lo:hi], preferred_element_type=F32)
        up = jnp.dot(h, wu_ref[:, lo:hi], preferred_element_type=F32)
        act = (_silu(gate) * up).astype(BF16)
        part = jnp.dot(act, w2_ref[lo:hi, :], preferred_element_type=F32)
        acc = part if acc is None else acc + part
    gate2 = mod_ref[0, 5:6, :]
    o_ref[0] = _layer_norm(alpha * x1 + gate2 * acc, l2g_ref[...], l2b_ref[...])


def _mix_out_proj(y, shared, pair, z_gmlp, z_pool, row0, fourier, x, pos, mod, params, ffn_params, alpha, tm):
    b, l, d = x.shape
    add_pos = pos is not None
    off = row0 // tm
    comb = lambda width, col: pl.BlockSpec((1, tm, width), lambda i, bi: (bi, i + off, col))
    s1 = pl.BlockSpec((1, tm, GROUP_W), lambda i, bi: (bi, i, 0))
    xspec = pl.BlockSpec((1, tm, d), lambda i, bi: (bi, i, 0))
    full = lambda shape: pl.BlockSpec(shape, lambda i, bi: (0,) * len(shape))
    h0, h1, hr = row0 // POOL_HALO, (row0 + l) // POOL_HALO - 1, tm // POOL_HALO
    halo = lambda f: pl.BlockSpec((1, POOL_HALO, GROUP_W), lambda i, bi: (bi, jnp.clip(f(i), h0, h1), 0))
    in_specs = [comb(GROUP_W, 0), comb(GROUP_W, SH_R), comb(GROUP_W, SH_V), comb(2 * GROUP_W, PR_KD // 2),
                comb(GROUP_W, SH_GATE), comb(2 * GROUP_W, 0), comb(GROUP_W, 0),
                halo(lambda i: h0 + i * hr - 1), halo(lambda i: h0 + (i + 1) * hr), s1, xspec]
    args = [y, shared, shared, pair, shared, z_gmlp, z_pool, z_pool, z_pool, fourier, x]
    if add_pos:
        in_specs.append(pl.BlockSpec((tm, d), lambda i, bi: (i, 0)))
        args.append(pos)
    in_specs += [pl.BlockSpec((1, 6, d), lambda i, bi: (bi, 0, 0))] + [full(p.shape) for p in params]
    args += [mod, *params]
    w1, w2, ln2_g, ln2_b = ffn_params
    f = w2.shape[0]
    mxu_n = 256
    half = (f // mxu_n + 1) // 2 * mxu_n
    f_chunks = ((0, half), (half, f))
    once = pl.Buffered(1)
    in_specs += [pl.BlockSpec((d, f), lambda i, bi: (0, 0), pipeline_mode=once),
                 pl.BlockSpec((d, f), lambda i, bi: (0, 1), pipeline_mode=once),
                 pl.BlockSpec((f, d), lambda i, bi: (0, 0), pipeline_mode=once),
                 full(ln2_g.shape), full(ln2_b.shape)]
    args += [w1, w1, w2, ln2_g, ln2_b]
    vmem = (2 * (10 * tm * GROUP_W * 4 + 3 * tm * d * 4 + d * d * 2) + 8 * tm * d * 4 + 24 * tm * GROUP_W * 4
            + 3 * d * f * 2 + 4 * tm * half * 4)
    return pl.pallas_call(
        functools.partial(_outproj_kernel, alpha=alpha, add_pos=add_pos, tm=tm, seq_len=l, f_chunks=f_chunks),
        grid=(l // tm, b),
        in_specs=in_specs,
        out_specs=xspec,
        out_shape=jax.ShapeDtypeStruct((b, l, d), F32),
        compiler_params=_cparams(("parallel", "parallel"), vmem),
        name="mix_ffn",
    )(*args)


def _pos_embed(n_tok, dim):
    rows = n_tok // GRID_W
    row, col = jnp.meshgrid(jnp.arange(rows, dtype=F32), jnp.arange(GRID_W, dtype=F32), indexing='ij')
    quarter = dim // 4
    freqs = jnp.exp(-math.log(10000.0) * jnp.arange(quarter, dtype=F32) / quarter)

    def enc(p):
        ang = p.reshape(-1, 1) * freqs[None, :]
        return jnp.concatenate([jnp.sin(ang), jnp.cos(ang)], -1)

    return jnp.concatenate([enc(row), enc(col)], -1)


def _block_diag(blocks):
    g, n, m = blocks.shape
    eye = jnp.eye(g, dtype=blocks.dtype)
    return (eye[:, None, :, None] * blocks[:, :, None, :]).reshape(g * n, g * m)


def _dft_tables(n):
    def direct(rows, cols, period):
        ang = ((rows[:, None] * cols[None, :]) % period).astype(F32) * (2.0 * math.pi / period)
        return jnp.cos(ang), jnp.sin(ang)

    idx = jnp.arange(n, dtype=jnp.int32)
    if n <= 1024:
        return direct(idx, idx, n)
    lo = 64
    ca, sa = direct(jnp.arange(n // lo, dtype=jnp.int32), idx, n // lo)
    cb, sb = direct(jnp.arange(lo, dtype=jnp.int32), idx, n)
    ca, sa, cb, sb = ca[:, None, :], sa[:, None, :], cb[None], sb[None]
    return (ca * cb - sa * sb).reshape(n, n), (sa * cb + ca * sb).reshape(n, n)


def kernel(x, c, ctx, c_ctx, w_mod, b_mod, w_in, rkv_conv, decay_w0, decay_w2, iclr_a0, iclr_a2,
           gate_g2, k_k, k_a, r_k, gn_g, gn_b, pool_w, pool_scale, gmlp_ln_g, gmlp_ln_b, gmlp_ws,
           gmlp_bs, fnet_w, fnet_b, w_out, ln1_g, ln1_b, ln2_g, ln2_b, ffn_w1, ffn_w2):
    bsz, seq, d = x.shape
    ctx_len = ctx.shape[1]
    depth = w_in.shape[0]
    alpha = (2 * depth) ** 0.25
    assert bsz * N_HEADS * 2 == V7X_LANES, "scan layout packs (direction, batch, head) on the 128 lanes"

    pos = _pos_embed(seq, d)
    ones_bd = _block_diag(jnp.ones((N_HEADS, HEAD, HEAD), F32)).astype(BF16)
    cc64, ss64 = _dft_tables(HEAD)
    cc_bd = _block_diag(jnp.tile(cc64[None], (4, 1, 1))).astype(BF16)
    sc_bd = _block_diag(jnp.tile(ss64[None], (4, 1, 1))).astype(BF16)

    def dft_rows(n):
        cn, sn = _dft_tables(n)
        return jnp.concatenate([cn, -sn], axis=1).astype(BF16)

    cs_x = dft_rows(seq)
    cs_c = dft_rows(ctx_len)

    pad_rows = (-(bsz + 1)) % V7X_SUBLANES
    c_all = jnp.concatenate([c, c_ctx[None], jnp.zeros((pad_rows, d), F32)], axis=0)

    tm_x = 512
    tm_c = ctx_len
    tb = 64

    xs, cs = x, ctx
    for l in range(depth):
        last = l == depth - 1
        first = l == 0
        m = _modulation(c_all, w_mod[l].astype(BF16), b_mod[l])
        mod_x = m[:bsz].reshape(bsz, 6, d)
        mod_c = jnp.broadcast_to(m[bsz].reshape(1, 6, d), (bsz, 6, d))

        wl = jnp.zeros((LORA_W, 5 * GROUP_W), F32)
        wl = wl.at[0:64, 0:256].set(decay_w2[l, 0]).at[64:128, 256:512].set(decay_w2[l, 1])
        wl = wl.at[128:192, 512:768].set(iclr_a2[l, 0]).at[192:256, 768:1024].set(iclr_a2[l, 1])
        wl = wl.at[256:384, 1024:1280].set(gate_g2[l]).astype(BF16)
        prep_args = (rkv_conv[l], wl, decay_w0[l], iclr_a0[l], k_k[l].reshape(1, -1), k_a[l].reshape(1, -1),
                     ones_bd)
        shared, pair, z_pool, z_gmlp, z_fnet = _proj_prep(xs, cs, pos if first else None, mod_x, mod_c,
                                                          w_in[l].astype(BF16), *prep_args)
        to_scan = lambda a, col_f, col_b, **kw: _to_scan_layout(a, seq, ctx_len, col_f, col_b, **kw)
        y_s = _rwkv_scan(to_scan(shared, SH_R, SH_R), to_scan(shared, SH_V, SH_V, step_major=True),
                         to_scan(shared, SH_KK, SH_KK), to_scan(pair, PR_LW, PR_LW + 1),
                         to_scan(pair, PR_B, PR_B + 1), to_scan(pair, PR_KD, PR_KD + 1), tb)
        y_nat = _from_scan_layout(y_s, bsz, seq, ctx_len)

        pool_bd = _block_diag(pool_w[l]).astype(BF16)
        fnet_bd = _block_diag(fnet_w[l]).astype(BF16)
        ws_b = gmlp_ws[l].astype(BF16)
        bias_tile = jnp.repeat(gmlp_bs[l].T, HEAD, axis=1)
        row = lambda a: a.reshape(1, -1)

        mix_params = (row(r_k[l]), row(gn_g[l]), row(gn_b[l]), ones_bd, row(gmlp_ln_g[l]), row(gmlp_ln_b[l]), ws_b,
                      bias_tile, pool_bd, row(pool_scale[l]), w_out[l].astype(BF16), row(ln1_g[l]), row(ln1_b[l]))

        ffn_params = (ffn_w1[l].astype(BF16), ffn_w2[l].astype(BF16), row(ln2_g[l]), row(ln2_b[l]))

        def sublayers(row0, n_rows, stream, pos_tab, mod, cs_tab, tm, tmf):
            fourier = _fourier_mixer(z_fnet, cs_tab, cc_bd, sc_bd, fnet_bd, row(fnet_b[l]), tmf, row0, n_rows)
            return _mix_out_proj(y_nat, shared, pair, z_gmlp, z_pool, row0, fourier, stream, pos_tab, mod,
                                 mix_params, ffn_params, alpha, tm)

        xs = sublayers(0, seq, xs, pos if first else None, mod_x, cs_x, tm_x, 256)
        if not last:
            cs = sublayers(seq, ctx_len, cs, None, mod_c, cs_c, tm_c, tm_c)
    return xs
```

```python
import functools
import math

import jax
import jax.numpy as jnp
from jax import lax
from jax.experimental import pallas as pl
from jax.experimental.pallas import tpu as pltpu

F32 = jnp.float32
BF16 = jnp.bfloat16

V7X_LANES = 128
V7X_SUBLANES = 8
V7X_VMEM_BYTES = 64 * 1024 * 1024
V7X_VMEM_LIMIT_CAP = V7X_VMEM_BYTES - 8 * 1024 * 1024

GRID_W = 64
HEAD = 64
GROUP_W = 256
N_HEADS = GROUP_W // HEAD
LORA_W = 384
POOL_WINDOWS = (2, 4, 8, 16)
CHUNK = 128
LN_EPS = 1e-5
GN_EPS = 64e-5


def _cparams(sem, vmem_bytes):
    limit = int(min(max(vmem_bytes, 16 * 1024 * 1024), V7X_VMEM_LIMIT_CAP))
    return pltpu.CompilerParams(dimension_semantics=sem, vmem_limit_bytes=limit)


def _silu(x):
    return x * jax.nn.sigmoid(x)


def _seg_sum(x, ones_bd):
    hi = x.astype(BF16)
    lo = (x - hi.astype(F32)).astype(BF16)
    return (jnp.dot(hi, ones_bd, preferred_element_type=F32)
            + jnp.dot(lo, ones_bd, preferred_element_type=F32))


def _layer_norm(z, g, b):
    mu = jnp.mean(z, axis=-1, keepdims=True)
    zc = z - mu
    var = jnp.mean(zc * zc, axis=-1, keepdims=True)
    return zc * lax.rsqrt(var + LN_EPS) * g + b


def _mod_kernel(c_ref, w_ref, b_ref, o_ref):
    s = _silu(c_ref[...])
    o_ref[...] = jnp.dot(s.astype(BF16), w_ref[...], preferred_element_type=F32) + b_ref[...]


def _modulation(c_all, w_mod, b_mod):
    rows, d = c_all.shape
    n = w_mod.shape[1]
    tn = 1024
    return pl.pallas_call(
        _mod_kernel,
        grid=(n // tn,),
        in_specs=[pl.BlockSpec((rows, d), lambda j: (0, 0)),
                  pl.BlockSpec((d, tn), lambda j: (0, j)),
                  pl.BlockSpec((1, tn), lambda j: (0, j))],
        out_specs=pl.BlockSpec((rows, tn), lambda j: (0, j)),
        out_shape=jax.ShapeDtypeStruct((rows, n), F32),
        compiler_params=_cparams(("parallel",), 4 * d * tn * 2),
        name="modulation",
    )(c_all, w_mod, b_mod.reshape(1, n))


SH_R, SH_V, SH_KK, SH_GATE = range(4)
PR_LW, PR_B, PR_KD = 0, 2, 4
RKV_W = 3 * GROUP_W


def _proj_prep_kernel(*refs, tm, nx, add_pos):
    (xx_ref, xp_ref, xn_ref, xc_ref), refs = refs[:4], refs[4:]
    xx, xp, xn = xx_ref[0], xp_ref[0], xn_ref[0]
    if add_pos:
        (pos_ref, pp_ref, pn_ref), refs = refs[:3], refs[3:]
        xx, xp, xn = xx + pos_ref[...], xp + pp_ref[...], xn + pn_ref[...]
    (modx_ref, modc_ref, w_ref, conv_ref, wl_ref, w0_ref, a0_ref, kk_ref, ka_ref, ones_ref,
     shared_o, pair_o, pool_o, gmlp_o, fnet_o) = refs
    i = pl.program_id(1)
    is_ctx = i >= nx
    x = jnp.where(is_ctx, xc_ref[0], xx)
    mod = jnp.where(is_ctx, modc_ref[0, 0:2, :], modx_ref[0, 0:2, :])
    adaln = lambda t: (t * (1.0 + mod[1:2]) + mod[0:1]).astype(BF16)
    proj = lambda t, lo, hi: jnp.dot(t, w_ref[:, lo:hi], preferred_element_type=F32)
    h = adaln(x)
    z = proj(h, 0, RKV_W)
    lo = proj(h, RKV_W, RKV_W + LORA_W)
    off = RKV_W + LORA_W
    pool_o[0] = proj(h, off, off + GROUP_W)
    gmlp_o[0] = proj(h, off + GROUP_W, off + 3 * GROUP_W)
    fnet_o[0] = proj(h, off + 3 * GROUP_W, off + 4 * GROUP_W)

    has_prev = jnp.logical_and(i > 0, i < nx).astype(F32)
    has_next = (i < nx - 1).astype(F32)
    prev_row = proj(adaln(xp), 0, RKV_W)[V7X_SUBLANES - 1:V7X_SUBLANES, :] * has_prev
    next_row = proj(adaln(xn), 0, RKV_W)[0:1, :] * has_next
    row = lax.broadcasted_iota(jnp.int32, z.shape, 0)
    zm1 = jnp.where(row == 0, prev_row, pltpu.roll(z, 1, 0))
    zp1 = jnp.where(row == tm - 1, next_row, pltpu.roll(z, tm - 1, 0))
    c = zm1 * conv_ref[0:1, :] + z * conv_ref[1:2, :] + zp1 * conv_ref[2:3, :]
    r = c[:, 0:GROUP_W]
    k = c[:, GROUP_W:2 * GROUP_W]
    v = c[:, 2 * GROUP_W:3 * GROUP_W]

    col = lax.broadcasted_iota(jnp.int32, lo.shape, 1)
    act = jnp.where(col < 128, jnp.tanh(lo), jnp.where(col < 256, lo, jax.nn.sigmoid(lo)))
    pre = jnp.dot(act.astype(BF16), wl_ref[...], preferred_element_type=F32)

    ones_bd = ones_ref[...]
    kk = k * kk_ref[...]
    nrm = jnp.sqrt(_seg_sum(kk * kk, ones_bd))
    kk = kk / jnp.maximum(nrm, 1e-12)
    blk = lambda j: slice(j * GROUP_W, (j + 1) * GROUP_W)
    for j, val in ((SH_R, r), (SH_V, v), (SH_KK, kk), (SH_GATE, pre[:, 4 * GROUP_W:5 * GROUP_W])):
        shared_o[0, :, blk(j)] = val
    ka = ka_ref[...]
    for d in range(2):
        xw = w0_ref[d:d + 1, :] + pre[:, blk(d)]
        pair_o[0, :, blk(PR_LW + d)] = -math.exp(-0.5) * jax.nn.sigmoid(xw)
        a = jax.nn.sigmoid(a0_ref[d:d + 1, :] + pre[:, blk(2 + d)])
        pair_o[0, :, blk(PR_KD + d)] = k * (1.0 + (a - 1.0) * ka)
        pair_o[0, :, blk(PR_B + d)] = kk * a


def _proj_prep(x, ctx, pos, mod_x, mod_c, w_in, conv, wl, w0, a0, k_k, k_a, ones_bd):
    b, l, d = x.shape
    tm = ctx.shape[1]
    nx = l // tm
    nblk8 = l // V7X_SUBLANES
    r8 = tm // V7X_SUBLANES
    add_pos = pos is not None
    full = lambda shape: pl.BlockSpec(shape, lambda bi, i: (0,) * len(shape))
    xi = lambda i: jnp.minimum(i, nx - 1)
    prev8 = lambda i: jnp.clip(i * r8 - 1, 0, nblk8 - 1)
    next8 = lambda i: jnp.minimum((i + 1) * r8, nblk8 - 1)
    in_specs = [pl.BlockSpec((1, tm, d), lambda bi, i: (bi, xi(i), 0)),
                pl.BlockSpec((1, V7X_SUBLANES, d), lambda bi, i: (bi, prev8(i), 0)),
                pl.BlockSpec((1, V7X_SUBLANES, d), lambda bi, i: (bi, next8(i), 0)),
                pl.BlockSpec((1, tm, d), lambda bi, i: (bi, 0, 0))]
    args = [x, x, x, ctx]
    if add_pos:
        in_specs += [pl.BlockSpec((tm, d), lambda bi, i: (xi(i), 0)),
                     pl.BlockSpec((V7X_SUBLANES, d), lambda bi, i: (prev8(i), 0)),
                     pl.BlockSpec((V7X_SUBLANES, d), lambda bi, i: (next8(i), 0))]
        args += [pos, pos, pos]
    mod_spec = pl.BlockSpec((1, 6, d), lambda bi, i: (bi, 0, 0))
    params = (w_in, conv, wl, w0, a0, k_k, k_a, ones_bd)
    in_specs += [mod_spec, mod_spec] + [full(p.shape) for p in params]
    args += [mod_x, mod_c, *params]
    widths = (4 * GROUP_W, 6 * GROUP_W, GROUP_W, 2 * GROUP_W, GROUP_W)
    n = w_in.shape[1]
    vmem = 2 * (4 * tm * d * 4 + d * n * 2 + tm * sum(widths) * 4) + 3 * tm * d * 4 + 16 * tm * RKV_W * 4
    return pl.pallas_call(
        functools.partial(_proj_prep_kernel, tm=tm, nx=nx, add_pos=add_pos),
        grid=(b, nx + 1),
        in_specs=in_specs,
        out_specs=[pl.BlockSpec((1, tm, wd), lambda bi, i: (bi, i, 0)) for wd in widths],
        out_shape=[jax.ShapeDtypeStruct((b, l + tm, wd), F32) for wd in widths],
        compiler_params=_cparams(("parallel", "arbitrary"), vmem),
        name="proj_prep",
    )(*args)


SCAN_ROW_PITCH = 72


def _to_scan_kernel(xf_ref, xb_ref, o_ref, y_ref, *, nb, tt, step_major):
    rev = (tt - 1) - lax.broadcasted_iota(jnp.int32, (GROUP_W, tt), 1)
    for b in range(nb):
        xf = xf_ref[b].T
        xb = jnp.take_along_axis(xb_ref[b].T, rev, axis=1)
        for h in range(N_HEADS):
            y_ref[pl.ds((b * 2 * N_HEADS + h) * SCAN_ROW_PITCH, HEAD), :] = xf[h * HEAD:(h + 1) * HEAD, :]
            y_ref[pl.ds((b * 2 * N_HEADS + N_HEADS + h) * SCAN_ROW_PITCH, HEAD), :] = xb[h * HEAD:(h + 1) * HEAD, :]
    for k in range(HEAD):
        z = y_ref[pl.ds(k, nb * 2 * N_HEADS, stride=SCAN_ROW_PITCH), :]
        if step_major:
            o_ref[pl.ds(k, tt, stride=HEAD), :] = z.T
        else:
            o_ref[k] = z.T


def _to_scan_layout(x, n_lat, n_ctx, fwd_cols, bwd_cols, step_major=False):
    nb, t, c = x.shape
    tt = V7X_LANES
    assert nb * 2 * N_HEADS == V7X_LANES and c % GROUP_W == 0 and n_lat % tt == 0 and n_ctx % tt == 0
    nl, nc = n_lat // tt, n_ctx // tt
    n = nl + nc
    if step_major:
        out_spec = pl.BlockSpec((tt * HEAD, V7X_LANES), lambda j: (j, 0))
        out_shape = jax.ShapeDtypeStruct((t * HEAD, V7X_LANES), F32)
    else:
        out_spec = pl.BlockSpec((HEAD, tt, V7X_LANES), lambda j: (0, j, 0))
        out_shape = jax.ShapeDtypeStruct((HEAD, t, V7X_LANES), F32)
    out = pl.pallas_call(
        functools.partial(_to_scan_kernel, nb=nb, tt=tt, step_major=step_major),
        grid=(n,),
        in_specs=[pl.BlockSpec((nb, tt, GROUP_W), lambda j: (0, jnp.where(j < nc, nl + j, j - nc), fwd_cols)),
                  pl.BlockSpec((nb, tt, GROUP_W), lambda j: (0, n - 1 - j, bwd_cols))],
        out_specs=out_spec,
        out_shape=out_shape,
        scratch_shapes=[pltpu.VMEM((V7X_LANES * SCAN_ROW_PITCH, tt), F32)],
        compiler_params=_cparams(("parallel",), 2 * (nb * tt * 2 * GROUP_W * 4 + HEAD * tt * V7X_LANES * 4)
                                 + 3 * V7X_LANES * SCAN_ROW_PITCH * tt * 4),
        name="to_scan_layout",
    )(x, x)
    return out.reshape(t, HEAD, V7X_LANES) if step_major else out


def _from_scan_kernel(yf_ref, yb_ref, o_ref, s_ref, *, nb, tt):
    rev = (tt - 1) - lax.broadcasted_iota(jnp.int32, (V7X_LANES, tt), 1)
    fwd_row = (lax.broadcasted_iota(jnp.int32, (V7X_LANES, tt), 0) & N_HEADS) == 0
    for v in range(HEAD):
        zf = yf_ref[pl.ds(v, tt, stride=SCAN_ROW_PITCH), :].T
        zb = jnp.take_along_axis(yb_ref[pl.ds(v, tt, stride=SCAN_ROW_PITCH), :].T, rev, axis=1)
        s_ref[pl.ds(v, V7X_LANES, stride=SCAN_ROW_PITCH), :] = jnp.where(fwd_row, zf, zb)
    groups = 2 * N_HEADS
    for b in range(nb):
        grp = lambda g: s_ref[pl.ds((b * groups + g) * SCAN_ROW_PITCH, HEAD), :]
        rows = [grp(h) + grp(N_HEADS + h) for h in range(N_HEADS)]
        o_ref[b] = jnp.concatenate(rows, axis=0).T


def _from_scan_layout(y, nb, n_lat, n_ctx):
    t, pitch, _ = y.shape
    assert pitch == SCAN_ROW_PITCH
    tt = V7X_LANES
    nl, nc = n_lat // tt, n_ctx // tt
    n = nl + nc
    y2d = y.reshape(t * pitch, V7X_LANES)
    blk = pl.BlockSpec((tt * pitch, V7X_LANES), lambda j: (jnp.where(j < nl, j + nc, j - nl), 0))
    rblk = pl.BlockSpec((tt * pitch, V7X_LANES), lambda j: (n - 1 - j, 0))
    return pl.pallas_call(
        functools.partial(_from_scan_kernel, nb=nb, tt=tt),
        grid=(n,),
        in_specs=[blk, rblk],
        out_specs=pl.BlockSpec((nb, tt, GROUP_W), lambda j: (0, j, 0)),
        out_shape=jax.ShapeDtypeStruct((nb, t, GROUP_W), F32),
        scratch_shapes=[pltpu.VMEM((V7X_LANES * SCAN_ROW_PITCH, tt), F32)],
        compiler_params=_cparams(("parallel",), 2 * (2 * tt * pitch * V7X_LANES * 4 + nb * tt * 2 * GROUP_W * 4)
                                 + 3 * V7X_LANES * SCAN_ROW_PITCH * tt * 4),
        name="from_scan_layout",
    )(y2d, y2d)


def _scan_kernel(r_ref, v_ref, kk_ref, lw_ref, b_ref, kd_ref, y_ref, h_ref, s_ref, g_ref, *, tb):
    @pl.when(pl.program_id(0) == 0)
    def _():
        h_ref[...] = jnp.zeros_like(h_ref)

    tile = (V7X_SUBLANES, V7X_LANES)
    R_G, KK_G, B_G, KD_G = range(4)

    sub = lax.broadcasted_iota(jnp.int32, tile, 0)

    def scale(k, carry):
        lw = lw_ref[k]
        parts = []
        total = jnp.zeros(tile, F32)
        for j in range(tb // V7X_SUBLANES):
            x = lw[j * V7X_SUBLANES:(j + 1) * V7X_SUBLANES, :]
            for s in (1, 2, 4):
                x = x + jnp.where(sub >= s, pltpu.roll(x, s, 0), 0.0)
            x = x + total
            total = jnp.broadcast_to(x[V7X_SUBLANES - 1:V7X_SUBLANES, :], tile)
            parts.append(x)
        cum = jnp.concatenate(parts, axis=0)
        g = jnp.exp(cum)
        g_inv = jnp.exp(-cum)
        s_ref[R_G, k, 0:tb] = r_ref[k] * g
        s_ref[KK_G, k, 0:tb] = kk_ref[k] * jnp.exp(cum - lw)
        s_ref[B_G, k, 0:tb] = b_ref[k] * g_inv
        s_ref[KD_G, k, 0:tb] = kd_ref[k] * g_inv
        g_ref[pl.ds(k, 1), :] = g[tb - 1:tb, :]
        return carry

    lax.fori_loop(0, HEAD, scale, 0)

    u0 = jnp.zeros((HEAD, V7X_LANES), F32)
    for k in range(HEAD):
        u0 = u0 + s_ref[KK_G, k, 0:1, :] * h_ref[k]

    def step(t, u):
        tn = jnp.minimum(t + 1, tb - 1)
        vv = v_ref[t]
        y = jnp.zeros((HEAD, V7X_LANES), F32)
        un = jnp.zeros((HEAD, V7X_LANES), F32)
        for k in range(HEAD):
            hn = h_ref[k] - s_ref[B_G, k, pl.ds(t, 1), :] * u + s_ref[KD_G, k, pl.ds(t, 1), :] * vv
            h_ref[k] = hn
            y = y + s_ref[R_G, k, pl.ds(t, 1), :] * hn
            un = un + s_ref[KK_G, k, pl.ds(tn, 1), :] * hn
        y_ref[t, 0:HEAD] = y
        return un

    y_ref[:, HEAD:, :] = jnp.zeros((tb, SCAN_ROW_PITCH - HEAD, V7X_LANES), F32)
    lax.fori_loop(0, tb, step, u0)
    for k in range(HEAD):
        h_ref[k] = h_ref[k] * g_ref[k:k + 1, :]


def _rwkv_scan(r, v, kk, w, b, kd, tb):
    t_tot = r.shape[1]
    kspec = pl.BlockSpec((HEAD, tb, V7X_LANES), lambda i: (0, i, 0))
    sspec = pl.BlockSpec((tb, HEAD, V7X_LANES), lambda i: (i, 0, 0))
    blk = tb * HEAD * V7X_LANES * 4
    return pl.pallas_call(
        functools.partial(_scan_kernel, tb=tb),
        grid=(t_tot // tb,),
        in_specs=[kspec, sspec, kspec, kspec, kspec, kspec],
        out_specs=pl.BlockSpec((tb, SCAN_ROW_PITCH, V7X_LANES), lambda i: (i, 0, 0)),
        out_shape=jax.ShapeDtypeStruct((t_tot, SCAN_ROW_PITCH, V7X_LANES), F32),
        scratch_shapes=[pltpu.VMEM((HEAD, HEAD, V7X_LANES), F32),
                        pltpu.VMEM((4, HEAD, tb, V7X_LANES), F32),
                        pltpu.VMEM((HEAD, V7X_LANES), F32)],
        compiler_params=_cparams(("arbitrary",), 14 * blk + 5 * blk + 3 * HEAD * HEAD * V7X_LANES * 4),
        name="rwkv_scan",
    )(r, v, kk, w, b, kd)


def _readout(y, r, v, kd2, gate, r_k, gn_g, gn_b, ones_bd):
    inv = 1.0 / HEAD
    kd = kd2[:, 0:GROUP_W] + kd2[:, GROUP_W:2 * GROUP_W]
    mu = _seg_sum(y, ones_bd) * inv
    yc = y - mu
    var = _seg_sum(yc * yc, ones_bd) * inv
    yn = yc * lax.rsqrt(var + GN_EPS)
    bonus = _seg_sum(r * kd * r_k, ones_bd) * v
    return (yn * gn_g + gn_b + bonus) * gate


POOL_HALO = 8


def _pool(z, prev_rows, next_rows, t0, seq_len, w_bd, scale):
    assert POOL_WINDOWS == (2, 4, 8, 16) and POOL_HALO == max(POOL_WINDOWS) // 2
    tm = z.shape[0]
    n = tm + 2 * POOL_HALO
    e = jnp.concatenate([prev_rows, z, next_rows], axis=0)
    shifted = lambda x, o: pltpu.roll(x, (-o) % n, 0)
    core = lambda x: x[POOL_HALO:POOL_HALO + tm, :]
    pos = t0 + lax.broadcasted_iota(jnp.int32, z.shape, 0)
    lane = lax.broadcasted_iota(jnp.int32, z.shape, 1)

    def count(wd):
        lo = jnp.maximum(pos - wd // 2, 0)
        hi = jnp.minimum(pos + wd - wd // 2, seq_len)
        return (hi - lo).astype(F32)

    s2 = shifted(e, -1) + e
    pooled = core(s2) / count(2)
    s4 = shifted(s2, -1) + shifted(s2, 1)
    pooled = jnp.where(lane >= 64, core(s4) / count(4), pooled)
    s8 = shifted(s4, -2) + shifted(s4, 2)
    pooled = jnp.where(lane >= 128, core(s8) / count(8), pooled)
    s16 = shifted(s8, -4) + shifted(s8, 4)
    pooled = jnp.where(lane >= 192, core(s16) / count(16), pooled)
    d = pooled - z
    return jnp.dot(d.astype(BF16), w_bd, preferred_element_type=F32) * scale


def _gelu_tanh(x):
    return 0.5 * x * (1.0 + jnp.tanh(math.sqrt(2.0 / math.pi) * (x + 0.044715 * (x * x * x))))


def _gmlp(z, ln_g, ln_b, ws_ref, bias, ones_bd):
    inv = 1.0 / HEAD
    tm = z.shape[0]
    u = _gelu_tanh(z[:, 0:GROUP_W])
    v = _gelu_tanh(z[:, GROUP_W:2 * GROUP_W])
    mu = _seg_sum(v, ones_bd) * inv
    vc = v - mu
    var = _seg_sum(vc * vc, ones_bd) * inv
    vn = (vc * lax.rsqrt(var + LN_EPS) * ln_g + ln_b).astype(BF16)
    lane = lax.broadcasted_iota(jnp.int32, (CHUNK, GROUP_W), 1)
    out = []
    for c in range(tm // CHUNK):
        vchunk = vn[c * CHUNK:(c + 1) * CHUNK, :]
        sv = bias
        for g in range(4):
            part = jnp.dot(ws_ref[g], vchunk, preferred_element_type=F32)
            sv = sv + jnp.where((lane >= g * HEAD) & (lane < (g + 1) * HEAD), part, 0.0)
        out.append(u[c * CHUNK:(c + 1) * CHUNK, :] * sv)
    return jnp.concatenate(out, axis=0)


def _fnet_kernel(z_ref, cs_ref, cc_ref, sc_ref, w_ref, b_ref, o_ref, zcs_ref, *, l, scale, nbg):
    c = z_ref.shape[2]

    @pl.when(pl.program_id(1) == 0)
    def _():
        for j in range(nbg):
            zb = z_ref[j].astype(BF16)
            zcs_ref[0:l, j * c:(j + 1) * c] = jnp.dot(zb, cc_ref[...], preferred_element_type=F32).astype(BF16)
            zcs_ref[l:2 * l, j * c:(j + 1) * c] = jnp.dot(zb, sc_ref[...], preferred_element_type=F32).astype(BF16)

    f = jnp.dot(cs_ref[...], zcs_ref[...], preferred_element_type=F32) * scale
    for j in range(nbg):
        o_ref[j] = jnp.dot(f[:, j * c:(j + 1) * c].astype(BF16), w_ref[...], preferred_element_type=F32) + b_ref[...]


def _fourier_mixer(z, cs, cc_bd, sc_bd, w_bd, bias, tm, row0, l, nbg=4):
    b, _, c = z.shape
    assert b % nbg == 0 and row0 % l == 0
    full = lambda shape: pl.BlockSpec(shape, lambda bi, i: (0,) * len(shape))
    scale = 1.0 / math.sqrt(l * HEAD)
    vmem = 2 * (nbg * l * c * 4 + tm * 2 * l * 2 + nbg * tm * c * 4) + 2 * l * nbg * c * 2 + 4 * l * c * 4
    return pl.pallas_call(
        functools.partial(_fnet_kernel, l=l, scale=scale, nbg=nbg),
        grid=(b // nbg, l // tm),
        in_specs=[pl.BlockSpec((nbg, l, c), lambda bi, i: (bi, row0 // l, 0)),
                  pl.BlockSpec((tm, 2 * l), lambda bi, i: (i, 0)),
                  full(cc_bd.shape), full(sc_bd.shape), full(w_bd.shape), full(bias.shape)],
        out_specs=pl.BlockSpec((nbg, tm, c), lambda bi, i: (bi, i, 0)),
        out_shape=jax.ShapeDtypeStruct((b, l, c), F32),
        scratch_shapes=[pltpu.VMEM((2 * l, nbg * c), BF16)],
        compiler_params=_cparams(("parallel", "arbitrary"), vmem),
        name="fourier_mixer",
    )(z, cs, cc_bd, sc_bd, w_bd, bias)


def _outproj_kernel(*refs, alpha, add_pos, tm, seq_len):
    (y_ref, r_ref, v_ref, kd_ref, gate_ref, zg_ref, zp_ref, zpp_ref, zpn_ref, f_ref, x_ref), refs = refs[:11], refs[11:]
    if add_pos:
        pos_ref, refs = refs[0], refs[1:]
        x = x_ref[0] + pos_ref[...]
    else:
        x = x_ref[0]
    (mod_ref, rk_ref, gg_ref, gb_ref, ones_ref, lng_ref, lnb_ref, ws_ref, bias_ref, pw_ref, ps_ref, w_ref, lg_ref,
     lb_ref, o_ref) = refs
    i = pl.program_id(0)
    ones_bd = ones_ref[...]
    a = _readout(y_ref[0], r_ref[0], v_ref[0], kd_ref[0], gate_ref[0], rk_ref[...], gg_ref[...], gb_ref[...], ones_bd)
    prev_rows = zpp_ref[0] * (i > 0).astype(F32)
    next_rows = zpn_ref[0] * (i < pl.num_programs(0) - 1).astype(F32)
    p = _pool(zp_ref[0], prev_rows, next_rows, i * tm, seq_len, pw_ref[...], ps_ref[...])
    g = _gmlp(zg_ref[0], lng_ref[...], lnb_ref[...], ws_ref, bias_ref[...], ones_bd)
    mix = None
    for j, part in enumerate((a, p, g, f_ref[0])):
        prod = jnp.dot(part.astype(BF16), w_ref[j * GROUP_W:(j + 1) * GROUP_W, :], preferred_element_type=F32)
        mix = prod if mix is None else mix + prod
    gate1 = mod_ref[0, 2:3, :]
    o_ref[0] = _layer_norm(alpha * x + gate1 * mix, lg_ref[...], lb_ref[...])


def _mix_out_proj(y, shared, pair, z_gmlp, z_pool, row0, fourier, x, pos, mod, params, alpha, tm):
    b, l, d = x.shape
    add_pos = pos is not None
    off = row0 // tm
    comb = lambda width, col: pl.BlockSpec((1, tm, width), lambda i, bi: (bi, i + off, col))
    s1 = pl.BlockSpec((1, tm, GROUP_W), lambda i, bi: (bi, i, 0))
    xspec = pl.BlockSpec((1, tm, d), lambda i, bi: (bi, i, 0))
    full = lambda shape: pl.BlockSpec(shape, lambda i, bi: (0,) * len(shape))
    h0, h1, hr = row0 // POOL_HALO, (row0 + l) // POOL_HALO - 1, tm // POOL_HALO
    halo = lambda f: pl.BlockSpec((1, POOL_HALO, GROUP_W), lambda i, bi: (bi, jnp.clip(f(i), h0, h1), 0))
    in_specs = [comb(GROUP_W, 0), comb(GROUP_W, SH_R), comb(GROUP_W, SH_V), comb(2 * GROUP_W, PR_KD // 2),
                comb(GROUP_W, SH_GATE), comb(2 * GROUP_W, 0), comb(GROUP_W, 0),
                halo(lambda i: h0 + i * hr - 1), halo(lambda i: h0 + (i + 1) * hr), s1, xspec]
    args = [y, shared, shared, pair, shared, z_gmlp, z_pool, z_pool, z_pool, fourier, x]
    if add_pos:
        in_specs.append(pl.BlockSpec((tm, d), lambda i, bi: (i, 0)))
        args.append(pos)
    in_specs += [pl.BlockSpec((1, 6, d), lambda i, bi: (bi, 0, 0))] + [full(p.shape) for p in params]
    args += [mod, *params]
    vmem = 2 * (10 * tm * GROUP_W * 4 + 3 * tm * d * 4 + d * d * 2) + 6 * tm * d * 4 + 24 * tm * GROUP_W * 4
    return pl.pallas_call(
        functools.partial(_outproj_kernel, alpha=alpha, add_pos=add_pos, tm=tm, seq_len=l),
        grid=(l // tm, b),
        in_specs=in_specs,
        out_specs=xspec,
        out_shape=jax.ShapeDtypeStruct((b, l, d), F32),
        compiler_params=_cparams(("parallel", "parallel"), vmem),
        name="mix_out_proj_ln",
    )(*args)


def _ffn_kernel(x_ref, mod_ref, wg_ref, wu_ref, w2_ref, lg_ref, lb_ref, o_ref, *, alpha, f_chunks):
    x = x_ref[0]
    sh = mod_ref[0, 3:4, :]
    sc = mod_ref[0, 4:5, :]
    h = (x * (1.0 + sc) + sh).astype(BF16)
    acc = None
    for lo, hi in f_chunks:
        gate = jnp.dot(h, wg_ref[:, lo:hi], preferred_element_type=F32)
        up = jnp.dot(h, wu_ref[:, lo:hi], preferred_element_type=F32)
        act = (_silu(gate) * up).astype(BF16)
        part = jnp.dot(act, w2_ref[lo:hi, :], preferred_element_type=F32)
        acc = part if acc is None else acc + part
    gate2 = mod_ref[0, 5:6, :]
    o_ref[0] = _layer_norm(alpha * x + gate2 * acc, lg_ref[...], lb_ref[...])


def _ffn(x, mod, w1, w2, ln_g, ln_b, alpha, tm):
    b, l, d = x.shape
    f = w2.shape[0]
    mxu_n = 256
    half = (f // mxu_n + 1) // 2 * mxu_n
    f_chunks = ((0, half), (half, f))
    xspec = pl.BlockSpec((1, tm, d), lambda bi, i: (bi, i, 0))
    full = lambda shape: pl.BlockSpec(shape, lambda bi, i: (0,) * len(shape))
    once = pl.Buffered(1)
    vmem = 4 * tm * d * 4 + 3 * d * f * 2 + 3 * tm * d * 4 + 4 * tm * half * 4
    return pl.pallas_call(
        functools.partial(_ffn_kernel, alpha=alpha, f_chunks=f_chunks),
        grid=(b, l // tm),
        in_specs=[xspec,
                  pl.BlockSpec((1, 6, d), lambda bi, i: (bi, 0, 0)),
                  pl.BlockSpec((d, f), lambda bi, i: (0, 0), pipeline_mode=once),
                  pl.BlockSpec((d, f), lambda bi, i: (0, 1), pipeline_mode=once),
                  pl.BlockSpec((f, d), lambda bi, i: (0, 0), pipeline_mode=once),
                  full(ln_g.shape), full(ln_b.shape)],
        out_specs=xspec,
        out_shape=jax.ShapeDtypeStruct((b, l, d), F32),
        compiler_params=_cparams(("parallel", "parallel"), vmem),
        name="ffn_ln",
    )(x, mod, w1, w1, w2, ln_g, ln_b)


def _pos_embed(n_tok, dim):
    rows = n_tok // GRID_W
    row, col = jnp.meshgrid(jnp.arange(rows, dtype=F32), jnp.arange(GRID_W, dtype=F32), indexing='ij')
    quarter = dim // 4
    freqs = jnp.exp(-math.log(10000.0) * jnp.arange(quarter, dtype=F32) / quarter)

    def enc(p):
        ang = p.reshape(-1, 1) * freqs[None, :]
        return jnp.concatenate([jnp.sin(ang), jnp.cos(ang)], -1)

    return jnp.concatenate([enc(row), enc(col)], -1)


def _block_diag(blocks):
    g, n, m = blocks.shape
    eye = jnp.eye(g, dtype=blocks.dtype)
    return (eye[:, None, :, None] * blocks[:, :, None, :]).reshape(g * n, g * m)


def _dft_tables(n):
    def direct(rows, cols, period):
        ang = ((rows[:, None] * cols[None, :]) % period).astype(F32) * (2.0 * math.pi / period)
        return jnp.cos(ang), jnp.sin(ang)

    idx = jnp.arange(n, dtype=jnp.int32)
    if n <= 1024:
        return direct(idx, idx, n)
    lo = 64
    ca, sa = direct(jnp.arange(n // lo, dtype=jnp.int32), idx, n // lo)
    cb, sb = direct(jnp.arange(lo, dtype=jnp.int32), idx, n)
    ca, sa, cb, sb = ca[:, None, :], sa[:, None, :], cb[None], sb[None]
    return (ca * cb - sa * sb).reshape(n, n), (sa * cb + ca * sb).reshape(n, n)


def kernel(x, c, ctx, c_ctx, w_mod, b_mod, w_in, rkv_conv, decay_w0, decay_w2, iclr_a0, iclr_a2,
           gate_g2, k_k, k_a, r_k, gn_g, gn_b, pool_w, pool_scale, gmlp_ln_g, gmlp_ln_b, gmlp_ws,
           gmlp_bs, fnet_w, fnet_b, w_out, ln1_g, ln1_b, ln2_g, ln2_b, ffn_w1, ffn_w2):
    bsz, seq, d = x.shape
    ctx_len = ctx.shape[1]
    depth = w_in.shape[0]
    alpha = (2 * depth) ** 0.25
    assert bsz * N_HEADS * 2 == V7X_LANES, "scan layout packs (direction, batch, head) on the 128 lanes"

    pos = _pos_embed(seq, d)
    ones_bd = _block_diag(jnp.ones((N_HEADS, HEAD, HEAD), F32)).astype(BF16)
    cc64, ss64 = _dft_tables(HEAD)
    cc_bd = _block_diag(jnp.tile(cc64[None], (4, 1, 1))).astype(BF16)
    sc_bd = _block_diag(jnp.tile(ss64[None], (4, 1, 1))).astype(BF16)

    def dft_rows(n):
        cn, sn = _dft_tables(n)
        return jnp.concatenate([cn, -sn], axis=1).astype(BF16)

    cs_x = dft_rows(seq)
    cs_c = dft_rows(ctx_len)

    pad_rows = (-(bsz + 1)) % V7X_SUBLANES
    c_all = jnp.concatenate([c, c_ctx[None], jnp.zeros((pad_rows, d), F32)], axis=0)

    tm_x = 512
    tm_c = ctx_len
    tb = 64

    xs, cs = x, ctx
    for l in range(depth):
        last = l == depth - 1
        first = l == 0
        m = _modulation(c_all, w_mod[l].astype(BF16), b_mod[l])
        mod_x = m[:bsz].reshape(bsz, 6, d)
        mod_c = jnp.broadcast_to(m[bsz].reshape(1, 6, d), (bsz, 6, d))

        wl = jnp.zeros((LORA_W, 5 * GROUP_W), F32)
        wl = wl.at[0:64, 0:256].set(decay_w2[l, 0]).at[64:128, 256:512].set(decay_w2[l, 1])
        wl = wl.at[128:192, 512:768].set(iclr_a2[l, 0]).at[192:256, 768:1024].set(iclr_a2[l, 1])
        wl = wl.at[256:384, 1024:1280].set(gate_g2[l]).astype(BF16)
        prep_args = (rkv_conv[l], wl, decay_w0[l], iclr_a0[l], k_k[l].reshape(1, -1), k_a[l].reshape(1, -1),
                     ones_bd)
        shared, pair, z_pool, z_gmlp, z_fnet = _proj_prep(xs, cs, pos if first else None, mod_x, mod_c,
                                                          w_in[l].astype(BF16), *prep_args)
        to_scan = lambda a, col_f, col_b, **kw: _to_scan_layout(a, seq, ctx_len, col_f, col_b, **kw)
        y_s = _rwkv_scan(to_scan(shared, SH_R, SH_R), to_scan(shared, SH_V, SH_V, step_major=True),
                         to_scan(shared, SH_KK, SH_KK), to_scan(pair, PR_LW, PR_LW + 1),
                         to_scan(pair, PR_B, PR_B + 1), to_scan(pair, PR_KD, PR_KD + 1), tb)
        y_nat = _from_scan_layout(y_s, bsz, seq, ctx_len)

        pool_bd = _block_diag(pool_w[l]).astype(BF16)
        fnet_bd = _block_diag(fnet_w[l]).astype(BF16)
        ws_b = gmlp_ws[l].astype(BF16)
        bias_tile = jnp.repeat(gmlp_bs[l].T, HEAD, axis=1)
        row = lambda a: a.reshape(1, -1)

        mix_params = (row(r_k[l]), row(gn_g[l]), row(gn_b[l]), ones_bd, row(gmlp_ln_g[l]), row(gmlp_ln_b[l]), ws_b,
                      bias_tile, pool_bd, row(pool_scale[l]), w_out[l].astype(BF16), row(ln1_g[l]), row(ln1_b[l]))

        def mix_sublayer(row0, n_rows, stream, pos_tab, mod, cs_tab, tm, tmf):
            fourier = _fourier_mixer(z_fnet, cs_tab, cc_bd, sc_bd, fnet_bd, row(fnet_b[l]), tmf, row0, n_rows)
            return _mix_out_proj(y_nat, shared, pair, z_gmlp, z_pool, row0, fourier, stream, pos_tab, mod,
                                 mix_params, alpha, tm)

        w1_l = ffn_w1[l].astype(BF16)
        w2_l = ffn_w2[l].astype(BF16)
        xs = mix_sublayer(0, seq, xs, pos if first else None, mod_x, cs_x, tm_x, tm_x)
        xs = _ffn(xs, mod_x, w1_l, w2_l, row(ln2_g[l]), row(ln2_b[l]), alpha, tm_x)
        if not last:
            cs = mix_sublayer(seq, ctx_len, cs, None, mod_c, cs_c, tm_c, tm_c)
            cs = _ffn(cs, mod_c, w1_l, w2_l, row(ln2_g[l]), row(ln2_b[l]), alpha, tm_c)
    return xs
```
